```python
import jax
import jax.numpy as jnp
from jax import lax
import numpy as np

D_MODEL = 4096
BATCH = 4
SEQ = 4096
DEPTH = 2

CTX_LEN = 256
GRID_W = 64
EPS = 1e-6
NEG_INF = -1e30
N_BRANCH = 4
BRANCH_W = 1024
ATT_QBLOCK = 128
ROPE_THETA = 10000.0

GLA_HEADS = 4
GLA_DK = 128
GLA_DV = BRANCH_W // GLA_HEADS
GLA_RANK = 16
GLA_TAU = 16.0
GLA_CHUNK = 64

MLA_HEADS = 8
MLA_Q_RANK = 768
MLA_KV_RANK = 256
MLA_NOPE = 128
MLA_ROPE = 64
MLA_DV = BRANCH_W // MLA_HEADS

LRU_WIDTH = BRANCH_W
LRU_BLOCKS = 8
LRU_BW = LRU_WIDTH // LRU_BLOCKS
CONV_W = 4
LRU_C = 8.0

NA_HEADS = 8
NA_DH = BRANCH_W // NA_HEADS
NA_WIN_R = 8
NA_WIN_C = 16

PEER_HEADS = 8
PEER_NKEYS = 128
PEER_N = PEER_NKEYS * PEER_NKEYS
PEER_DQ = 256
PEER_TOPK = 16
PEER_BLOCK = 128

MIX_SIZES = (
    GLA_HEADS * GLA_DK,
    GLA_HEADS * GLA_DK,
    GLA_HEADS * GLA_DV,
    GLA_HEADS * GLA_DV,
    GLA_RANK,
    GLA_RANK,
    MLA_Q_RANK,
    MLA_KV_RANK,
    MLA_ROPE,
    LRU_WIDTH,
    LRU_WIDTH,
    NA_HEADS * NA_DH,
    NA_HEADS * NA_DH,
    NA_HEADS * NA_DH,
)
MIX_COLS = sum(MIX_SIZES)
IN_WIDTH = MIX_COLS + N_BRANCH * D_MODEL

kernel_name = 'hybrid_gla_mla_rglru_natten_peer_dit'


def _rmsnorm(x, g):
    xf = x.astype(jnp.float32)
    y = xf * lax.rsqrt(jnp.mean(xf * xf, axis=-1, keepdims=True) + EPS)
    return (y * g.astype(jnp.float32)).astype(x.dtype)


def _modulate(h, shift, scale):
    return h * (1 + scale) + shift


def _split_cols(z):
    idx = np.cumsum(MIX_SIZES)[:-1].tolist()
    return jnp.split(z, idx, axis=-1)


def _axial_rope(n_tok, dim):
    t = jnp.arange(n_tok, dtype=jnp.int32)
    row = (t // GRID_W).astype(jnp.float32)
    col = (t % GRID_W).astype(jnp.float32)
    n_freq = dim // 4
    inv = ROPE_THETA ** (-jnp.arange(n_freq, dtype=jnp.float32) / n_freq)
    ang = jnp.concatenate([row[:, None] * inv, col[:, None] * inv], axis=-1)
    return jnp.cos(ang), jnp.sin(ang)


def _apply_rope(x, cos, sin):
    xf = x.astype(jnp.float32)
    half = xf.shape[-1] // 2
    x1, x2 = xf[..., :half], xf[..., half:]
    cs, sn = cos[:, None, :], sin[:, None, :]
    return jnp.concatenate([x1 * cs - x2 * sn, x1 * sn + x2 * cs], axis=-1).astype(x.dtype)


def _block_attention(q, k, v):
    B, T, H, dq = q.shape
    scale = dq ** -0.5
    nb = T // ATT_QBLOCK
    qb = jnp.moveaxis(q.reshape(B, nb, ATT_QBLOCK, H, dq), 1, 0)

    def one(qblk):
        s = jnp.einsum('bqhd,bkhd->bhqk', qblk, k).astype(jnp.float32) * scale
        p = jax.nn.softmax(s, axis=-1).astype(v.dtype)
        return jnp.einsum('bhqk,bkhd->bqhd', p, v)

    o = lax.map(one, qb)
    return jnp.moveaxis(o, 0, 1).reshape(B, T, H, v.shape[-1])


def _gla_scan(q, k, v, log_a, s0):
    B, T, H, dk = q.shape
    dv = v.shape[-1]
    C = GLA_CHUNK
    n = T // C
    f32 = jnp.float32
    q = q.astype(f32).reshape(B, n, C, H, dk) * dk ** -0.5
    k = k.astype(f32).reshape(B, n, C, H, dk)
    v = v.astype(f32).reshape(B, n, C, H, dv)
    b = jnp.cumsum(log_a.astype(f32).reshape(B, n, C, H, dk), axis=2)
    b_end = b[:, :, -1:]
    q_dec = q * jnp.exp(b)
    k_inv = k * jnp.exp(-b)
    k_end = k * jnp.exp(b_end - b)
    lower = jnp.tril(jnp.ones((C, C), dtype=bool))
    att = jnp.where(lower, jnp.einsum('bnthk,bnshk->bnhts', q_dec, k_inv), 0.0)
    o_intra = jnp.einsum('bnhts,bnshv->bnthv', att, v)
    kv_chunk = jnp.einsum('bnshk,bnshv->nbhkv', k_end, v)
    decay_chunk = jnp.moveaxis(jnp.exp(b_end[:, :, 0]), 1, 0)

    def step(S, inp):
        d, kv = inp
        return d[..., None] * S + kv, S

    s_last, s_start = lax.scan(step, s0.astype(f32), (decay_chunk, kv_chunk))
    o_inter = jnp.einsum('bnthk,nbhkv->bnthv', q_dec, s_start)
    return (o_intra + o_inter).reshape(B, T, H, dv), s_last


def _gla_bidir(q, k, v, la_f, la_b, s0_f, s0_b):
    o_f, s_f = _gla_scan(q, k, v, la_f, s0_f)
    fl = lambda t: jnp.flip(t, axis=1)
    o_b, s_b = _gla_scan(fl(q), fl(k), fl(v), fl(la_b), s0_b)
    return o_f + fl(o_b), s_f, s_b


def _gla_mixer(parts_l, parts_c, wa2, ba, norm_g, update_ctx):
    def prep(parts):
        zq, zk, zv, zg, zaf, zab = parts
        B, T, _ = zq.shape
        la = [jax.nn.log_sigmoid((za @ wa2[i] + ba[i]).astype(jnp.float32)).reshape(B, T, GLA_HEADS, GLA_DK) / GLA_TAU
              for i, za in enumerate((zaf, zab))]
        return (zq.reshape(B, T, GLA_HEADS, GLA_DK), zk.reshape(B, T, GLA_HEADS, GLA_DK),
                zv.reshape(B, T, GLA_HEADS, GLA_DV), zg, la[0], la[1])

    def out(o, zg):
        B, T = zg.shape[:2]
        o = _rmsnorm(o, norm_g).reshape(B, T, GLA_HEADS * GLA_DV)
        return o.astype(zg.dtype) * jax.nn.silu(zg)

    qc, kc, vc, gc, lfc, lbc = prep(parts_c)
    zero = jnp.zeros((qc.shape[0], GLA_HEADS, GLA_DK, GLA_DV), jnp.float32)
    oc, s_f, s_b = _gla_bidir(qc, kc, vc, lfc, lbc, zero, zero)
    ql, kl, vl, gl, lfl, lbl = prep(parts_l)
    ol, _, _ = _gla_bidir(ql, kl, vl, lfl, lbl, s_f, s_b)
    y_c = out(oc, gc) if update_ctx else None
    return out(ol, gl), y_c


def _mla_mixer(parts_l, parts_c, q_norm_g, w_uq, kv_norm_g, w_ukv, rope_cs, update_ctx):
    cos, sin = rope_cs

    def qkv(parts, with_pos):
        cq, ckv, kr = parts
        B, T, _ = cq.shape
        q = (_rmsnorm(cq, q_norm_g) @ w_uq).reshape(B, T, MLA_HEADS, MLA_NOPE + MLA_ROPE)
        kv = (_rmsnorm(ckv, kv_norm_g) @ w_ukv).reshape(B, T, MLA_HEADS, MLA_NOPE + MLA_DV)
        q_nope, q_pe = q[..., :MLA_NOPE], q[..., MLA_NOPE:]
        k_nope, v = kv[..., :MLA_NOPE], kv[..., MLA_NOPE:]
        k_pe = kr[:, :, None, :]
        if with_pos:
            q_pe = _apply_rope(q_pe, cos, sin)
            k_pe = _apply_rope(k_pe, cos, sin)
        q = jnp.concatenate([q_nope, q_pe], axis=-1)
        k = jnp.concatenate([k_nope, jnp.broadcast_to(k_pe, (B, T, MLA_HEADS, MLA_ROPE))], axis=-1)
        return q, k, v

    qc, kc, vc = qkv(parts_c, False)
    ql, kl, vl = qkv(parts_l, True)
    k_all = jnp.concatenate([kc, kl], axis=1)
    v_all = jnp.concatenate([vc, vl], axis=1)
    B, T = ql.shape[:2]
    y_l = _block_attention(ql, k_all, v_all).reshape(B, T, MLA_HEADS * MLA_DV)
    y_c = _block_attention(qc, kc, vc).reshape(B, qc.shape[1], MLA_HEADS * MLA_DV) if update_ctx else None
    return y_l, y_c


def _dwconv_centred(x, w, b):
    C = x.shape[-1]
    y = lax.conv_general_dilated(x, w[:, None, :].astype(x.dtype), window_strides=(1,),
                                 padding=[(CONV_W // 2, CONV_W - 1 - CONV_W // 2)],
                                 dimension_numbers=('NWC', 'WIO', 'NWC'), feature_group_count=C)
    return y + b


def _lin_combine(e1, e2):
    a1, b1 = e1
    a2, b2 = e2
    return a1 * a2, a2 * b1 + b2


def _rglru_scan(xc, wa, ba, wx, bx, lam, h0):
    B, T, W = xc.shape
    f32 = jnp.float32
    xf = xc.astype(f32)
    xb = xf.reshape(B, T, LRU_BLOCKS, LRU_BW)
    r = jax.nn.sigmoid(jnp.einsum('btnk,nkj->btnj', xb, wa.astype(f32)).reshape(B, T, W) + ba.astype(f32))
    i = jax.nn.sigmoid(jnp.einsum('btnk,nkj->btnj', xb, wx.astype(f32)).reshape(B, T, W) + bx.astype(f32))
    log_a = -LRU_C * r * jax.nn.softplus(-lam.astype(f32))
    a = jnp.exp(log_a)
    u = jnp.sqrt(-jnp.expm1(2.0 * log_a)) * i * xf
    u = u.at[:, 0].add(a[:, 0] * h0)
    _, h = lax.associative_scan(_lin_combine, (a, u), axis=1)
    return h, h[:, -1]


def _rglru_mixer(parts_l, parts_c, conv_w, conv_b, wa, ba, wx, bx, lam, update_ctx):
    def both_dirs(zx, h0_f, h0_b):
        xc = _dwconv_centred(zx, conv_w, conv_b)
        h_f, last_f = _rglru_scan(xc, wa[0], ba[0], wx[0], bx[0], lam[0], h0_f)
        h_b, last_b = _rglru_scan(jnp.flip(xc, axis=1), wa[1], ba[1], wx[1], bx[1], lam[1], h0_b)
        return h_f + jnp.flip(h_b, axis=1), last_f, last_b

    gc, zxc = parts_c
    zero = jnp.zeros((gc.shape[0], LRU_WIDTH), jnp.float32)
    hc, st_f, st_b = both_dirs(zxc, zero, zero)
    gl, zxl = parts_l
    hl, _, _ = both_dirs(zxl, st_f, st_b)
    y_c = jax.nn.gelu(gc) * hc.astype(gc.dtype) if update_ctx else None
    return jax.nn.gelu(gl) * hl.astype(gl.dtype), y_c


def _na_mixer(parts_l, parts_c, rpb, rows, update_ctx):
    zq, zk, zv = parts_l
    B, T, _ = zq.shape
    H, dh = NA_HEADS, NA_DH
    q = zq.reshape(B, rows, GRID_W, H, dh)
    k = zk.reshape(B, rows, GRID_W, H, dh)
    v = zv.reshape(B, rows, GRID_W, H, dh)
    qc, kc, vc = [t.reshape(B, t.shape[1], H, dh) for t in parts_c]
    wr = min(NA_WIN_R, rows)
    col = jnp.arange(GRID_W)
    c_start = jnp.clip(col - NA_WIN_C // 2, 0, GRID_W - NA_WIN_C)
    in_win = (col[None, :] >= c_start[:, None]) & (col[None, :] < c_start[:, None] + NA_WIN_C)
    dc = jnp.clip(col[None, :] - col[:, None] + NA_WIN_C - 1, 0, 2 * NA_WIN_C - 2)
    scale = dh ** -0.5
    rpb = rpb.astype(jnp.float32)

    def one_row(r):
        r_start = jnp.clip(r - wr // 2, 0, rows - wr)
        kr = lax.dynamic_slice_in_dim(k, r_start, wr, axis=1)
        vr = lax.dynamic_slice_in_dim(v, r_start, wr, axis=1)
        qr = lax.dynamic_index_in_dim(q, r, axis=1, keepdims=False)
        dr = r_start + jnp.arange(wr) - r + NA_WIN_R - 1
        bias = jnp.transpose(rpb[:, dr[:, None, None], dc[None, :, :]], (0, 2, 1, 3))
        s_loc = jnp.einsum('bqhd,bwkhd->bhqwk', qr, kr).astype(jnp.float32) * scale + bias[None]
        s_loc = jnp.where(in_win[:, None, :], s_loc, NEG_INF).reshape(B, H, GRID_W, wr * GRID_W)
        s_ctx = jnp.einsum('bqhd,bkhd->bhqk', qr, kc).astype(jnp.float32) * scale
        p = jax.nn.softmax(jnp.concatenate([s_loc, s_ctx], axis=-1), axis=-1).astype(v.dtype)
        n_loc = wr * GRID_W
        return (jnp.einsum('bhqk,bkhd->bqhd', p[..., :n_loc], vr.reshape(B, n_loc, H, dh))
                + jnp.einsum('bhqk,bkhd->bqhd', p[..., n_loc:], vc))

    o = lax.map(one_row, jnp.arange(rows))
    y_l = jnp.moveaxis(o, 0, 1).reshape(B, T, H * dh)
    y_c = _block_attention(qc, kc, vc).reshape(B, qc.shape[1], H * dh) if update_ctx else None
    return y_l, y_c


def _merge(h, ys, w_in_l, w_branch_l, w_out_l):
    m = 0
    for i, y in enumerate(ys):
        lo = MIX_COLS + i * D_MODEL
        gate = jax.nn.sigmoid(h @ w_in_l[:, lo:lo + D_MODEL])
        m = m + gate * (y @ w_branch_l[i])
    return m @ w_out_l


def _peer(h, wq, keys, u, v):
    B, T, D = h.shape
    q = (h @ wq).astype(jnp.float32).reshape(B, T, PEER_HEADS, PEER_DQ)
    half = PEER_DQ // 2
    s1 = jnp.einsum('bthd,hkd->bthk', q[..., :half], keys[:, 0].astype(jnp.float32))
    s2 = jnp.einsum('bthd,hkd->bthk', q[..., half:], keys[:, 1].astype(jnp.float32))
    v1, i1 = lax.top_k(s1, PEER_TOPK)
    v2, i2 = lax.top_k(s2, PEER_TOPK)
    cand = (v1[..., :, None] + v2[..., None, :]).reshape(B, T, PEER_HEADS, PEER_TOPK * PEER_TOPK)
    cidx = (i1[..., :, None] * PEER_NKEYS + i2[..., None, :]).reshape(B, T, PEER_HEADS, PEER_TOPK * PEER_TOPK)
    top_s, pos = lax.top_k(cand, PEER_TOPK)
    eidx = jnp.take_along_axis(cidx, pos, axis=-1)
    g = jax.nn.softmax(top_s, axis=-1).astype(h.dtype)
    E = PEER_HEADS * PEER_TOPK
    nblk = (B * T) // PEER_BLOCK
    xs = (h.reshape(nblk, PEER_BLOCK, D), eidx.reshape(nblk, PEER_BLOCK, E), g.reshape(nblk, PEER_BLOCK, E))

    def one(args):
        xb, ib, gb = args
        act = jax.nn.gelu(jnp.einsum('td,ted->te', xb, u[ib]))
        return jnp.einsum('te,ted->td', gb * act, v[ib])

    return lax.map(one, xs).reshape(B, T, D)


def setup_inputs(seed: int = 0) -> dict:
    key = jax.random.key(seed)
    ks = jax.random.split(key, 32)
    L, D = DEPTH, D_MODEL

    def nrm(i, shape, scale):
        return jax.random.normal(ks[i], shape, jnp.float32) * scale

    a0 = jax.random.uniform(ks[31], (L, 2, LRU_WIDTH), jnp.float32, 0.9, 0.999) ** (1.0 / LRU_C)
    return {
        'x': nrm(0, (BATCH, SEQ, D), 1.0),
        'c': nrm(1, (BATCH, D), 1.0),
        'ctx': nrm(2, (BATCH, CTX_LEN, D), 1.0),
        'c_ctx': nrm(3, (D,), 1.0),
        'w_ada': nrm(4, (L, D, 6 * D), 0.5 * D ** -0.5),
        'b_ada': nrm(5, (L, 6 * D), 0.02),
        'norm1_g': 1.0 + nrm(6, (L, D), 0.02),
        'norm2_g': 1.0 + nrm(7, (L, D), 0.02),
        'w_in': nrm(8, (L, D, IN_WIDTH), D ** -0.5),
        'gla_wa2': nrm(9, (L, 2, GLA_RANK, GLA_HEADS * GLA_DK), GLA_RANK ** -0.5),
        'gla_ba': nrm(10, (L, 2, GLA_HEADS * GLA_DK), 0.02),
        'gla_norm_g': 1.0 + nrm(11, (L, GLA_DV), 0.02),
        'mla_q_norm_g': 1.0 + nrm(12, (L, MLA_Q_RANK), 0.02),
        'mla_w_uq': nrm(13, (L, MLA_Q_RANK, MLA_HEADS * (MLA_NOPE + MLA_ROPE)), MLA_Q_RANK ** -0.5),
        'mla_kv_norm_g': 1.0 + nrm(14, (L, MLA_KV_RANK), 0.02),
        'mla_w_ukv': nrm(15, (L, MLA_KV_RANK, MLA_HEADS * (MLA_NOPE + MLA_DV)), MLA_KV_RANK ** -0.5),
        'lru_conv_w': nrm(16, (L, CONV_W, LRU_WIDTH), CONV_W ** -0.5),
        'lru_conv_b': nrm(17, (L, LRU_WIDTH), 0.02),
        'lru_wa': nrm(18, (L, 2, LRU_BLOCKS, LRU_BW, LRU_BW), LRU_BW ** -0.5),
        'lru_ba': nrm(19, (L, 2, LRU_WIDTH), 0.02),
        'lru_wx': nrm(20, (L, 2, LRU_BLOCKS, LRU_BW, LRU_BW), LRU_BW ** -0.5),
        'lru_bx': nrm(21, (L, 2, LRU_WIDTH), 0.02),
        'lru_lambda': jnp.log(a0) - jnp.log1p(-a0),
        'na_rpb': nrm(22, (L, NA_HEADS, 2 * NA_WIN_R - 1, 2 * NA_WIN_C - 1), 0.1),
        'w_branch': nrm(23, (L, N_BRANCH, BRANCH_W, D), BRANCH_W ** -0.5),
        'w_out': nrm(24, (L, D, D), D ** -0.5),
        'peer_wq': nrm(25, (L, D, PEER_HEADS * PEER_DQ), D ** -0.5),
        'peer_keys': nrm(26, (L, PEER_HEADS, 2, PEER_NKEYS, PEER_DQ // 2), (PEER_DQ // 2) ** -0.5),
        'peer_u': nrm(27, (L, PEER_N, D), D ** -0.5),
        'peer_v': nrm(28, (L, PEER_N, D), 1.0),
        'final_norm_g': 1.0 + nrm(29, (D,), 0.02),
    }


def reference(x, c, ctx, c_ctx, w_ada, b_ada, norm1_g, norm2_g, w_in, gla_wa2, gla_ba, gla_norm_g,
              mla_q_norm_g, mla_w_uq, mla_kv_norm_g, mla_w_ukv, lru_conv_w, lru_conv_b, lru_wa, lru_ba,
              lru_wx, lru_bx, lru_lambda, na_rpb, w_branch, w_out, peer_wq, peer_keys, peer_u, peer_v,
              final_norm_g):
    seq = x.shape[1]
    rows = seq // GRID_W
    rope_cs = _axial_rope(seq, MLA_ROPE)
    xc = ctx
    for l in range(DEPTH):
        update_ctx = l < DEPTH - 1
        mod_l = jnp.split((jax.nn.silu(c) @ w_ada[l] + b_ada[l])[:, None, :], 6, axis=-1)
        mod_c = jnp.split((jax.nn.silu(c_ctx) @ w_ada[l] + b_ada[l])[None, None, :], 6, axis=-1)
        h = _modulate(_rmsnorm(x, norm1_g[l]), mod_l[0], mod_l[1])
        hc = _modulate(_rmsnorm(xc, norm1_g[l]), mod_c[0], mod_c[1])
        zl = _split_cols(h @ w_in[l, :, :MIX_COLS])
        zc = _split_cols(hc @ w_in[l, :, :MIX_COLS])
        ys = (
            _gla_mixer(zl[0:6], zc[0:6], gla_wa2[l], gla_ba[l], gla_norm_g[l], update_ctx),
            _mla_mixer(zl[6:9], zc[6:9], mla_q_norm_g[l], mla_w_uq[l], mla_kv_norm_g[l], mla_w_ukv[l],
                       rope_cs, update_ctx),
            _rglru_mixer(zl[9:11], zc[9:11], lru_conv_w[l], lru_conv_b[l], lru_wa[l], lru_ba[l],
                         lru_wx[l], lru_bx[l], lru_lambda[l], update_ctx),
            _na_mixer(zl[11:14], zc[11:14], na_rpb[l], rows, update_ctx),
        )
        x = x + mod_l[2] * _merge(h, [y[0] for y in ys], w_in[l], w_branch[l], w_out[l])
        h2 = _modulate(_rmsnorm(x, norm2_g[l]), mod_l[3], mod_l[4])
        x = x + mod_l[5] * _peer(h2, peer_wq[l], peer_keys[l], peer_u[l], peer_v[l])
        if update_ctx:
            xc = xc + mod_c[2] * _merge(hc, [y[1] for y in ys], w_in[l], w_branch[l], w_out[l])
            h2c = _modulate(_rmsnorm(xc, norm2_g[l]), mod_c[3], mod_c[4])
            xc = xc + mod_c[5] * _peer(h2c, peer_wq[l], peer_keys[l], peer_u[l], peer_v[l])
    return _rmsnorm(x, final_norm_g)
```

```python
import functools

import jax
import jax.numpy as jnp
import numpy as np
from jax import lax
from jax.experimental import pallas as pl
from jax.experimental.pallas import tpu as pltpu

GRID_W = 64
EPS = 1e-6
NEG_INF = -1e30
N_BRANCH = 4
BRANCH_W = 1024
ATT_QBLOCK = 128
ROPE_THETA = 10000.0

GLA_HEADS = 4
GLA_DK = 128
GLA_DV = BRANCH_W // GLA_HEADS
GLA_RANK = 16
GLA_TAU = 16.0
GLA_CHUNK = 64

MLA_HEADS = 8
MLA_Q_RANK = 768
MLA_KV_RANK = 256
MLA_NOPE = 128
MLA_ROPE = 64
MLA_DV = BRANCH_W // MLA_HEADS

LRU_WIDTH = BRANCH_W
LRU_BLOCKS = 8
LRU_BW = LRU_WIDTH // LRU_BLOCKS
CONV_W = 4
LRU_C = 8.0

NA_HEADS = 8
NA_DH = BRANCH_W // NA_HEADS
NA_WIN_R = 8
NA_WIN_C = 16

PEER_HEADS = 8
PEER_DQ = 256
PEER_TOPK = 16

MIX_SIZES = (
    GLA_HEADS * GLA_DK, GLA_HEADS * GLA_DK, GLA_HEADS * GLA_DV, GLA_HEADS * GLA_DV, GLA_RANK, GLA_RANK,
    MLA_Q_RANK, MLA_KV_RANK, MLA_ROPE, LRU_WIDTH, LRU_WIDTH,
    NA_HEADS * NA_DH, NA_HEADS * NA_DH, NA_HEADS * NA_DH,
)
MIX_COLS = sum(MIX_SIZES)

V7X_VMEM_BYTES = 64 * 1024 * 1024
VMEM_LIMIT = V7X_VMEM_BYTES - 8 * 1024 * 1024
LANE = 128
SUBLANE = 8

F32 = jnp.float32
BF16 = jnp.bfloat16


def _tile(n, pref, mult=SUBLANE):
    if n <= pref:
        return n
    t = (pref // mult) * mult
    while t > mult and n % t:
        t -= mult
    assert n % t == 0, (n, pref, mult)
    return t


def _params(*sem):
    return pltpu.CompilerParams(dimension_semantics=sem, vmem_limit_bytes=VMEM_LIMIT)


def _ada_kernel(c_ref, w_ref, b_ref, o_ref):
    cv = c_ref[...]
    a = cv * jax.nn.sigmoid(cv)
    o_ref[0] = jnp.dot(a, w_ref[0], preferred_element_type=F32, precision=lax.Precision.HIGHEST) + b_ref[0]


def _ada(cc, w_ada, b_ada):
    L, D, W = w_ada.shape
    R = cc.shape[0]
    tn = _tile(W, 512, LANE)
    return pl.pallas_call(
        _ada_kernel,
        grid=(L, W // tn),
        in_specs=[
            pl.BlockSpec((R, D), lambda l, j: (0, 0)),
            pl.BlockSpec((1, D, tn), lambda l, j: (l, 0, j)),
            pl.BlockSpec((1, 1, tn), lambda l, j: (l, 0, j)),
        ],
        out_specs=pl.BlockSpec((1, R, tn), lambda l, j: (l, 0, j)),
        out_shape=jax.ShapeDtypeStruct((L, R, W), F32),
        compiler_params=_params("arbitrary", "arbitrary"),
        name="ada_mod",
    )(cc, w_ada, b_ada.reshape(L, 1, W))


def _norm_kernel(*refs, has_delta, modulate, emit_x):
    it = iter(refs)
    x_ref = next(it)
    if has_delta:
        d_ref, gate_ref = next(it), next(it)
    g_ref = next(it)
    if modulate:
        shift_ref, scale_ref = next(it), next(it)
    if emit_x:
        xo_ref = next(it)
    h_ref = next(it)
    x = x_ref[0]
    if has_delta:
        x = x + gate_ref[0] * d_ref[0]
    if emit_x:
        xo_ref[0] = x
    y = x * lax.rsqrt(jnp.mean(x * x, axis=-1, keepdims=True) + EPS)
    y = y * g_ref[...]
    if modulate:
        y = y * (1.0 + scale_ref[0]) + shift_ref[0]
    h_ref[0] = y.astype(h_ref.dtype)


def _norm(x, g, *, delta=None, gate=None, shift=None, scale=None, out_dtype=None, emit_x=True):
    out_dtype = BF16 if out_dtype is None else out_dtype
    B, T, D = x.shape
    tt = _tile(T, 256)
    has_delta = delta is not None
    emit_x = emit_x and has_delta
    modulate = shift is not None
    tok = pl.BlockSpec((1, tt, D), lambda b, t: (b, t, 0))
    vec = pl.BlockSpec((1, 1, D), lambda b, t: (b, 0, 0))
    args, specs = [x], [tok]
    if has_delta:
        args += [delta, gate.reshape(B, 1, D)]
        specs += [tok, vec]
    args.append(g.reshape(1, D))
    specs.append(pl.BlockSpec((1, D), lambda b, t: (0, 0)))
    if modulate:
        args += [shift.reshape(B, 1, D), scale.reshape(B, 1, D)]
        specs += [vec, vec]
    out_shape, out_specs = [], []
    if emit_x:
        out_shape.append(jax.ShapeDtypeStruct((B, T, D), F32))
        out_specs.append(tok)
    out_shape.append(jax.ShapeDtypeStruct((B, T, D), out_dtype))
    out_specs.append(tok)
    outs = pl.pallas_call(
        functools.partial(_norm_kernel, has_delta=has_delta, modulate=modulate, emit_x=emit_x),
        grid=(B, T // tt),
        in_specs=specs,
        out_specs=out_specs,
        out_shape=out_shape,
        compiler_params=_params("arbitrary", "arbitrary"),
        name="res_norm_mod",
    )(*args)
    if emit_x:
        return outs[0], outs[1]
    return None, outs[0]


def _mm_kernel(*refs, act, has_res):
    if has_res:
        a_ref, b_ref, r_ref, m_ref, o_ref = refs
    else:
        a_ref, b_ref, o_ref = refs
    acc = jnp.dot(a_ref[...], b_ref[...], preferred_element_type=F32)
    if act == "sigmoid":
        acc = jax.nn.sigmoid(acc)
    if has_res:
        acc = r_ref[...] + m_ref[0] * acc
    o_ref[...] = acc.astype(o_ref.dtype)


def _matmul(a, b, *, out_dtype, act=None, res=None, mod=None, rows_per_batch=None, tm=1024, tn=512):
    M, K = a.shape
    N = b.shape[1]
    tm = _tile(rows_per_batch if rows_per_batch else M, tm)
    tn = _tile(N, tn, LANE)
    has_res = res is not None
    args = [a, b]
    specs = [pl.BlockSpec((tm, K), lambda i, j: (i, 0)), pl.BlockSpec((K, tn), lambda i, j: (0, j))]
    if has_res:
        args += [res, mod.reshape(mod.shape[0], 1, N)]
        specs += [
            pl.BlockSpec((tm, tn), lambda i, j: (i, j)),
            pl.BlockSpec((1, 1, tn), lambda i, j: ((i * tm) // rows_per_batch, 0, j)),
        ]
    return pl.pallas_call(
        functools.partial(_mm_kernel, act=act, has_res=has_res),
        grid=(M // tm, N // tn),
        in_specs=specs,
        out_specs=pl.BlockSpec((tm, tn), lambda i, j: (i, j)),
        out_shape=jax.ShapeDtypeStruct((M, N), out_dtype),
        compiler_params=_params("arbitrary", "arbitrary"),
        name="matmul",
    )(*args)


def _merge_kernel(y0, y1, y2, y3, g0, g1, g2, g3, w_ref, o_ref):
    acc = None
    for i, (y, g) in enumerate(((y0, g0), (y1, g1), (y2, g2), (y3, g3))):
        p = g[...].astype(F32) * jnp.dot(y[...], w_ref[i], preferred_element_type=F32)
        acc = p if acc is None else acc + p
    o_ref[...] = acc.astype(o_ref.dtype)


def _merge(ys, gates, w_branch):
    M = ys[0].shape[0]
    D = w_branch.shape[2]
    tm = _tile(M, 1024)
    tn = _tile(D, 512, LANE)
    nj = D // tn
    y_spec = pl.BlockSpec((tm, BRANCH_W), lambda i, j: (i, 0))
    g_specs = [pl.BlockSpec((tm, tn), functools.partial(lambda i, j, br: (i, br * nj + j), br=br))
               for br in range(N_BRANCH)]
    return pl.pallas_call(
        _merge_kernel,
        grid=(M // tm, nj),
        in_specs=[y_spec] * N_BRANCH + g_specs + [pl.BlockSpec((N_BRANCH, BRANCH_W, tn), lambda i, j: (0, 0, j))],
        out_specs=pl.BlockSpec((tm, tn), lambda i, j: (i, j)),
        out_shape=jax.ShapeDtypeStruct((M, D), BF16),
        compiler_params=_params("arbitrary", "arbitrary"),
        name="merge",
    )(*ys, gates, gates, gates, gates, w_branch)


def _peer_w_kernel(g_ref, e1_ref, e2_ref, o_ref, *, nk):
    tm = g_ref.shape[0]
    key_iota = lax.broadcasted_iota(jnp.int32, (nk, g_ref.shape[1]), 0)

    def body(t, carry):
        row = pl.ds(t, 1)
        a = jnp.where(e1_ref[row, :] == key_iota, g_ref[row, :], 0.0).astype(BF16)
        b = jnp.where(e2_ref[row, :] == key_iota, 1.0, 0.0).astype(BF16)
        w = lax.dot_general(a, b, (((1,), (1,)), ((), ())), preferred_element_type=F32)
        o_ref[t] = w.astype(o_ref.dtype)
        return carry

    lax.fori_loop(0, tm, body, 0)


def _peer_w(g, e1, e2, nk):
    M, S = g.shape
    tm = _tile(M, 256)
    spec = pl.BlockSpec((tm, S), lambda i: (i, 0))
    return pl.pallas_call(
        functools.partial(_peer_w_kernel, nk=nk),
        grid=(M // tm,),
        in_specs=[spec, spec, spec],
        out_specs=pl.BlockSpec((tm, nk, nk), lambda i: (i, 0, 0)),
        out_shape=jax.ShapeDtypeStruct((M, nk, nk), BF16),
        compiler_params=_params("arbitrary"),
        name="peer_route_weights",
    )(g, e1, e2)


def _gelu_tanh(x):
    return 0.5 * x * (1.0 + jnp.tanh(0.7978845608028654 * (x + 0.044715 * (x * x * x))))


def _peer_dense_kernel(h_ref, u_ref, v_ref, w_ref, o_ref):
    @pl.when(pl.program_id(1) == 0)
    def _():
        o_ref[...] = jnp.zeros_like(o_ref)

    s = lax.dot_general(h_ref[...], u_ref[...], (((1,), (1,)), ((), ())), preferred_element_type=F32)
    a = (_gelu_tanh(s) * w_ref[...].astype(F32)).astype(BF16)
    o_ref[...] += jnp.dot(a, v_ref[...], preferred_element_type=F32)


def _peer_dense(h2, u, v, w):
    M, D = h2.shape
    E = u.shape[0]
    tm = _tile(M, 512)
    te = _tile(E, 512, LANE)
    return pl.pallas_call(
        _peer_dense_kernel,
        grid=(M // tm, E // te),
        in_specs=[
            pl.BlockSpec((tm, D), lambda i, e: (i, 0)),
            pl.BlockSpec((te, D), lambda i, e: (e, 0)),
            pl.BlockSpec((te, D), lambda i, e: (e, 0)),
            pl.BlockSpec((tm, te), lambda i, e: (i, e)),
        ],
        out_specs=pl.BlockSpec((tm, D), lambda i, e: (i, 0)),
        out_shape=jax.ShapeDtypeStruct((M, D), F32),
        compiler_params=_params("arbitrary", "arbitrary"),
        name="peer_dense",
    )(h2, u, v, w)


def _peer_route_xla(q, keys):
    M = q.shape[0]
    nk = keys.shape[2]
    half = PEER_DQ // 2
    qh = q.reshape(M, PEER_HEADS, PEER_DQ)
    s1 = jnp.einsum("thd,hkd->thk", qh[..., :half], keys[:, 0], precision=lax.Precision.HIGHEST)
    s2 = jnp.einsum("thd,hkd->thk", qh[..., half:], keys[:, 1], precision=lax.Precision.HIGHEST)
    v1, i1 = lax.top_k(s1, PEER_TOPK)
    v2, i2 = lax.top_k(s2, PEER_TOPK)
    cand = (v1[..., :, None] + v2[..., None, :]).reshape(M, PEER_HEADS, PEER_TOPK * PEER_TOPK)
    top_s, pos = lax.top_k(cand, PEER_TOPK)
    e1 = jnp.take_along_axis(i1, pos // PEER_TOPK, axis=-1)
    e2 = jnp.take_along_axis(i2, pos % PEER_TOPK, axis=-1)
    g = jax.nn.softmax(top_s, axis=-1)
    S = PEER_HEADS * PEER_TOPK
    return g.reshape(M, S), e1.reshape(M, S).astype(jnp.int32), e2.reshape(M, S).astype(jnp.int32)


def _peer(h2, wq, keys, u, v):
    nk = keys.shape[2]
    q = _matmul(h2, wq, out_dtype=F32)
    g, e1, e2 = _peer_route_xla(q, keys)
    w = _peer_w(g, e1, e2, nk).reshape(h2.shape[0], nk * nk)
    return _peer_dense(h2, u, v, w)


def _rmsnorm(x, g):
    xf = x.astype(F32)
    y = xf * lax.rsqrt(jnp.mean(xf * xf, axis=-1, keepdims=True) + EPS)
    return (y * g.astype(F32)).astype(x.dtype)


def _split_cols(z):
    idx = np.cumsum(MIX_SIZES)[:-1].tolist()
    return jnp.split(z, idx, axis=-1)


def _axial_rope(n_tok, dim):
    t = jnp.arange(n_tok, dtype=jnp.int32)
    row = (t // GRID_W).astype(F32)
    col = (t % GRID_W).astype(F32)
    n_freq = dim // 4
    inv = ROPE_THETA ** (-jnp.arange(n_freq, dtype=F32) / n_freq)
    ang = jnp.concatenate([row[:, None] * inv, col[:, None] * inv], axis=-1)
    return jnp.cos(ang), jnp.sin(ang)


def _apply_rope(x, cos, sin):
    xf = x.astype(F32)
    half = xf.shape[-1] // 2
    x1, x2 = xf[..., :half], xf[..., half:]
    cs, sn = cos[:, None, :], sin[:, None, :]
    return jnp.concatenate([x1 * cs - x2 * sn, x1 * sn + x2 * cs], axis=-1).astype(x.dtype)


def _block_attention(q, k, v):
    B, T, H, dq = q.shape
    scale = dq ** -0.5
    nb = T // ATT_QBLOCK
    qb = jnp.moveaxis(q.reshape(B, nb, ATT_QBLOCK, H, dq), 1, 0)

    def one(qblk):
        s = jnp.einsum("bqhd,bkhd->bhqk", qblk, k).astype(F32) * scale
        p = jax.nn.softmax(s, axis=-1).astype(v.dtype)
        return jnp.einsum("bhqk,bkhd->bqhd", p, v)

    o = lax.map(one, qb)
    return jnp.moveaxis(o, 0, 1).reshape(B, T, H, v.shape[-1])


def _gla_scan(q, k, v, log_a, s0):
    B, T, H, dk = q.shape
    dv = v.shape[-1]
    C = GLA_CHUNK
    n = T // C
    q = q.astype(F32).reshape(B, n, C, H, dk) * dk ** -0.5
    k = k.astype(F32).reshape(B, n, C, H, dk)
    v = v.astype(F32).reshape(B, n, C, H, dv)
    b = jnp.cumsum(log_a.astype(F32).reshape(B, n, C, H, dk), axis=2)
    b_end = b[:, :, -1:]
    q_dec = q * jnp.exp(b)
    k_inv = k * jnp.exp(-b)
    k_end = k * jnp.exp(b_end - b)
    lower = jnp.tril(jnp.ones((C, C), dtype=bool))
    att = jnp.where(lower, jnp.einsum("bnthk,bnshk->bnhts", q_dec, k_inv), 0.0)
    o_intra = jnp.einsum("bnhts,bnshv->bnthv", att, v)
    kv_chunk = jnp.einsum("bnshk,bnshv->nbhkv", k_end, v)
    decay_chunk = jnp.moveaxis(jnp.exp(b_end[:, :, 0]), 1, 0)

    def step(S, inp):
        d, kv = inp
        return d[..., None] * S + kv, S

    s_last, s_start = lax.scan(step, s0.astype(F32), (decay_chunk, kv_chunk))
    o_inter = jnp.einsum("bnthk,nbhkv->bnthv", q_dec, s_start)
    return (o_intra + o_inter).reshape(B, T, H, dv), s_last


def _gla_bidir(q, k, v, la_f, la_b, s0_f, s0_b):
    o_f, s_f = _gla_scan(q, k, v, la_f, s0_f)
    fl = lambda t: jnp.flip(t, axis=1)
    o_b, s_b = _gla_scan(fl(q), fl(k), fl(v), fl(la_b), s0_b)
    return o_f + fl(o_b), s_f, s_b


def _gla_mixer(parts_l, parts_c, wa2, ba, norm_g, update_ctx):
    def prep(parts):
        zq, zk, zv, zg, zaf, zab = parts
        B, T, _ = zq.shape
        la = [jax.nn.log_sigmoid((za @ wa2[i] + ba[i]).astype(F32)).reshape(B, T, GLA_HEADS, GLA_DK) / GLA_TAU
              for i, za in enumerate((zaf, zab))]
        return (zq.reshape(B, T, GLA_HEADS, GLA_DK), zk.reshape(B, T, GLA_HEADS, GLA_DK),
                zv.reshape(B, T, GLA_HEADS, GLA_DV), zg, la[0], la[1])

    def out(o, zg):
        B, T = zg.shape[:2]
        o = _rmsnorm(o, norm_g).reshape(B, T, GLA_HEADS * GLA_DV)
        return o.astype(zg.dtype) * jax.nn.silu(zg)

    qc, kc, vc, gc, lfc, lbc = prep(parts_c)
    zero = jnp.zeros((qc.shape[0], GLA_HEADS, GLA_DK, GLA_DV), F32)
    oc, s_f, s_b = _gla_bidir(qc, kc, vc, lfc, lbc, zero, zero)
    ql, kl, vl, gl, lfl, lbl = prep(parts_l)
    ol, _, _ = _gla_bidir(ql, kl, vl, lfl, lbl, s_f, s_b)
    y_c = out(oc, gc) if update_ctx else None
    return out(ol, gl), y_c


def _mla_mixer(parts_l, parts_c, q_norm_g, w_uq, kv_norm_g, w_ukv, rope_cs, update_ctx):
    cos, sin = rope_cs

    def qkv(parts, with_pos):
        cq, ckv, kr = parts
        B, T, _ = cq.shape
        q = (_rmsnorm(cq, q_norm_g) @ w_uq).reshape(B, T, MLA_HEADS, MLA_NOPE + MLA_ROPE)
        kv = (_rmsnorm(ckv, kv_norm_g) @ w_ukv).reshape(B, T, MLA_HEADS, MLA_NOPE + MLA_DV)
        q_nope, q_pe = q[..., :MLA_NOPE], q[..., MLA_NOPE:]
        k_nope, v = kv[..., :MLA_NOPE], kv[..., MLA_NOPE:]
        k_pe = kr[:, :, None, :]
        if with_pos:
            q_pe = _apply_rope(q_pe, cos, sin)
            k_pe = _apply_rope(k_pe, cos, sin)
        q = jnp.concatenate([q_nope, q_pe], axis=-1)
        k = jnp.concatenate([k_nope, jnp.broadcast_to(k_pe, (B, T, MLA_HEADS, MLA_ROPE))], axis=-1)
        return q, k, v

    qc, kc, vc = qkv(parts_c, False)
    ql, kl, vl = qkv(parts_l, True)
    k_all = jnp.concatenate([kc, kl], axis=1)
    v_all = jnp.concatenate([vc, vl], axis=1)
    B, T = ql.shape[:2]
    y_l = _block_attention(ql, k_all, v_all).reshape(B, T, MLA_HEADS * MLA_DV)
    y_c = _block_attention(qc, kc, vc).reshape(B, qc.shape[1], MLA_HEADS * MLA_DV) if update_ctx else None
    return y_l, y_c


def _dwconv_centred(x, w, b):
    C = x.shape[-1]
    y = lax.conv_general_dilated(x, w[:, None, :].astype(x.dtype), window_strides=(1,),
                                 padding=[(CONV_W // 2, CONV_W - 1 - CONV_W // 2)],
                                 dimension_numbers=("NWC", "WIO", "NWC"), feature_group_count=C)
    return y + b


def _lin_combine(e1, e2):
    a1, b1 = e1
    a2, b2 = e2
    return a1 * a2, a2 * b1 + b2


def _rglru_scan(xc, wa, ba, wx, bx, lam, h0):
    B, T, W = xc.shape
    xf = xc.astype(F32)
    xb = xf.reshape(B, T, LRU_BLOCKS, LRU_BW)
    r = jax.nn.sigmoid(jnp.einsum("btnk,nkj->btnj", xb, wa.astype(F32)).reshape(B, T, W) + ba.astype(F32))
    i = jax.nn.sigmoid(jnp.einsum("btnk,nkj->btnj", xb, wx.astype(F32)).reshape(B, T, W) + bx.astype(F32))
    log_a = -LRU_C * r * jax.nn.softplus(-lam.astype(F32))
    a = jnp.exp(log_a)
    u = jnp.sqrt(-jnp.expm1(2.0 * log_a)) * i * xf
    u = u.at[:, 0].add(a[:, 0] * h0)
    _, h = lax.associative_scan(_lin_combine, (a, u), axis=1)
    return h, h[:, -1]


def _rglru_mixer(parts_l, parts_c, conv_w, conv_b, wa, ba, wx, bx, lam, update_ctx):
    def both_dirs(zx, h0_f, h0_b):
        xc = _dwconv_centred(zx, conv_w, conv_b)
        h_f, last_f = _rglru_scan(xc, wa[0], ba[0], wx[0], bx[0], lam[0], h0_f)
        h_b, last_b = _rglru_scan(jnp.flip(xc, axis=1), wa[1], ba[1], wx[1], bx[1], lam[1], h0_b)
        return h_f + jnp.flip(h_b, axis=1), last_f, last_b

    gc, zxc = parts_c
    zero = jnp.zeros((gc.shape[0], LRU_WIDTH), F32)
    hc, st_f, st_b = both_dirs(zxc, zero, zero)
    gl, zxl = parts_l
    hl, _, _ = both_dirs(zxl, st_f, st_b)
    y_c = jax.nn.gelu(gc) * hc.astype(gc.dtype) if update_ctx else None
    return jax.nn.gelu(gl) * hl.astype(gl.dtype), y_c


def _na_mixer(parts_l, parts_c, rpb, rows, update_ctx):
    zq, zk, zv = parts_l
    B, T, _ = zq.shape
    H, dh = NA_HEADS, NA_DH
    q = zq.reshape(B, rows, GRID_W, H, dh)
    k = zk.reshape(B, rows, GRID_W, H, dh)
    v = zv.reshape(B, rows, GRID_W, H, dh)
    qc, kc, vc = [t.reshape(B, t.shape[1], H, dh) for t in parts_c]
    wr = min(NA_WIN_R, rows)
    col = jnp.arange(GRID_W)
    c_start = jnp.clip(col - NA_WIN_C // 2, 0, GRID_W - NA_WIN_C)
    in_win = (col[None, :] >= c_start[:, None]) & (col[None, :] < c_start[:, None] + NA_WIN_C)
    dc = jnp.clip(col[None, :] - col[:, None] + NA_WIN_C - 1, 0, 2 * NA_WIN_C - 2)
    scale = dh ** -0.5
    rpb = rpb.astype(F32)

    def one_row(r):
        r_start = jnp.clip(r - wr // 2, 0, rows - wr)
        kr = lax.dynamic_slice_in_dim(k, r_start, wr, axis=1)
        vr = lax.dynamic_slice_in_dim(v, r_start, wr, axis=1)
        qr = lax.dynamic_index_in_dim(q, r, axis=1, keepdims=False)
        dr = r_start + jnp.arange(wr) - r + NA_WIN_R - 1
        bias = jnp.transpose(rpb[:, dr[:, None, None], dc[None, :, :]], (0, 2, 1, 3))
        s_loc = jnp.einsum("bqhd,bwkhd->bhqwk", qr, kr).astype(F32) * scale + bias[None]
        s_loc = jnp.where(in_win[:, None, :], s_loc, NEG_INF).reshape(B, H, GRID_W, wr * GRID_W)
        s_ctx = jnp.einsum("bqhd,bkhd->bhqk", qr, kc).astype(F32) * scale
        p = jax.nn.softmax(jnp.concatenate([s_loc, s_ctx], axis=-1), axis=-1).astype(v.dtype)
        n_loc = wr * GRID_W
        return (jnp.einsum("bhqk,bkhd->bqhd", p[..., :n_loc], vr.reshape(B, n_loc, H, dh))
                + jnp.einsum("bhqk,bkhd->bqhd", p[..., n_loc:], vc))

    o = lax.map(one_row, jnp.arange(rows))
    y_l = jnp.moveaxis(o, 0, 1).reshape(B, T, H * dh)
    y_c = _block_attention(qc, kc, vc).reshape(B, qc.shape[1], H * dh) if update_ctx else None
    return y_l, y_c


def kernel(x, c, ctx, c_ctx, w_ada, b_ada, norm1_g, norm2_g, w_in, gla_wa2, gla_ba, gla_norm_g, mla_q_norm_g, mla_w_uq, mla_kv_norm_g, mla_w_ukv, lru_conv_w, lru_conv_b, lru_wa, lru_ba, lru_wx, lru_bx, lru_lambda, na_rpb, w_branch, w_out, peer_wq, peer_keys, peer_u, peer_v, final_norm_g):
    B, T, D = x.shape
    Tc = ctx.shape[1]
    L = w_ada.shape[0]
    rows = T // GRID_W
    rope_cs = _axial_rope(T, MLA_ROPE)

    n_rows = -(-(B + 1) // SUBLANE) * SUBLANE
    cc = jnp.zeros((n_rows, D), F32).at[:B].set(c).at[B].set(c_ctx)
    mods = _ada(cc, w_ada, b_ada)

    mixp = -(-MIX_COLS // 512) * 512
    xc = ctx
    pe_l = pe_c = gate_l = gate_c = None
    for l in range(L):
        update_ctx = l < L - 1
        ml = jnp.split(mods[l, :B], 6, axis=-1)
        mc = jnp.split(jnp.broadcast_to(mods[l, B], (B, 6 * D)), 6, axis=-1)
        w_mix = jnp.pad(w_in[l, :, :MIX_COLS], ((0, 0), (0, mixp - MIX_COLS))).astype(BF16)
        w_gate = w_in[l, :, MIX_COLS:].astype(BF16)
        wb = w_branch[l].astype(BF16)
        wo = w_out[l].astype(BF16)
        wq = peer_wq[l].astype(BF16)
        pu = peer_u[l].astype(BF16)
        pv = peer_v[l].astype(BF16)

        xn, h = _norm(x, norm1_g[l], delta=pe_l, gate=gate_l, shift=ml[0], scale=ml[1])
        x = x if xn is None else xn
        xcn, hc = _norm(xc, norm1_g[l], delta=pe_c, gate=gate_c, shift=mc[0], scale=mc[1])
        xc = xc if xcn is None else xcn

        h2d = h.reshape(B * T, D)
        hc2d = hc.reshape(B * Tc, D)
        zl = _split_cols(_matmul(h2d, w_mix, out_dtype=F32).reshape(B, T, mixp)[..., :MIX_COLS])
        zc = _split_cols(_matmul(hc2d, w_mix, out_dtype=F32).reshape(B, Tc, mixp)[..., :MIX_COLS])
        ys = (
            _gla_mixer(zl[0:6], zc[0:6], gla_wa2[l], gla_ba[l], gla_norm_g[l], update_ctx),
            _mla_mixer(zl[6:9], zc[6:9], mla_q_norm_g[l], mla_w_uq[l], mla_kv_norm_g[l], mla_w_ukv[l],
                       rope_cs, update_ctx),
            _rglru_mixer(zl[9:11], zc[9:11], lru_conv_w[l], lru_conv_b[l], lru_wa[l], lru_ba[l],
                         lru_wx[l], lru_bx[l], lru_lambda[l], update_ctx),
            _na_mixer(zl[11:14], zc[11:14], na_rpb[l], rows, update_ctx),
        )

        def channel_mix(xs, hs, ys_s, m, n_tok):
            M = B * n_tok
            gates = _matmul(hs, w_gate, out_dtype=BF16, act="sigmoid")
            mrg = _merge([y.reshape(M, BRANCH_W).astype(BF16) for y in ys_s], gates, wb)
            xs = _matmul(mrg, wo, out_dtype=F32, res=xs.reshape(M, D), mod=m[2],
                         rows_per_batch=n_tok).reshape(B, n_tok, D)
            _, h2 = _norm(xs, norm2_g[l], shift=m[3], scale=m[4])
            pe = _peer(h2.reshape(M, D), wq, peer_keys[l], pu, pv).reshape(B, n_tok, D)
            return xs, pe

        x, pe_l = channel_mix(x, h2d, [y[0] for y in ys], ml, T)
        gate_l = ml[5]
        if update_ctx:
            xc, pe_c = channel_mix(xc, hc2d, [y[1] for y in ys], mc, Tc)
            gate_c = mc[5]
        else:
            pe_c = gate_c = None
    _, out = _norm(x, final_norm_g, delta=pe_l, gate=gate_l, out_dtype=F32, emit_x=False)
    return out
```

```python
import functools

import jax
import jax.numpy as jnp
import numpy as np
from jax import lax
from jax.experimental import pallas as pl
from jax.experimental.pallas import tpu as pltpu

GRID_W = 64
EPS = 1e-6
NEG_INF = -1e30
N_BRANCH = 4
BRANCH_W = 1024
ROPE_THETA = 10000.0

GLA_HEADS = 4
GLA_DK = 128
GLA_DV = BRANCH_W // GLA_HEADS
GLA_RANK = 16
GLA_TAU = 16.0
GLA_CHUNK = 64

MLA_HEADS = 8
MLA_Q_RANK = 768
MLA_KV_RANK = 256
MLA_NOPE = 128
MLA_ROPE = 64
MLA_DV = BRANCH_W // MLA_HEADS
MLA_DQP = 256

LRU_WIDTH = BRANCH_W
LRU_BLOCKS = 8
LRU_BW = LRU_WIDTH // LRU_BLOCKS
CONV_W = 4
LRU_C = 8.0

NA_HEADS = 8
NA_DH = BRANCH_W // NA_HEADS
NA_WIN_R = 8
NA_WIN_C = 16
NA_QROWS = 8
NA_KROWS = 16

PEER_HEADS = 8
PEER_DQ = 256
PEER_TOPK = 16

MIX_SIZES = (
    GLA_HEADS * GLA_DK, GLA_HEADS * GLA_DK, GLA_HEADS * GLA_DV, GLA_HEADS * GLA_DV, GLA_RANK, GLA_RANK,
    MLA_Q_RANK, MLA_KV_RANK, MLA_ROPE, LRU_WIDTH, LRU_WIDTH,
    NA_HEADS * NA_DH, NA_HEADS * NA_DH, NA_HEADS * NA_DH,
)
MIX_COLS = sum(MIX_SIZES)

COL_GV, COL_GG, COL_LG, COL_LX, COL_NQ, COL_NK, COL_NV = 0, 1024, 2048, 3072, 4096, 5120, 6144
COL_GQ, COL_GK = 7168, 7680
COL_MQKV = 8192
COL_KR = 9216
COL_DEC = 9344
MIXP = 9728

V7X_VMEM_BYTES = 64 * 1024 * 1024
VMEM_LIMIT = V7X_VMEM_BYTES - 8 * 1024 * 1024
LANE = 128
SUBLANE = 8

F32 = jnp.float32
BF16 = jnp.bfloat16
HIGHEST = lax.Precision.HIGHEST
NT_DIMS = (((1,), (1,)), ((), ()))
TN_DIMS = (((0,), (0,)), ((), ()))


def _tile(n, pref, mult=SUBLANE):
    if n <= pref:
        return n
    t = (pref // mult) * mult
    while t > mult and n % t:
        t -= mult
    assert n % t == 0, (n, pref, mult)
    return t


def _params(*sem):
    return pltpu.CompilerParams(dimension_semantics=sem, vmem_limit_bytes=VMEM_LIMIT)


def _bf(x):
    return x.astype(BF16)


def _ada_kernel(c_ref, w_ref, b_ref, o_ref):
    cv = c_ref[...]
    a = cv * jax.nn.sigmoid(cv)
    o_ref[0] = jnp.dot(a, w_ref[0], preferred_element_type=F32, precision=HIGHEST) + b_ref[0]


def _ada(cc, w_ada, b_ada):
    L, D, W = w_ada.shape
    R = cc.shape[0]
    tn = _tile(W, 512, LANE)
    return pl.pallas_call(
        _ada_kernel,
        grid=(L, W // tn),
        in_specs=[
            pl.BlockSpec((R, D), lambda l, j: (0, 0)),
            pl.BlockSpec((1, D, tn), lambda l, j: (l, 0, j)),
            pl.BlockSpec((1, 1, tn), lambda l, j: (l, 0, j)),
        ],
        out_specs=pl.BlockSpec((1, R, tn), lambda l, j: (l, 0, j)),
        out_shape=jax.ShapeDtypeStruct((L, R, W), F32),
        compiler_params=_params("arbitrary", "arbitrary"),
        name="ada_mod",
    )(cc, w_ada, b_ada.reshape(L, 1, W))


def _norm_kernel(*refs, has_delta, modulate, emit_x):
    it = iter(refs)
    x_ref = next(it)
    if has_delta:
        d_ref, gate_ref = next(it), next(it)
    g_ref = next(it)
    if modulate:
        shift_ref, scale_ref = next(it), next(it)
    if emit_x:
        xo_ref = next(it)
    h_ref = next(it)
    x = x_ref[0]
    if has_delta:
        x = x + gate_ref[0] * d_ref[0]
    if emit_x:
        xo_ref[0] = x
    y = x * lax.rsqrt(jnp.mean(x * x, axis=-1, keepdims=True) + EPS)
    y = y * g_ref[...]
    if modulate:
        y = y * (1.0 + scale_ref[0]) + shift_ref[0]
    h_ref[0] = y.astype(h_ref.dtype)


def _norm(x, g, *, delta=None, gate=None, shift=None, scale=None, out_dtype=None, emit_x=True):
    out_dtype = BF16 if out_dtype is None else out_dtype
    B, T, D = x.shape
    tt = _tile(T, 256)
    has_delta = delta is not None
    emit_x = emit_x and has_delta
    modulate = shift is not None
    tok = pl.BlockSpec((1, tt, D), lambda b, t: (b, t, 0))
    vec = pl.BlockSpec((1, 1, D), lambda b, t: (b, 0, 0))
    args, specs = [x], [tok]
    if has_delta:
        args += [delta, gate.reshape(B, 1, D)]
        specs += [tok, vec]
    args.append(g.reshape(1, D))
    specs.append(pl.BlockSpec((1, D), lambda b, t: (0, 0)))
    if modulate:
        args += [shift.reshape(B, 1, D), scale.reshape(B, 1, D)]
        specs += [vec, vec]
    out_shape, out_specs = [], []
    if emit_x:
        out_shape.append(jax.ShapeDtypeStruct((B, T, D), F32))
        out_specs.append(tok)
    out_shape.append(jax.ShapeDtypeStruct((B, T, D), out_dtype))
    out_specs.append(tok)
    outs = pl.pallas_call(
        functools.partial(_norm_kernel, has_delta=has_delta, modulate=modulate, emit_x=emit_x),
        grid=(B, T // tt),
        in_specs=specs,
        out_specs=out_specs,
        out_shape=out_shape,
        compiler_params=_params("arbitrary", "arbitrary"),
        name="res_norm_mod",
    )(*args)
    if emit_x:
        return outs[0], outs[1]
    return None, outs[0]


def _mm_kernel(*refs, act, has_res):
    if has_res:
        a_ref, b_ref, r_ref, m_ref, o_ref = refs
    else:
        a_ref, b_ref, o_ref = refs
    acc = jnp.dot(a_ref[...], b_ref[...], preferred_element_type=F32)
    if act == "sigmoid":
        acc = jax.nn.sigmoid(acc)
    if has_res:
        acc = r_ref[...] + m_ref[0] * acc
    o_ref[...] = acc.astype(o_ref.dtype)


def _matmul(a, b, *, out_dtype, act=None, res=None, mod=None, rows_per_batch=None, tm=1024, tn=512):
    M, K = a.shape
    N = b.shape[1]
    tm = _tile(rows_per_batch if rows_per_batch else M, tm)
    tn = _tile(N, tn, LANE)
    has_res = res is not None
    args = [a, b]
    specs = [pl.BlockSpec((tm, K), lambda i, j: (i, 0)), pl.BlockSpec((K, tn), lambda i, j: (0, j))]
    if has_res:
        args += [res, mod.reshape(mod.shape[0], 1, N)]
        specs += [
            pl.BlockSpec((tm, tn), lambda i, j: (i, j)),
            pl.BlockSpec((1, 1, tn), lambda i, j: ((i * tm) // rows_per_batch, 0, j)),
        ]
    return pl.pallas_call(
        functools.partial(_mm_kernel, act=act, has_res=has_res),
        grid=(M // tm, N // tn),
        in_specs=specs,
        out_specs=pl.BlockSpec((tm, tn), lambda i, j: (i, j)),
        out_shape=jax.ShapeDtypeStruct((M, N), out_dtype),
        compiler_params=_params("arbitrary", "arbitrary"),
        name="matmul",
    )(*args)


def _merge_kernel(y0, y1, y2, y3, g0, g1, g2, g3, w_ref, o_ref):
    acc = None
    for i, (y, g) in enumerate(((y0, g0), (y1, g1), (y2, g2), (y3, g3))):
        p = g[...].astype(F32) * jnp.dot(y[...], w_ref[i], preferred_element_type=F32)
        acc = p if acc is None else acc + p
    o_ref[...] = acc.astype(o_ref.dtype)


def _merge(ys, gates, w_branch):
    M = ys[0].shape[0]
    D = w_branch.shape[2]
    tm = _tile(M, 1024)
    tn = _tile(D, 512, LANE)
    nj = D // tn
    y_spec = pl.BlockSpec((tm, BRANCH_W), lambda i, j: (i, 0))
    g_specs = [pl.BlockSpec((tm, tn), functools.partial(lambda i, j, br: (i, br * nj + j), br=br))
               for br in range(N_BRANCH)]
    return pl.pallas_call(
        _merge_kernel,
        grid=(M // tm, nj),
        in_specs=[y_spec] * N_BRANCH + g_specs + [pl.BlockSpec((N_BRANCH, BRANCH_W, tn), lambda i, j: (0, 0, j))],
        out_specs=pl.BlockSpec((tm, tn), lambda i, j: (i, j)),
        out_shape=jax.ShapeDtypeStruct((M, D), BF16),
        compiler_params=_params("arbitrary", "arbitrary"),
        name="merge",
    )(*ys, gates, gates, gates, gates, w_branch)


def _topk_rows(s, k):
    n = s.shape[0]
    iota = lax.broadcasted_iota(jnp.int32, s.shape, 0)
    vals, sels = [], []
    for _ in range(k):
        m = jnp.max(s, axis=0, keepdims=True)
        am = jnp.min(jnp.where(s == m, iota, n), axis=0, keepdims=True)
        sel = iota == am
        vals.append(m)
        sels.append(sel)
        s = jnp.where(sel, -jnp.inf, s)
    return vals, sels


def _peer_route_kernel(q_ref, k_ref, g_ref, e1_ref, e2_ref):
    half = PEER_DQ // 2
    K = PEER_TOPK
    q = q_ref[...]
    halves = []
    for j in range(2):
        s = lax.dot_general(k_ref[0, j], q[:, j * half:(j + 1) * half], NT_DIMS,
                            preferred_element_type=F32, precision=HIGHEST)
        key_id = lax.broadcasted_iota(jnp.int32, s.shape, 0).astype(F32)
        vals, sels = _topk_rows(s, K)
        ids = [jnp.sum(jnp.where(sel, key_id, 0.0), axis=0, keepdims=True) for sel in sels]
        halves.append((jnp.concatenate(vals, axis=0), jnp.concatenate(ids, axis=0)))
    (v1, i1), (v2, i2) = halves
    cand = jnp.concatenate([v1[a:a + 1] + v2 for a in range(K)], axis=0)
    c1 = jnp.concatenate([jnp.broadcast_to(i1[a:a + 1], i2.shape) for a in range(K)], axis=0)
    c2 = jnp.concatenate([i2] * K, axis=0)
    top, sels = _topk_rows(cand, K)
    top = jnp.concatenate(top, axis=0)
    p = jnp.exp(top - top[0:1])
    g_ref[...] = p / jnp.sum(p, axis=0, keepdims=True)
    e1_ref[...] = jnp.concatenate([jnp.sum(jnp.where(sel, c1, 0.0), axis=0, keepdims=True) for sel in sels], axis=0)
    e2_ref[...] = jnp.concatenate([jnp.sum(jnp.where(sel, c2, 0.0), axis=0, keepdims=True) for sel in sels], axis=0)


def _peer_route(q, keys):
    M = q.shape[0]
    H, _, nk, half = keys.shape
    tm = _tile(M, 256, LANE)
    out = jax.ShapeDtypeStruct((H * PEER_TOPK, M), F32)
    o_spec = pl.BlockSpec((PEER_TOPK, tm), lambda i, h: (h, i))
    return pl.pallas_call(
        _peer_route_kernel,
        grid=(M // tm, H),
        in_specs=[
            pl.BlockSpec((tm, PEER_DQ), lambda i, h: (i, h)),
            pl.BlockSpec((1, 2, nk, half), lambda i, h: (h, 0, 0, 0)),
        ],
        out_specs=[o_spec, o_spec, o_spec],
        out_shape=[out, out, out],
        compiler_params=_params("arbitrary", "arbitrary"),
        name="peer_route",
    )(q, keys)


def _peer_w_kernel(g_ref, e1_ref, e2_ref, o_ref, gt_ref, e1t_ref, e2t_ref, *, nk):
    tm = o_ref.shape[0]
    gt_ref[...] = g_ref[...].T
    e1t_ref[...] = e1_ref[...].T
    e2t_ref[...] = e2_ref[...].T
    key_iota = lax.broadcasted_iota(jnp.int32, (nk, g_ref.shape[0]), 0).astype(F32)

    def body(t, carry):
        row = pl.ds(t, 1)
        a = jnp.where(e1t_ref[row, :] == key_iota, gt_ref[row, :], 0.0).astype(BF16)
        b = jnp.where(e2t_ref[row, :] == key_iota, 1.0, 0.0).astype(BF16)
        w = lax.dot_general(a, b, NT_DIMS, preferred_element_type=F32)
        o_ref[t] = w.astype(o_ref.dtype)
        return carry

    lax.fori_loop(0, tm, body, 0)


def _peer_w(g, e1, e2, nk):
    S, M = g.shape
    tm = _tile(M, 256, LANE)
    spec = pl.BlockSpec((S, tm), lambda i: (0, i))
    return pl.pallas_call(
        functools.partial(_peer_w_kernel, nk=nk),
        grid=(M // tm,),
        in_specs=[spec, spec, spec],
        out_specs=pl.BlockSpec((tm, nk, nk), lambda i: (i, 0, 0)),
        out_shape=jax.ShapeDtypeStruct((M, nk, nk), BF16),
        scratch_shapes=[pltpu.VMEM((tm, S), F32)] * 3,
        compiler_params=_params("arbitrary"),
        name="peer_route_weights",
    )(g, e1, e2)


def _gelu_tanh(x):
    return 0.5 * x * (1.0 + jnp.tanh(0.7978845608028654 * (x + 0.044715 * (x * x * x))))


def _peer_dense_kernel(h_ref, u_ref, v_ref, w_ref, o_ref):
    @pl.when(pl.program_id(1) == 0)
    def _():
        o_ref[...] = jnp.zeros_like(o_ref)

    s = lax.dot_general(h_ref[...], u_ref[...], NT_DIMS, preferred_element_type=F32)
    a = (_gelu_tanh(s) * w_ref[...].astype(F32)).astype(BF16)
    o_ref[...] += jnp.dot(a, v_ref[...], preferred_element_type=F32)


def _peer_dense(h2, u, v, w):
    M, D = h2.shape
    E = u.shape[0]
    tm = _tile(M, 512)
    te = _tile(E, 512, LANE)
    return pl.pallas_call(
        _peer_dense_kernel,
        grid=(M // tm, E // te),
        in_specs=[
            pl.BlockSpec((tm, D), lambda i, e: (i, 0)),
            pl.BlockSpec((te, D), lambda i, e: (e, 0)),
            pl.BlockSpec((te, D), lambda i, e: (e, 0)),
            pl.BlockSpec((tm, te), lambda i, e: (i, e)),
        ],
        out_specs=pl.BlockSpec((tm, D), lambda i, e: (i, 0)),
        out_shape=jax.ShapeDtypeStruct((M, D), F32),
        compiler_params=_params("arbitrary", "arbitrary"),
        name="peer_dense",
    )(h2, u, v, w)


def _peer(h2, wq, keys, u, v):
    nk = keys.shape[2]
    q = _matmul(h2, wq, out_dtype=F32)
    g, e1, e2 = _peer_route(q, keys)
    w = _peer_w(g, e1, e2, nk).reshape(h2.shape[0], nk * nk)
    return _peer_dense(h2, u, v, w)


def _attn_kernel(*refs, scale, two):
    if two:
        q_ref, k1_ref, v1_ref, k2_ref, v2_ref, o_ref = refs
    else:
        q_ref, k1_ref, v1_ref, o_ref = refs
    q = _bf(q_ref[0].astype(F32) * scale)
    s1 = lax.dot_general(q, _bf(k1_ref[0]), NT_DIMS, preferred_element_type=F32)
    m = jnp.max(s1, axis=-1, keepdims=True)
    if two:
        s2 = lax.dot_general(q, _bf(k2_ref[0]), NT_DIMS, preferred_element_type=F32)
        m = jnp.maximum(m, jnp.max(s2, axis=-1, keepdims=True))
    p1 = jnp.exp(s1 - m)
    l = jnp.sum(p1, axis=-1, keepdims=True)
    o = jnp.dot(_bf(p1), _bf(v1_ref[0]), preferred_element_type=F32)
    if two:
        p2 = jnp.exp(s2 - m)
        l = l + jnp.sum(p2, axis=-1, keepdims=True)
        o = o + jnp.dot(_bf(p2), _bf(v2_ref[0]), preferred_element_type=F32)
    o_ref[0] = (o / l).astype(o_ref.dtype)


def _attention(q, qcol, k1, k1col, v1, v1col, k2=None, k2col=0, v2=None, v2col=0, *, heads, dq, dv, scale):
    B, Tq, _ = q.shape
    tq = _tile(Tq, 256)
    two = k2 is not None

    def spec(arr, col, w, tiled):
        n = arr.shape[1]
        if tiled:
            return pl.BlockSpec((1, tq, w), lambda b, h, t: (b, t, col // w + h))
        return pl.BlockSpec((1, n, w), lambda b, h, t: (b, 0, col // w + h))

    args = [q, k1, v1]
    specs = [spec(q, qcol, dq, True), spec(k1, k1col, dq, False), spec(v1, v1col, dv, False)]
    if two:
        args += [k2, v2]
        specs += [spec(k2, k2col, dq, False), spec(v2, v2col, dv, False)]
    return pl.pallas_call(
        functools.partial(_attn_kernel, scale=scale, two=two),
        grid=(B, heads, Tq // tq),
        in_specs=specs,
        out_specs=pl.BlockSpec((1, tq, dv), lambda b, h, t: (b, t, h)),
        out_shape=jax.ShapeDtypeStruct((B, Tq, heads * dv), BF16),
        compiler_params=_params("arbitrary", "arbitrary", "arbitrary"),
        name="attention",
    )(*args)


def _mla_prep_kernel(z_ref, zr_ref, c_ref, s_ref, gq_ref, gkv_ref, wq_ref, wkv_ref, q_out, k_out, v_out):
    z = z_ref[0]

    def rms(x, g):
        return x * lax.rsqrt(jnp.mean(x * x, axis=-1, keepdims=True) + EPS) * g

    q = jnp.dot(_bf(rms(z[:, :MLA_Q_RANK], gq_ref[...])), wq_ref[...], preferred_element_type=F32)
    kv = jnp.dot(_bf(rms(z[:, MLA_Q_RANK:], gkv_ref[...])), wkv_ref[...], preferred_element_type=F32)
    cos, sin = c_ref[...], s_ref[...]

    def rope(x):
        return x * cos + pltpu.roll(x, MLA_ROPE, 1) * sin

    kr = rope(zr_ref[0]).astype(k_out.dtype)
    hv = MLA_HEADS * MLA_NOPE
    for h in range(MLA_HEADS):
        lo = h * MLA_DQP
        q_out[0, :, lo:lo + MLA_NOPE] = q[:, lo:lo + MLA_NOPE].astype(q_out.dtype)
        q_out[0, :, lo + MLA_NOPE:lo + MLA_DQP] = rope(q[:, lo + MLA_NOPE:lo + MLA_DQP]).astype(q_out.dtype)
        k_out[0, :, lo:lo + MLA_NOPE] = kv[:, h * MLA_NOPE:(h + 1) * MLA_NOPE].astype(k_out.dtype)
        k_out[0, :, lo + MLA_NOPE:lo + MLA_DQP] = kr
    v_out[0] = kv[:, hv:].astype(v_out.dtype)


def _mla_prep(z, cos_t, sin_t, gq, gkv, wq_p, wkv_p):
    B, T, _ = z.shape
    tt = _tile(T, 256)
    wq_w = MLA_HEADS * MLA_DQP
    wkv_w = MLA_HEADS * (MLA_NOPE + MLA_DV)
    cw = MLA_Q_RANK + MLA_KV_RANK
    tab = pl.BlockSpec((tt, LANE), lambda b, t: (t, 0))
    return pl.pallas_call(
        _mla_prep_kernel,
        grid=(B, T // tt),
        in_specs=[
            pl.BlockSpec((1, tt, cw), lambda b, t: (b, t, COL_MQKV // cw)),
            pl.BlockSpec((1, tt, LANE), lambda b, t: (b, t, COL_KR // LANE)),
            tab, tab,
            pl.BlockSpec((1, MLA_Q_RANK), lambda b, t: (0, 0)),
            pl.BlockSpec((1, MLA_KV_RANK), lambda b, t: (0, 0)),
            pl.BlockSpec((MLA_Q_RANK, wq_w), lambda b, t: (0, 0)),
            pl.BlockSpec((MLA_KV_RANK, wkv_w), lambda b, t: (0, 0)),
        ],
        out_specs=[
            pl.BlockSpec((1, tt, wq_w), lambda b, t: (b, t, 0)),
            pl.BlockSpec((1, tt, wq_w), lambda b, t: (b, t, 0)),
            pl.BlockSpec((1, tt, MLA_HEADS * MLA_DV), lambda b, t: (b, t, 0)),
        ],
        out_shape=[
            jax.ShapeDtypeStruct((B, T, wq_w), BF16),
            jax.ShapeDtypeStruct((B, T, wq_w), BF16),
            jax.ShapeDtypeStruct((B, T, MLA_HEADS * MLA_DV), BF16),
        ],
        compiler_params=_params("arbitrary", "arbitrary"),
        name="mla_prep",
    )(z, z, cos_t, sin_t, gq.reshape(1, -1), gkv.reshape(1, -1), wq_p, wkv_p)


def _mla_weights(w_uq, w_ukv):
    half = MLA_ROPE // 2
    wq = w_uq.reshape(MLA_Q_RANK, MLA_HEADS, MLA_NOPE + MLA_ROPE)
    pe = wq[..., MLA_NOPE:]
    pe_sw = jnp.concatenate([pe[..., half:], pe[..., :half]], axis=-1)
    wq_p = jnp.concatenate([wq[..., :MLA_NOPE], pe, pe_sw], axis=-1).reshape(MLA_Q_RANK, MLA_HEADS * MLA_DQP)
    wkv = w_ukv.reshape(MLA_KV_RANK, MLA_HEADS, MLA_NOPE + MLA_DV)
    wkv_p = jnp.concatenate([wkv[..., :MLA_NOPE].reshape(MLA_KV_RANK, -1),
                             wkv[..., MLA_NOPE:].reshape(MLA_KV_RANK, -1)], axis=-1)
    return _bf(wq_p), _bf(wkv_p)


def _rope_tables(n_tok, with_pos):
    n_freq = MLA_ROPE // 4
    zeros = jnp.zeros((n_tok, LANE - MLA_ROPE), F32)
    if not with_pos:
        return (jnp.concatenate([jnp.ones((n_tok, MLA_ROPE), F32), zeros], axis=1),
                jnp.zeros((n_tok, LANE), F32))
    t = jnp.arange(n_tok, dtype=jnp.int32)
    row = (t // GRID_W).astype(F32)
    col = (t % GRID_W).astype(F32)
    inv = ROPE_THETA ** (-jnp.arange(n_freq, dtype=F32) / n_freq)
    ang = jnp.concatenate([row[:, None] * inv, col[:, None] * inv], axis=-1)
    cos, sin = jnp.cos(ang), jnp.sin(ang)
    return (jnp.concatenate([cos, cos, zeros], axis=1), jnp.concatenate([-sin, sin, zeros], axis=1))


def _na_kernel(q_ref, k_ref, v_ref, kc_ref, vc_ref, bm_ref, o_ref, *, scale, rows):
    nblk = rows // NA_QROWS
    nq = NA_QROWS * GRID_W
    nkw = NA_KROWS * GRID_W
    kc = _bf(kc_ref[0])
    vc = _bf(vc_ref[0])

    def body(blk, carry):
        q0 = pl.multiple_of(blk * nq, nq)
        kb = jnp.clip(NA_QROWS * blk - NA_WIN_R // 2, 0, rows - NA_KROWS)
        k0 = pl.multiple_of(kb * GRID_W, (NA_WIN_R // 2) * GRID_W)
        pat = jnp.where(blk == 0, 0, jnp.where(blk == nblk - 1, 2, 1))
        q = _bf(q_ref[0, pl.ds(q0, nq), :] * scale)
        kw = _bf(k_ref[0, pl.ds(k0, nkw), :])
        vw = _bf(v_ref[0, pl.ds(k0, nkw), :])
        bm = bm_ref[0, pat]
        s = lax.dot_general(q, kw, NT_DIMS, preferred_element_type=F32)
        s = jnp.where(bm > 0.5 * NEG_INF, s + bm, NEG_INF)
        sc = lax.dot_general(q, kc, NT_DIMS, preferred_element_type=F32)
        m = jnp.maximum(jnp.max(s, axis=-1, keepdims=True), jnp.max(sc, axis=-1, keepdims=True))
        p = jnp.exp(s - m)
        pc = jnp.exp(sc - m)
        l = jnp.sum(p, axis=-1, keepdims=True) + jnp.sum(pc, axis=-1, keepdims=True)
        o = jnp.dot(_bf(p), vw, preferred_element_type=F32) + jnp.dot(_bf(pc), vc, preferred_element_type=F32)
        o_ref[0, pl.ds(q0, nq), :] = (o / l).astype(o_ref.dtype)
        return carry

    lax.fori_loop(0, nblk, body, 0)


def _na_bias_table(rpb, rows):
    nblk = rows // NA_QROWS
    col = np.arange(GRID_W)
    c_start = np.clip(col - NA_WIN_C // 2, 0, GRID_W - NA_WIN_C)
    in_win = (col[None, :] >= c_start[:, None]) & (col[None, :] < c_start[:, None] + NA_WIN_C)
    dc = np.clip(col[None, :] - col[:, None] + NA_WIN_C - 1, 0, 2 * NA_WIN_C - 2)
    dc_onehot = (dc[:, :, None] == np.arange(2 * NA_WIN_C - 1)).astype(np.float32)
    toeplitz = jnp.einsum("hab,qkb->haqk", rpb.astype(F32), dc_onehot, precision=HIGHEST)
    n_dr = 2 * NA_WIN_R - 1
    sel = np.zeros((3, NA_QROWS, NA_KROWS, n_dr), np.float32)
    valid = np.zeros((3, NA_QROWS, NA_KROWS), bool)
    for p, blk in enumerate((0, min(1, nblk - 1), nblk - 1)):
        kb = int(np.clip(NA_QROWS * blk - NA_WIN_R // 2, 0, rows - NA_KROWS))
        for rq in range(NA_QROWS):
            r = NA_QROWS * blk + rq
            r_start = int(np.clip(r - NA_WIN_R // 2, 0, rows - NA_WIN_R))
            for rk in range(NA_KROWS):
                kr = kb + rk
                if r_start <= kr < r_start + NA_WIN_R:
                    sel[p, rq, rk, kr - r + NA_WIN_R - 1] = 1.0
                    valid[p, rq, rk] = True
    bias = jnp.einsum("prka,haqc->hprqkc", sel, toeplitz, precision=HIGHEST)
    mask = valid[None, :, :, None, :, None] & in_win[None, None, None, :, None, :]
    H = rpb.shape[0]
    return jnp.where(mask, bias, NEG_INF).reshape(H, 3, NA_QROWS * GRID_W, NA_KROWS * GRID_W)


def _na(z, zc, rpb):
    B, T, _ = z.shape
    Tc = zc.shape[1]
    rows = T // GRID_W
    assert rows % NA_QROWS == 0 and rows >= NA_KROWS
    bm = _na_bias_table(rpb, rows)
    dh = NA_DH

    def seq(n, col):
        return pl.BlockSpec((1, n, dh), lambda b, h: (b, 0, col // dh + h))

    return pl.pallas_call(
        functools.partial(_na_kernel, scale=dh ** -0.5, rows=rows),
        grid=(B, NA_HEADS),
        in_specs=[seq(T, COL_NQ), seq(T, COL_NK), seq(T, COL_NV), seq(Tc, COL_NK), seq(Tc, COL_NV),
                  pl.BlockSpec((1,) + bm.shape[1:], lambda b, h: (h, 0, 0, 0))],
        out_specs=pl.BlockSpec((1, T, dh), lambda b, h: (b, 0, h)),
        out_shape=jax.ShapeDtypeStruct((B, T, NA_HEADS * dh), BF16),
        compiler_params=_params("arbitrary", "arbitrary"),
        name="neighbourhood_attention",
    )(z, z, z, zc, zc, bm)


def _lru_kernel(g_ref, x_ref, gc_ref, xc_ref, cw_ref, cb_ref, wa_ref, ba_ref, wx_ref, bx_ref, lam_ref,
                y_ref, yc_ref, a_scr, u_scr, h_scr):
    row8 = lax.broadcasted_iota(jnp.int32, (SUBLANE, LANE), 0)

    def scan_pair(n_tok, h0f, h0b):
        nb = n_tok // SUBLANE

        def body(i, carry):
            hf, hb = carry
            rf = pl.multiple_of(i * SUBLANE, SUBLANE)
            rb = pl.multiple_of((nb - 1 - i) * SUBLANE, SUBLANE)
            A, U = a_scr[0, pl.ds(rf, SUBLANE), :], u_scr[0, pl.ds(rf, SUBLANE), :]
            Ab, Ub = a_scr[1, pl.ds(rb, SUBLANE), :], u_scr[1, pl.ds(rb, SUBLANE), :]
            for s in (1, 2, 4):
                m = row8 >= s
                U = jnp.where(m, A * pltpu.roll(U, s, 0) + U, U)
                A = jnp.where(m, A * pltpu.roll(A, s, 0), A)
                mb = row8 < SUBLANE - s
                Ub = jnp.where(mb, Ab * pltpu.roll(Ub, SUBLANE - s, 0) + Ub, Ub)
                Ab = jnp.where(mb, Ab * pltpu.roll(Ab, SUBLANE - s, 0), Ab)
            hbf = A * hf + U
            hbb = Ab * hb + Ub
            h_scr[0, pl.ds(rf, SUBLANE), :] = hbf
            h_scr[1, pl.ds(rb, SUBLANE), :] = hbb
            return hbf[SUBLANE - 1:SUBLANE, :], hbb[0:1, :]

        return lax.fori_loop(0, nb, body, (h0f, h0b))

    def run(gate_ref, zx_ref, out_ref, n_tok, h0f, h0b):
        x = zx_ref[0]
        t = lax.broadcasted_iota(jnp.int32, x.shape, 0)
        w = cw_ref[...]
        xc = (w[0:1] * jnp.where(t >= 2, pltpu.roll(x, 2, 0), 0.0)
              + w[1:2] * jnp.where(t >= 1, pltpu.roll(x, 1, 0), 0.0)
              + w[2:3] * x
              + w[3:4] * jnp.where(t < n_tok - 1, pltpu.roll(x, n_tok - 1, 0), 0.0)) + cb_ref[...]
        xcb = _bf(xc)
        for d in range(2):
            r = jax.nn.sigmoid(jnp.dot(xcb, wa_ref[d, 0], preferred_element_type=F32) + ba_ref[d:d + 1, :])
            i = jax.nn.sigmoid(jnp.dot(xcb, wx_ref[d, 0], preferred_element_type=F32) + bx_ref[d:d + 1, :])
            nl = -lam_ref[d:d + 1, :]
            softplus = jnp.maximum(nl, 0.0) + jnp.log1p(jnp.exp(-jnp.abs(nl)))
            log_a = -LRU_C * r * softplus
            a = jnp.exp(log_a)
            a_scr[d, 0:n_tok, :] = a
            u_scr[d, 0:n_tok, :] = jnp.sqrt(1.0 - jnp.exp(2.0 * log_a)) * i * xc
        hf, hb = scan_pair(n_tok, h0f, h0b)
        h = h_scr[0, 0:n_tok, :] + h_scr[1, 0:n_tok, :]
        out_ref[0] = (_gelu_tanh(gate_ref[0]) * h).astype(out_ref.dtype)
        return hf, hb

    zero = jnp.zeros((1, LANE), F32)
    hf, hb = run(gc_ref, xc_ref, yc_ref, xc_ref.shape[1], zero, zero)
    run(g_ref, x_ref, y_ref, x_ref.shape[1], hf, hb)


def _lru(z, zc, conv_w, conv_b, wa, ba, wx, bx, lam):
    B, T, _ = z.shape
    Tc = zc.shape[1]
    bw = LRU_BW

    def seq(n, col):
        return pl.BlockSpec((1, n, bw), lambda b, j: (b, 0, col // bw + j))

    vec2 = pl.BlockSpec((2, bw), lambda b, j: (0, j))
    mat = pl.BlockSpec((2, 1, bw, bw), lambda b, j: (0, j, 0, 0))
    out = lambda n: pl.BlockSpec((1, n, bw), lambda b, j: (b, 0, j))
    return pl.pallas_call(
        _lru_kernel,
        grid=(B, LRU_BLOCKS),
        in_specs=[seq(T, COL_LG), seq(T, COL_LX), seq(Tc, COL_LG), seq(Tc, COL_LX),
                  pl.BlockSpec((CONV_W, bw), lambda b, j: (0, j)), pl.BlockSpec((1, bw), lambda b, j: (0, j)),
                  mat, vec2, mat, vec2, vec2],
        out_specs=[out(T), out(Tc)],
        out_shape=[jax.ShapeDtypeStruct((B, T, LRU_WIDTH), BF16), jax.ShapeDtypeStruct((B, Tc, LRU_WIDTH), BF16)],
        scratch_shapes=[pltpu.VMEM((2, T, bw), F32)] * 3,
        compiler_params=_params("arbitrary", "arbitrary"),
        name="rg_lru",
    )(z, z, zc, zc, conv_w, conv_b.reshape(1, -1), _bf(wa), ba, _bf(wx), bx, lam)


def _gla_kernel(q_ref, k_ref, v_ref, g_ref, d_ref, qc_ref, kc_ref, vc_ref, gc_ref, dc_ref,
                wa_ref, ba_ref, ng_ref, y_ref, yc_ref, la_scr, o_scr):
    C = GLA_CHUNK
    ri = lax.broadcasted_iota(jnp.int32, (C, C), 0)
    ci = lax.broadcasted_iota(jnp.int32, (C, C), 1)
    keep = (ri >= ci, ri <= ci)
    tri = (keep[0].astype(F32), keep[1].astype(F32))
    scale = GLA_DK ** -0.5

    def chunk(refs, c, d, st):
        qr, kr, vr = refs
        r0 = pl.multiple_of(c * C, C)
        rows = pl.ds(r0, C)
        b = jnp.dot(tri[d], la_scr[d, rows, :], preferred_element_type=F32, precision=HIGHEST)
        q = qr[0, rows, :] * scale
        k = kr[0, rows, :]
        v = _bf(vr[0, rows, :])
        qd = _bf(q * jnp.exp(b))
        att = lax.dot_general(qd, _bf(k * jnp.exp(-b)), NT_DIMS, preferred_element_type=F32)
        att = jnp.where(keep[d], att, 0.0)
        o = (jnp.dot(_bf(att), v, preferred_element_type=F32)
             + lax.dot_general(qd, _bf(st), NT_DIMS, preferred_element_type=F32))
        b_tot = b[C - 1:C, :] if d == 0 else b[0:1, :]
        ke = _bf(k * jnp.exp(b_tot - b))
        st = st * jnp.exp(b_tot) + lax.dot_general(v, ke, TN_DIMS, preferred_element_type=F32)
        o_scr[d, rows, :] = o
        return st

    def run(refs, gate_ref, dec_ref, out_ref, n_tok, st_f, st_b):
        dec = dec_ref[0]
        for d in range(2):
            x = jnp.dot(dec, wa_ref[d], preferred_element_type=F32, precision=HIGHEST) + ba_ref[d:d + 1, :]
            log_sig = jnp.minimum(x, 0.0) - jnp.log1p(jnp.exp(-jnp.abs(x)))
            la_scr[d, 0:n_tok, :] = log_sig * (1.0 / GLA_TAU)
        n = n_tok // C

        def body(i, carry):
            sf, sb = carry
            return chunk(refs, i, 0, sf), chunk(refs, n - 1 - i, 1, sb)

        st_f, st_b = lax.fori_loop(0, n, body, (st_f, st_b))
        o = o_scr[0, 0:n_tok, :] + o_scr[1, 0:n_tok, :]
        o = o * lax.rsqrt(jnp.mean(o * o, axis=-1, keepdims=True) + EPS) * ng_ref[...]
        gate = gate_ref[0]
        out_ref[0] = (o * (gate * jax.nn.sigmoid(gate))).astype(out_ref.dtype)
        return st_f, st_b

    zero = jnp.zeros((GLA_DV, GLA_DK), F32)
    st_f, st_b = run((qc_ref, kc_ref, vc_ref), gc_ref, dc_ref, yc_ref, qc_ref.shape[1], zero, zero)
    run((q_ref, k_ref, v_ref), g_ref, d_ref, y_ref, q_ref.shape[1], st_f, st_b)


def _gla(z, zc, wa2, ba, norm_g):
    B, T, _ = z.shape
    Tc = zc.shape[1]
    wa_p = jnp.zeros((2, LANE, GLA_HEADS * GLA_DK), F32)
    wa_p = wa_p.at[0, :GLA_RANK].set(wa2[0]).at[1, GLA_RANK:2 * GLA_RANK].set(wa2[1])

    def seq(n, col, w):
        return pl.BlockSpec((1, n, w), lambda b, h: (b, 0, col // w + h), pipeline_mode=pl.Buffered(1))

    def dec(n):
        return pl.BlockSpec((1, n, LANE), lambda b, h: (b, 0, COL_DEC // LANE), pipeline_mode=pl.Buffered(1))

    def ins(n):
        return [seq(n, COL_GQ, GLA_DK), seq(n, COL_GK, GLA_DK), seq(n, COL_GV, GLA_DV), seq(n, COL_GG, GLA_DV), dec(n)]

    out = lambda n: pl.BlockSpec((1, n, GLA_DV), lambda b, h: (b, 0, h))
    return pl.pallas_call(
        _gla_kernel,
        grid=(B, GLA_HEADS),
        in_specs=ins(T) + ins(Tc) + [
            pl.BlockSpec((2, LANE, GLA_DK), lambda b, h: (0, 0, h)),
            pl.BlockSpec((2, GLA_DK), lambda b, h: (0, h)),
            pl.BlockSpec((1, GLA_DV), lambda b, h: (0, 0)),
        ],
        out_specs=[out(T), out(Tc)],
        out_shape=[jax.ShapeDtypeStruct((B, T, BRANCH_W), BF16), jax.ShapeDtypeStruct((B, Tc, BRANCH_W), BF16)],
        scratch_shapes=[pltpu.VMEM((2, T, GLA_DK), F32), pltpu.VMEM((2, T, GLA_DV), F32)],
        compiler_params=_params("arbitrary", "arbitrary"),
        name="gla",
    )(z, z, z, z, z, zc, zc, zc, zc, zc, wa_p, ba, norm_g.reshape(1, -1))


def _permute_mix_weight(w):
    o = np.cumsum((0,) + MIX_SIZES)
    gq, gk, gv, gg, af, ab, cq, ckv, kr, lg, lx, nq, nk, nv = [w[:, o[i]:o[i + 1]] for i in range(len(MIX_SIZES))]
    half = MLA_ROPE // 2
    kr_sw = jnp.concatenate([kr[:, half:], kr[:, :half]], axis=1)
    pad = jnp.zeros((w.shape[0], MIXP - (COL_DEC + 2 * GLA_RANK)), w.dtype)
    return _bf(jnp.concatenate([gv, gg, lg, lx, nq, nk, nv, gq, gk, cq, ckv, kr, kr_sw, af, ab, pad], axis=1))


def kernel(x, c, ctx, c_ctx, w_ada, b_ada, norm1_g, norm2_g, w_in, gla_wa2, gla_ba, gla_norm_g, mla_q_norm_g, mla_w_uq, mla_kv_norm_g, mla_w_ukv, lru_conv_w, lru_conv_b, lru_wa, lru_ba, lru_wx, lru_bx, lru_lambda, na_rpb, w_branch, w_out, peer_wq, peer_keys, peer_u, peer_v, final_norm_g):
    B, T, D = x.shape
    Tc = ctx.shape[1]
    L = w_ada.shape[0]
    rope_l = _rope_tables(T, True)
    rope_c = _rope_tables(Tc, False)

    n_rows = -(-(B + 1) // SUBLANE) * SUBLANE
    cc = jnp.zeros((n_rows, D), F32).at[:B].set(c).at[B].set(c_ctx)
    mods = _ada(cc, w_ada, b_ada)

    xc = ctx
    pe_l = pe_c = gate_l = gate_c = None
    for l in range(L):
        update_ctx = l < L - 1
        ml = jnp.split(mods[l, :B], 6, axis=-1)
        mc = jnp.split(jnp.broadcast_to(mods[l, B], (B, 6 * D)), 6, axis=-1)
        w_mix = _permute_mix_weight(w_in[l, :, :MIX_COLS])
        w_gate = _bf(w_in[l, :, MIX_COLS:])
        wb = _bf(w_branch[l])
        wo = _bf(w_out[l])
        wq = _bf(peer_wq[l])
        pu = _bf(peer_u[l])
        pv = _bf(peer_v[l])
        mla_wq, mla_wkv = _mla_weights(mla_w_uq[l], mla_w_ukv[l])

        xn, h = _norm(x, norm1_g[l], delta=pe_l, gate=gate_l, shift=ml[0], scale=ml[1])
        x = x if xn is None else xn
        xcn, hc = _norm(xc, norm1_g[l], delta=pe_c, gate=gate_c, shift=mc[0], scale=mc[1])
        xc = xc if xcn is None else xcn

        h2d = h.reshape(B * T, D)
        hc2d = hc.reshape(B * Tc, D)
        z = _matmul(h2d, w_mix, out_dtype=F32).reshape(B, T, MIXP)
        zc = _matmul(hc2d, w_mix, out_dtype=F32).reshape(B, Tc, MIXP)

        y_gla, yc_gla = _gla(z, zc, gla_wa2[l], gla_ba[l], gla_norm_g[l])
        y_lru, yc_lru = _lru(z, zc, lru_conv_w[l], lru_conv_b[l], lru_wa[l], lru_ba[l], lru_wx[l], lru_bx[l],
                             lru_lambda[l])
        ql, kl, vl = _mla_prep(z, *rope_l, mla_q_norm_g[l], mla_kv_norm_g[l], mla_wq, mla_wkv)
        qc, kc, vc = _mla_prep(zc, *rope_c, mla_q_norm_g[l], mla_kv_norm_g[l], mla_wq, mla_wkv)
        mla_args = dict(heads=MLA_HEADS, dq=MLA_DQP, dv=MLA_DV, scale=(MLA_NOPE + MLA_ROPE) ** -0.5)
        y_mla = _attention(ql, 0, kc, 0, vc, 0, kl, 0, vl, 0, **mla_args)
        y_na = _na(z, zc, na_rpb[l])
        ys_l = [y_gla, y_mla, y_lru, y_na]
        if update_ctx:
            yc_mla = _attention(qc, 0, kc, 0, vc, 0, **mla_args)
            yc_na = _attention(zc, COL_NQ, zc, COL_NK, zc, COL_NV, heads=NA_HEADS, dq=NA_DH, dv=NA_DH,
                               scale=NA_DH ** -0.5)
            ys_c = [yc_gla, yc_mla, yc_lru, yc_na]

        def channel_mix(xs, hs, ys_s, m, n_tok):
            M = B * n_tok
            gates = _matmul(hs, w_gate, out_dtype=BF16, act="sigmoid")
            mrg = _merge([y.reshape(M, BRANCH_W) for y in ys_s], gates, wb)
            xs = _matmul(mrg, wo, out_dtype=F32, res=xs.reshape(M, D), mod=m[2],
                         rows_per_batch=n_tok).reshape(B, n_tok, D)
            _, h2 = _norm(xs, norm2_g[l], shift=m[3], scale=m[4])
            pe = _peer(h2.reshape(M, D), wq, peer_keys[l], pu, pv).reshape(B, n_tok, D)
            return xs, pe

        x, pe_l = channel_mix(x, h2d, ys_l, ml, T)
        gate_l = ml[5]
        if update_ctx:
            xc, pe_c = channel_mix(xc, hc2d, ys_c, mc, Tc)
            gate_c = mc[5]
        else:
            pe_c = gate_c = None
    _, out = _norm(x, final_norm_g, delta=pe_l, gate=gate_l, out_dtype=F32, emit_x=False)
    return out
```

```python
import functools

import jax
import jax.numpy as jnp
import numpy as np
from jax import lax
from jax.experimental import pallas as pl
from jax.experimental.pallas import tpu as pltpu

GRID_W = 64
EPS = 1e-6
NEG_INF = -1e30
N_BRANCH = 4
BRANCH_W = 1024
ROPE_THETA = 10000.0

GLA_HEADS = 4
GLA_DK = 128
GLA_DV = BRANCH_W // GLA_HEADS
GLA_RANK = 16
GLA_TAU = 16.0
GLA_CHUNK = 64

MLA_HEADS = 8
MLA_Q_RANK = 768
MLA_KV_RANK = 256
MLA_NOPE = 128
MLA_ROPE = 64
MLA_DV = BRANCH_W // MLA_HEADS
MLA_DQP = 256

LRU_WIDTH = BRANCH_W
LRU_BLOCKS = 8
LRU_BW = LRU_WIDTH // LRU_BLOCKS
CONV_W = 4
LRU_C = 8.0

NA_HEADS = 8
NA_DH = BRANCH_W // NA_HEADS
NA_WIN_R = 8
NA_WIN_C = 16
NA_QROWS = 8
NA_KROWS = 16

PEER_HEADS = 8
PEER_DQ = 256
PEER_TOPK = 16

MIX_SIZES = (
    GLA_HEADS * GLA_DK, GLA_HEADS * GLA_DK, GLA_HEADS * GLA_DV, GLA_HEADS * GLA_DV, GLA_RANK, GLA_RANK,
    MLA_Q_RANK, MLA_KV_RANK, MLA_ROPE, LRU_WIDTH, LRU_WIDTH,
    NA_HEADS * NA_DH, NA_HEADS * NA_DH, NA_HEADS * NA_DH,
)
MIX_COLS = sum(MIX_SIZES)

COL_GV, COL_GG, COL_LG, COL_LX, COL_NQ, COL_NK, COL_NV = 0, 1024, 2048, 3072, 4096, 5120, 6144
COL_GQ, COL_GK = 7168, 7680
COL_MQKV = 8192
COL_KR = 9216
COL_DEC = 9344
MIXP = 9728

V7X_VMEM_BYTES = 64 * 1024 * 1024
VMEM_LIMIT = V7X_VMEM_BYTES - 8 * 1024 * 1024
LANE = 128
SUBLANE = 8

F32 = jnp.float32
BF16 = jnp.bfloat16
HIGHEST = lax.Precision.HIGHEST
NT_DIMS = (((1,), (1,)), ((), ()))
TN_DIMS = (((0,), (0,)), ((), ()))


def _tile(n, pref, mult=SUBLANE):
    if n <= pref:
        return n
    t = (pref // mult) * mult
    while t > mult and n % t:
        t -= mult
    assert n % t == 0, (n, pref, mult)
    return t


def _params(*sem):
    return pltpu.CompilerParams(dimension_semantics=sem, vmem_limit_bytes=VMEM_LIMIT)


def _bf(x):
    return x.astype(BF16)


def _ada_kernel(c_ref, w_ref, b_ref, o_ref):
    cv = c_ref[...]
    a = cv * jax.nn.sigmoid(cv)
    o_ref[0] = jnp.dot(a, w_ref[0], preferred_element_type=F32, precision=HIGHEST) + b_ref[0]


def _ada(cc, w_ada, b_ada):
    L, D, W = w_ada.shape
    R = cc.shape[0]
    tn = _tile(W, 512, LANE)
    return pl.pallas_call(
        _ada_kernel,
        grid=(L, W // tn),
        in_specs=[
            pl.BlockSpec((R, D), lambda l, j: (0, 0)),
            pl.BlockSpec((1, D, tn), lambda l, j: (l, 0, j)),
            pl.BlockSpec((1, 1, tn), lambda l, j: (l, 0, j)),
        ],
        out_specs=pl.BlockSpec((1, R, tn), lambda l, j: (l, 0, j)),
        out_shape=jax.ShapeDtypeStruct((L, R, W), F32),
        compiler_params=_params("arbitrary", "arbitrary"),
        name="ada_mod",
    )(cc, w_ada, b_ada.reshape(L, 1, W))


def _norm_kernel(*refs, has_delta, modulate, emit_x):
    it = iter(refs)
    x_ref = next(it)
    if has_delta:
        d_ref, gate_ref = next(it), next(it)
    g_ref = next(it)
    if modulate:
        shift_ref, scale_ref = next(it), next(it)
    if emit_x:
        xo_ref = next(it)
    h_ref = next(it)
    x = x_ref[0]
    if has_delta:
        x = x + gate_ref[0] * d_ref[0]
    if emit_x:
        xo_ref[0] = x
    y = x * lax.rsqrt(jnp.mean(x * x, axis=-1, keepdims=True) + EPS)
    y = y * g_ref[...]
    if modulate:
        y = y * (1.0 + scale_ref[0]) + shift_ref[0]
    h_ref[0] = y.astype(h_ref.dtype)


def _norm(x, g, *, delta=None, gate=None, shift=None, scale=None, out_dtype=None, emit_x=True):
    out_dtype = BF16 if out_dtype is None else out_dtype
    B, T, D = x.shape
    tt = _tile(T, 256)
    has_delta = delta is not None
    emit_x = emit_x and has_delta
    modulate = shift is not None
    tok = pl.BlockSpec((1, tt, D), lambda b, t: (b, t, 0))
    vec = pl.BlockSpec((1, 1, D), lambda b, t: (b, 0, 0))
    args, specs = [x], [tok]
    if has_delta:
        args += [delta, gate.reshape(B, 1, D)]
        specs += [tok, vec]
    args.append(g.reshape(1, D))
    specs.append(pl.BlockSpec((1, D), lambda b, t: (0, 0)))
    if modulate:
        args += [shift.reshape(B, 1, D), scale.reshape(B, 1, D)]
        specs += [vec, vec]
    out_shape, out_specs = [], []
    if emit_x:
        out_shape.append(jax.ShapeDtypeStruct((B, T, D), F32))
        out_specs.append(tok)
    out_shape.append(jax.ShapeDtypeStruct((B, T, D), out_dtype))
    out_specs.append(tok)
    outs = pl.pallas_call(
        functools.partial(_norm_kernel, has_delta=has_delta, modulate=modulate, emit_x=emit_x),
        grid=(B, T // tt),
        in_specs=specs,
        out_specs=out_specs,
        out_shape=out_shape,
        compiler_params=_params("arbitrary", "arbitrary"),
        name="res_norm_mod",
    )(*args)
    if emit_x:
        return outs[0], outs[1]
    return None, outs[0]


def _mm_kernel(*refs, act, has_res):
    if has_res:
        a_ref, b_ref, r_ref, m_ref, o_ref = refs
    else:
        a_ref, b_ref, o_ref = refs
    acc = jnp.dot(a_ref[...], b_ref[...], preferred_element_type=F32)
    if act == "sigmoid":
        acc = jax.nn.sigmoid(acc)
    if has_res:
        acc = r_ref[...] + m_ref[0] * acc
    o_ref[...] = acc.astype(o_ref.dtype)


def _matmul(a, b, *, out_dtype, act=None, res=None, mod=None, rows_per_batch=None, tm=1024, tn=512):
    M, K = a.shape
    N = b.shape[1]
    tm = _tile(rows_per_batch if rows_per_batch else M, tm)
    tn = _tile(N, tn, LANE)
    has_res = res is not None
    args = [a, b]
    specs = [pl.BlockSpec((tm, K), lambda i, j: (i, 0)), pl.BlockSpec((K, tn), lambda i, j: (0, j))]
    if has_res:
        args += [res, mod.reshape(mod.shape[0], 1, N)]
        specs += [
            pl.BlockSpec((tm, tn), lambda i, j: (i, j)),
            pl.BlockSpec((1, 1, tn), lambda i, j: ((i * tm) // rows_per_batch, 0, j)),
        ]
    return pl.pallas_call(
        functools.partial(_mm_kernel, act=act, has_res=has_res),
        grid=(M // tm, N // tn),
        in_specs=specs,
        out_specs=pl.BlockSpec((tm, tn), lambda i, j: (i, j)),
        out_shape=jax.ShapeDtypeStruct((M, N), out_dtype),
        compiler_params=_params("arbitrary", "arbitrary"),
        name="matmul",
    )(*args)


def _merge_kernel(y0, y1, y2, y3, g0, g1, g2, g3, w_ref, o_ref):
    acc = None
    for i, (y, g) in enumerate(((y0, g0), (y1, g1), (y2, g2), (y3, g3))):
        p = g[...].astype(F32) * jnp.dot(y[...], w_ref[i], preferred_element_type=F32)
        acc = p if acc is None else acc + p
    o_ref[...] = acc.astype(o_ref.dtype)


def _merge(ys, gates, w_branch):
    M = ys[0].shape[0]
    D = w_branch.shape[2]
    tm = _tile(M, 1024)
    tn = _tile(D, 512, LANE)
    nj = D // tn
    y_spec = pl.BlockSpec((tm, BRANCH_W), lambda i, j: (i, 0))
    g_specs = [pl.BlockSpec((tm, tn), functools.partial(lambda i, j, br: (i, br * nj + j), br=br))
               for br in range(N_BRANCH)]
    return pl.pallas_call(
        _merge_kernel,
        grid=(M // tm, nj),
        in_specs=[y_spec] * N_BRANCH + g_specs + [pl.BlockSpec((N_BRANCH, BRANCH_W, tn), lambda i, j: (0, 0, j))],
        out_specs=pl.BlockSpec((tm, tn), lambda i, j: (i, j)),
        out_shape=jax.ShapeDtypeStruct((M, D), BF16),
        compiler_params=_params("arbitrary", "arbitrary"),
        name="merge",
    )(*ys, gates, gates, gates, gates, w_branch)


def _topk_rows(s, k):
    n = s.shape[0]
    iota = lax.broadcasted_iota(jnp.int32, s.shape, 0).astype(F32)
    vals, idxs = [], []
    for _ in range(k):
        m = jnp.max(s, axis=0, keepdims=True)
        am = jnp.min(jnp.where(s == m, iota, float(n)), axis=0, keepdims=True)
        vals.append(m)
        idxs.append(am)
        s = jnp.where(iota == am, -jnp.inf, s)
    return jnp.concatenate(vals, axis=0), jnp.concatenate(idxs, axis=0)


def _candidate_rows(x1, x2):
    K, m = x1.shape
    r1, r2, ok = [], [], []
    a = 0
    while K // (a + 1) > 1:
        nb = K // (a + 1)
        width = -(-nb // SUBLANE) * SUBLANE
        r1.append(jnp.broadcast_to(x1[a:a + 1], (width, m)))
        r2.append(x2[0:width])
        ok.append(lax.broadcasted_iota(jnp.int32, (width, m), 0) < nb)
        a += 1
    assert (K - a) % SUBLANE == 0
    r1.append(x1[a:K])
    r2.append(jnp.broadcast_to(x2[0:1], (K - a, m)))
    ok.append(jnp.full((K - a, m), True))
    return jnp.concatenate(r1, axis=0), jnp.concatenate(r2, axis=0), jnp.concatenate(ok, axis=0)


def _peer_route_kernel(q_ref, k_ref, g_ref, e1_ref, e2_ref):
    half = PEER_DQ // 2
    K = PEER_TOPK
    for c0 in range(0, q_ref.shape[0], LANE):
        cols = slice(c0, c0 + LANE)
        q = q_ref[cols, :]
        v1, i1 = _topk_rows(lax.dot_general(k_ref[0, 0], q[:, :half], NT_DIMS, preferred_element_type=F32,
                                            precision=HIGHEST), K)
        v2, i2 = _topk_rows(lax.dot_general(k_ref[0, 1], q[:, half:], NT_DIMS, preferred_element_type=F32,
                                            precision=HIGHEST), K)
        c1, c2, ok = _candidate_rows(v1, v2)
        top, pos = _topk_rows(jnp.where(ok, c1 + c2, -jnp.inf), K)
        p = jnp.exp(top - top[0:1])
        g_ref[:, cols] = p / jnp.sum(p, axis=0, keepdims=True)
        id1, id2, _ = _candidate_rows(i1, i2)
        row = lax.broadcasted_iota(jnp.int32, id1.shape, 0).astype(F32)
        e1, e2 = [], []
        for r in range(K):
            sel = row == pos[r:r + 1]
            e1.append(jnp.sum(jnp.where(sel, id1, 0.0), axis=0, keepdims=True))
            e2.append(jnp.sum(jnp.where(sel, id2, 0.0), axis=0, keepdims=True))
        e1_ref[:, cols] = jnp.concatenate(e1, axis=0)
        e2_ref[:, cols] = jnp.concatenate(e2, axis=0)


def _peer_route(q, keys):
    M = q.shape[0]
    H, _, nk, half = keys.shape
    tm = _tile(M, 256, LANE)
    out = jax.ShapeDtypeStruct((H * PEER_TOPK, M), F32)
    o_spec = pl.BlockSpec((PEER_TOPK, tm), lambda i, h: (h, i))
    return pl.pallas_call(
        _peer_route_kernel,
        grid=(M // tm, H),
        in_specs=[
            pl.BlockSpec((tm, PEER_DQ), lambda i, h: (i, h)),
            pl.BlockSpec((1, 2, nk, half), lambda i, h: (h, 0, 0, 0)),
        ],
        out_specs=[o_spec, o_spec, o_spec],
        out_shape=[out, out, out],
        compiler_params=_params("arbitrary", "arbitrary"),
        name="peer_route",
    )(q, keys)


def _peer_w_kernel(g_ref, e1_ref, e2_ref, o_ref, gt_ref, e1t_ref, e2t_ref, w_scr, *, nk):
    tm = o_ref.shape[0]
    gt_ref[...] = g_ref[...].T
    e1t_ref[...] = e1_ref[...].T
    e2t_ref[...] = e2_ref[...].T
    key_iota = lax.broadcasted_iota(jnp.int32, (nk, g_ref.shape[0]), 0).astype(F32)

    def body(t, carry):
        row = pl.ds(t, 1)
        a = jnp.where(e1t_ref[row, :] == key_iota, gt_ref[row, :], 0.0).astype(BF16)
        b = jnp.where(e2t_ref[row, :] == key_iota, 1.0, 0.0).astype(BF16)
        w_scr[t] = lax.dot_general(a, b, NT_DIMS, preferred_element_type=F32)
        return carry

    lax.fori_loop(0, tm, body, 0, unroll=8)
    tb = _tile(tm, 64)
    for t0 in range(0, tm, tb):
        planes = jnp.swapaxes(w_scr[t0:t0 + tb], 0, 1)
        for j in range(nk):
            o_ref[t0:t0 + tb, j * nk:(j + 1) * nk] = planes[j].astype(o_ref.dtype)


def _peer_w(g, e1, e2, nk):
    S, M = g.shape
    tm = _tile(M, 256, LANE)
    spec = pl.BlockSpec((S, tm), lambda i: (0, i))
    return pl.pallas_call(
        functools.partial(_peer_w_kernel, nk=nk),
        grid=(M // tm,),
        in_specs=[spec, spec, spec],
        out_specs=pl.BlockSpec((tm, nk * nk), lambda i: (i, 0)),
        out_shape=jax.ShapeDtypeStruct((M, nk * nk), BF16),
        scratch_shapes=[pltpu.VMEM((tm, S), F32)] * 3 + [pltpu.VMEM((tm, nk, nk), F32)],
        compiler_params=_params("arbitrary"),
        name="peer_route_weights",
    )(g, e1, e2)


def _gelu_tanh(x):
    return 0.5 * x * (1.0 + jnp.tanh(0.7978845608028654 * (x + 0.044715 * (x * x * x))))


def _peer_dense_kernel(h_ref, u_ref, v_ref, w_ref, o_ref):
    @pl.when(pl.program_id(1) == 0)
    def _():
        o_ref[...] = jnp.zeros_like(o_ref)

    s = lax.dot_general(h_ref[...], u_ref[...], NT_DIMS, preferred_element_type=F32)
    a = (_gelu_tanh(s) * w_ref[...].astype(F32)).astype(BF16)
    o_ref[...] += jnp.dot(a, v_ref[...], preferred_element_type=F32)


def _peer_dense(h2, u, v, w, layer):
    M, D = h2.shape
    E = u.shape[1]
    tm = _tile(M, 512)
    te = _tile(E, 512, LANE)
    return pl.pallas_call(
        _peer_dense_kernel,
        grid=(M // tm, E // te),
        in_specs=[
            pl.BlockSpec((tm, D), lambda i, e: (i, 0)),
            pl.BlockSpec((None, te, D), lambda i, e: (layer, e, 0)),
            pl.BlockSpec((None, te, D), lambda i, e: (layer, e, 0)),
            pl.BlockSpec((tm, te), lambda i, e: (i, e)),
        ],
        out_specs=pl.BlockSpec((tm, D), lambda i, e: (i, 0)),
        out_shape=jax.ShapeDtypeStruct((M, D), F32),
        compiler_params=_params("arbitrary", "arbitrary"),
        name="peer_dense",
    )(h2, u, v, w)


def _peer(h2, wq, keys, u, v, layer):
    nk = keys.shape[2]
    q = _matmul(h2, wq, out_dtype=F32)
    g, e1, e2 = _peer_route(q, keys)
    return _peer_dense(h2, u, v, _peer_w(g, e1, e2, nk), layer)


def _attn_kernel(*refs, scale, two):
    if two:
        q_ref, k1_ref, v1_ref, k2_ref, v2_ref, o_ref = refs
    else:
        q_ref, k1_ref, v1_ref, o_ref = refs
    q = _bf(q_ref[0].astype(F32) * scale)
    s1 = lax.dot_general(q, _bf(k1_ref[0]), NT_DIMS, preferred_element_type=F32)
    m = jnp.max(s1, axis=-1, keepdims=True)
    if two:
        s2 = lax.dot_general(q, _bf(k2_ref[0]), NT_DIMS, preferred_element_type=F32)
        m = jnp.maximum(m, jnp.max(s2, axis=-1, keepdims=True))
    p1 = jnp.exp(s1 - m)
    l = jnp.sum(p1, axis=-1, keepdims=True)
    o = jnp.dot(_bf(p1), _bf(v1_ref[0]), preferred_element_type=F32)
    if two:
        p2 = jnp.exp(s2 - m)
        l = l + jnp.sum(p2, axis=-1, keepdims=True)
        o = o + jnp.dot(_bf(p2), _bf(v2_ref[0]), preferred_element_type=F32)
    o_ref[0] = (o / l).astype(o_ref.dtype)


def _attention(q, qcol, k1, k1col, v1, v1col, k2=None, k2col=0, v2=None, v2col=0, *, heads, dq, dv, scale):
    B, Tq, _ = q.shape
    tq = _tile(Tq, 256)
    two = k2 is not None

    def spec(arr, col, w, tiled):
        n = arr.shape[1]
        if tiled:
            return pl.BlockSpec((1, tq, w), lambda b, h, t: (b, t, col // w + h))
        return pl.BlockSpec((1, n, w), lambda b, h, t: (b, 0, col // w + h))

    args = [q, k1, v1]
    specs = [spec(q, qcol, dq, True), spec(k1, k1col, dq, False), spec(v1, v1col, dv, False)]
    if two:
        args += [k2, v2]
        specs += [spec(k2, k2col, dq, False), spec(v2, v2col, dv, False)]
    return pl.pallas_call(
        functools.partial(_attn_kernel, scale=scale, two=two),
        grid=(B, heads, Tq // tq),
        in_specs=specs,
        out_specs=pl.BlockSpec((1, tq, dv), lambda b, h, t: (b, t, h)),
        out_shape=jax.ShapeDtypeStruct((B, Tq, heads * dv), BF16),
        compiler_params=_params("arbitrary", "arbitrary", "arbitrary"),
        name="attention",
    )(*args)


def _mla_prep_kernel(z_ref, zr_ref, c_ref, s_ref, gq_ref, gkv_ref, wq_ref, wkv_ref, q_out, k_out, v_out):
    z = z_ref[0]

    def rms(x, g):
        return x * lax.rsqrt(jnp.mean(x * x, axis=-1, keepdims=True) + EPS) * g

    q = jnp.dot(_bf(rms(z[:, :MLA_Q_RANK], gq_ref[...])), wq_ref[...], preferred_element_type=F32)
    kv = jnp.dot(_bf(rms(z[:, MLA_Q_RANK:], gkv_ref[...])), wkv_ref[...], preferred_element_type=F32)
    cos, sin = c_ref[...], s_ref[...]

    def rope(x):
        return x * cos + pltpu.roll(x, MLA_ROPE, 1) * sin

    kr = rope(zr_ref[0]).astype(k_out.dtype)
    hv = MLA_HEADS * MLA_NOPE
    for h in range(MLA_HEADS):
        lo = h * MLA_DQP
        q_out[0, :, lo:lo + MLA_NOPE] = q[:, lo:lo + MLA_NOPE].astype(q_out.dtype)
        q_out[0, :, lo + MLA_NOPE:lo + MLA_DQP] = rope(q[:, lo + MLA_NOPE:lo + MLA_DQP]).astype(q_out.dtype)
        k_out[0, :, lo:lo + MLA_NOPE] = kv[:, h * MLA_NOPE:(h + 1) * MLA_NOPE].astype(k_out.dtype)
        k_out[0, :, lo + MLA_NOPE:lo + MLA_DQP] = kr
    v_out[0] = kv[:, hv:].astype(v_out.dtype)


def _mla_prep(z, cos_t, sin_t, gq, gkv, wq_p, wkv_p):
    B, T, _ = z.shape
    tt = _tile(T, 256)
    wq_w = MLA_HEADS * MLA_DQP
    wkv_w = MLA_HEADS * (MLA_NOPE + MLA_DV)
    cw = MLA_Q_RANK + MLA_KV_RANK
    tab = pl.BlockSpec((tt, LANE), lambda b, t: (t, 0))
    return pl.pallas_call(
        _mla_prep_kernel,
        grid=(B, T // tt),
        in_specs=[
            pl.BlockSpec((1, tt, cw), lambda b, t: (b, t, COL_MQKV // cw)),
            pl.BlockSpec((1, tt, LANE), lambda b, t: (b, t, COL_KR // LANE)),
            tab, tab,
            pl.BlockSpec((1, MLA_Q_RANK), lambda b, t: (0, 0)),
            pl.BlockSpec((1, MLA_KV_RANK), lambda b, t: (0, 0)),
            pl.BlockSpec((MLA_Q_RANK, wq_w), lambda b, t: (0, 0)),
            pl.BlockSpec((MLA_KV_RANK, wkv_w), lambda b, t: (0, 0)),
        ],
        out_specs=[
            pl.BlockSpec((1, tt, wq_w), lambda b, t: (b, t, 0)),
            pl.BlockSpec((1, tt, wq_w), lambda b, t: (b, t, 0)),
            pl.BlockSpec((1, tt, MLA_HEADS * MLA_DV), lambda b, t: (b, t, 0)),
        ],
        out_shape=[
            jax.ShapeDtypeStruct((B, T, wq_w), BF16),
            jax.ShapeDtypeStruct((B, T, wq_w), BF16),
            jax.ShapeDtypeStruct((B, T, MLA_HEADS * MLA_DV), BF16),
        ],
        compiler_params=_params("arbitrary", "arbitrary"),
        name="mla_prep",
    )(z, z, cos_t, sin_t, gq.reshape(1, -1), gkv.reshape(1, -1), wq_p, wkv_p)


def _mla_weights(w_uq, w_ukv):
    half = MLA_ROPE // 2
    wq = w_uq.reshape(MLA_Q_RANK, MLA_HEADS, MLA_NOPE + MLA_ROPE)
    pe = wq[..., MLA_NOPE:]
    pe_sw = jnp.concatenate([pe[..., half:], pe[..., :half]], axis=-1)
    wq_p = jnp.concatenate([wq[..., :MLA_NOPE], pe, pe_sw], axis=-1).reshape(MLA_Q_RANK, MLA_HEADS * MLA_DQP)
    wkv = w_ukv.reshape(MLA_KV_RANK, MLA_HEADS, MLA_NOPE + MLA_DV)
    wkv_p = jnp.concatenate([wkv[..., :MLA_NOPE].reshape(MLA_KV_RANK, -1),
                             wkv[..., MLA_NOPE:].reshape(MLA_KV_RANK, -1)], axis=-1)
    return _bf(wq_p), _bf(wkv_p)


def _rope_tables(n_tok, with_pos):
    n_freq = MLA_ROPE // 4
    zeros = jnp.zeros((n_tok, LANE - MLA_ROPE), F32)
    if not with_pos:
        return (jnp.concatenate([jnp.ones((n_tok, MLA_ROPE), F32), zeros], axis=1),
                jnp.zeros((n_tok, LANE), F32))
    t = jnp.arange(n_tok, dtype=jnp.int32)
    row = (t // GRID_W).astype(F32)
    col = (t % GRID_W).astype(F32)
    inv = ROPE_THETA ** (-jnp.arange(n_freq, dtype=F32) / n_freq)
    ang = jnp.concatenate([row[:, None] * inv, col[:, None] * inv], axis=-1)
    cos, sin = jnp.cos(ang), jnp.sin(ang)
    return (jnp.concatenate([cos, cos, zeros], axis=1), jnp.concatenate([-sin, sin, zeros], axis=1))


def _na_kernel(q_ref, k_ref, v_ref, kc_ref, vc_ref, bm_ref, o_ref, *, scale, rows):
    nblk = rows // NA_QROWS
    nq = NA_QROWS * GRID_W
    nkw = NA_KROWS * GRID_W
    kc = _bf(kc_ref[0])
    vc = _bf(vc_ref[0])

    def body(blk, carry):
        q0 = pl.multiple_of(blk * nq, nq)
        kb = jnp.clip(NA_QROWS * blk - NA_WIN_R // 2, 0, rows - NA_KROWS)
        k0 = pl.multiple_of(kb * GRID_W, (NA_WIN_R // 2) * GRID_W)
        pat = jnp.where(blk == 0, 0, jnp.where(blk == nblk - 1, 2, 1))
        q = _bf(q_ref[0, pl.ds(q0, nq), :] * scale)
        kw = _bf(k_ref[0, pl.ds(k0, nkw), :])
        vw = _bf(v_ref[0, pl.ds(k0, nkw), :])
        bm = bm_ref[0, pat]
        s = lax.dot_general(q, kw, NT_DIMS, preferred_element_type=F32)
        s = jnp.where(bm > 0.5 * NEG_INF, s + bm, NEG_INF)
        sc = lax.dot_general(q, kc, NT_DIMS, preferred_element_type=F32)
        m = jnp.maximum(jnp.max(s, axis=-1, keepdims=True), jnp.max(sc, axis=-1, keepdims=True))
        p = jnp.exp(s - m)
        pc = jnp.exp(sc - m)
        l = jnp.sum(p, axis=-1, keepdims=True) + jnp.sum(pc, axis=-1, keepdims=True)
        o = jnp.dot(_bf(p), vw, preferred_element_type=F32) + jnp.dot(_bf(pc), vc, preferred_element_type=F32)
        o_ref[0, pl.ds(q0, nq), :] = (o / l).astype(o_ref.dtype)
        return carry

    lax.fori_loop(0, nblk, body, 0)


def _na_bias_table(rpb, rows):
    nblk = rows // NA_QROWS
    col = np.arange(GRID_W)
    c_start = np.clip(col - NA_WIN_C // 2, 0, GRID_W - NA_WIN_C)
    in_win = (col[None, :] >= c_start[:, None]) & (col[None, :] < c_start[:, None] + NA_WIN_C)
    dc = np.clip(col[None, :] - col[:, None] + NA_WIN_C - 1, 0, 2 * NA_WIN_C - 2)
    dc_onehot = (dc[:, :, None] == np.arange(2 * NA_WIN_C - 1)).astype(np.float32)
    toeplitz = jnp.einsum("hab,qkb->haqk", rpb.astype(F32), dc_onehot, precision=HIGHEST)
    n_dr = 2 * NA_WIN_R - 1
    sel = np.zeros((3, NA_QROWS, NA_KROWS, n_dr), np.float32)
    valid = np.zeros((3, NA_QROWS, NA_KROWS), bool)
    for p, blk in enumerate((0, min(1, nblk - 1), nblk - 1)):
        kb = int(np.clip(NA_QROWS * blk - NA_WIN_R // 2, 0, rows - NA_KROWS))
        for rq in range(NA_QROWS):
            r = NA_QROWS * blk + rq
            r_start = int(np.clip(r - NA_WIN_R // 2, 0, rows - NA_WIN_R))
            for rk in range(NA_KROWS):
                kr = kb + rk
                if r_start <= kr < r_start + NA_WIN_R:
                    sel[p, rq, rk, kr - r + NA_WIN_R - 1] = 1.0
                    valid[p, rq, rk] = True
    bias = jnp.einsum("prka,haqc->hprqkc", sel, toeplitz, precision=HIGHEST)
    mask = valid[None, :, :, None, :, None] & in_win[None, None, None, :, None, :]
    H = rpb.shape[0]
    return jnp.where(mask, bias, NEG_INF).reshape(H, 3, NA_QROWS * GRID_W, NA_KROWS * GRID_W)


def _na(z, zc, rpb):
    B, T, _ = z.shape
    Tc = zc.shape[1]
    rows = T // GRID_W
    assert rows % NA_QROWS == 0 and rows >= NA_KROWS
    bm = _na_bias_table(rpb, rows)
    dh = NA_DH

    def seq(n, col):
        return pl.BlockSpec((1, n, dh), lambda b, h: (b, 0, col // dh + h))

    return pl.pallas_call(
        functools.partial(_na_kernel, scale=dh ** -0.5, rows=rows),
        grid=(B, NA_HEADS),
        in_specs=[seq(T, COL_NQ), seq(T, COL_NK), seq(T, COL_NV), seq(Tc, COL_NK), seq(Tc, COL_NV),
                  pl.BlockSpec((1,) + bm.shape[1:], lambda b, h: (h, 0, 0, 0))],
        out_specs=pl.BlockSpec((1, T, dh), lambda b, h: (b, 0, h)),
        out_shape=jax.ShapeDtypeStruct((B, T, NA_HEADS * dh), BF16),
        compiler_params=_params("arbitrary", "arbitrary"),
        name="neighbourhood_attention",
    )(z, z, z, zc, zc, bm)


def _lru_kernel(g_ref, x_ref, gc_ref, xc_ref, cw_ref, cb_ref, wa_ref, ba_ref, wx_ref, bx_ref, lam_ref,
                y_ref, yc_ref, a_scr, u_scr, h_scr):
    row8 = lax.broadcasted_iota(jnp.int32, (SUBLANE, LANE), 0)

    def scan_pair(n_tok, h0f, h0b):
        nb = n_tok // SUBLANE

        def body(i, carry):
            hf, hb = carry
            rf = pl.multiple_of(i * SUBLANE, SUBLANE)
            rb = pl.multiple_of((nb - 1 - i) * SUBLANE, SUBLANE)
            A, U = a_scr[0, pl.ds(rf, SUBLANE), :], u_scr[0, pl.ds(rf, SUBLANE), :]
            Ab, Ub = a_scr[1, pl.ds(rb, SUBLANE), :], u_scr[1, pl.ds(rb, SUBLANE), :]
            for s in (1, 2, 4):
                m = row8 >= s
                U = jnp.where(m, A * pltpu.roll(U, s, 0) + U, U)
                A = jnp.where(m, A * pltpu.roll(A, s, 0), A)
                mb = row8 < SUBLANE - s
                Ub = jnp.where(mb, Ab * pltpu.roll(Ub, SUBLANE - s, 0) + Ub, Ub)
                Ab = jnp.where(mb, Ab * pltpu.roll(Ab, SUBLANE - s, 0), Ab)
            hbf = A * hf + U
            hbb = Ab * hb + Ub
            h_scr[0, pl.ds(rf, SUBLANE), :] = hbf
            h_scr[1, pl.ds(rb, SUBLANE), :] = hbb
            return hbf[SUBLANE - 1:SUBLANE, :], hbb[0:1, :]

        return lax.fori_loop(0, nb, body, (h0f, h0b))

    def run(gate_ref, zx_ref, out_ref, n_tok, h0f, h0b):
        x = zx_ref[0]
        t = lax.broadcasted_iota(jnp.int32, x.shape, 0)
        w = cw_ref[...]
        xc = (w[0:1] * jnp.where(t >= 2, pltpu.roll(x, 2, 0), 0.0)
              + w[1:2] * jnp.where(t >= 1, pltpu.roll(x, 1, 0), 0.0)
              + w[2:3] * x
              + w[3:4] * jnp.where(t < n_tok - 1, pltpu.roll(x, n_tok - 1, 0), 0.0)) + cb_ref[...]
        xcb = _bf(xc)
        for d in range(2):
            r = jax.nn.sigmoid(jnp.dot(xcb, wa_ref[d, 0], preferred_element_type=F32) + ba_ref[d:d + 1, :])
            i = jax.nn.sigmoid(jnp.dot(xcb, wx_ref[d, 0], preferred_element_type=F32) + bx_ref[d:d + 1, :])
            nl = -lam_ref[d:d + 1, :]
            softplus = jnp.maximum(nl, 0.0) + jnp.log1p(jnp.exp(-jnp.abs(nl)))
            log_a = -LRU_C * r * softplus
            a = jnp.exp(log_a)
            a_scr[d, 0:n_tok, :] = a
            u_scr[d, 0:n_tok, :] = jnp.sqrt(1.0 - jnp.exp(2.0 * log_a)) * i * xc
        hf, hb = scan_pair(n_tok, h0f, h0b)
        h = h_scr[0, 0:n_tok, :] + h_scr[1, 0:n_tok, :]
        out_ref[0] = (_gelu_tanh(gate_ref[0]) * h).astype(out_ref.dtype)
        return hf, hb

    zero = jnp.zeros((1, LANE), F32)
    hf, hb = run(gc_ref, xc_ref, yc_ref, xc_ref.shape[1], zero, zero)
    run(g_ref, x_ref, y_ref, x_ref.shape[1], hf, hb)


def _lru(z, zc, conv_w, conv_b, wa, ba, wx, bx, lam):
    B, T, _ = z.shape
    Tc = zc.shape[1]
    bw = LRU_BW

    def seq(n, col):
        return pl.BlockSpec((1, n, bw), lambda b, j: (b, 0, col // bw + j))

    vec2 = pl.BlockSpec((2, bw), lambda b, j: (0, j))
    mat = pl.BlockSpec((2, 1, bw, bw), lambda b, j: (0, j, 0, 0))
    out = lambda n: pl.BlockSpec((1, n, bw), lambda b, j: (b, 0, j))
    return pl.pallas_call(
        _lru_kernel,
        grid=(B, LRU_BLOCKS),
        in_specs=[seq(T, COL_LG), seq(T, COL_LX), seq(Tc, COL_LG), seq(Tc, COL_LX),
                  pl.BlockSpec((CONV_W, bw), lambda b, j: (0, j)), pl.BlockSpec((1, bw), lambda b, j: (0, j)),
                  mat, vec2, mat, vec2, vec2],
        out_specs=[out(T), out(Tc)],
        out_shape=[jax.ShapeDtypeStruct((B, T, LRU_WIDTH), BF16), jax.ShapeDtypeStruct((B, Tc, LRU_WIDTH), BF16)],
        scratch_shapes=[pltpu.VMEM((2, T, bw), F32)] * 3,
        compiler_params=_params("arbitrary", "arbitrary"),
        name="rg_lru",
    )(z, z, zc, zc, conv_w, conv_b.reshape(1, -1), _bf(wa), ba, _bf(wx), bx, lam)


def _gla_kernel(q_ref, k_ref, v_ref, g_ref, d_ref, qc_ref, kc_ref, vc_ref, gc_ref, dc_ref,
                wa_ref, ba_ref, ng_ref, y_ref, yc_ref, la_scr, o_scr):
    C = GLA_CHUNK
    ri = lax.broadcasted_iota(jnp.int32, (C, C), 0)
    ci = lax.broadcasted_iota(jnp.int32, (C, C), 1)
    keep = (ri >= ci, ri <= ci)
    tri = (keep[0].astype(F32), keep[1].astype(F32))
    scale = GLA_DK ** -0.5

    def chunk(refs, c, d, st):
        qr, kr, vr = refs
        r0 = pl.multiple_of(c * C, C)
        rows = pl.ds(r0, C)
        b = jnp.dot(tri[d], la_scr[d, rows, :], preferred_element_type=F32, precision=HIGHEST)
        q = qr[0, rows, :] * scale
        k = kr[0, rows, :]
        v = _bf(vr[0, rows, :])
        qd = _bf(q * jnp.exp(b))
        att = lax.dot_general(qd, _bf(k * jnp.exp(-b)), NT_DIMS, preferred_element_type=F32)
        att = jnp.where(keep[d], att, 0.0)
        o = (jnp.dot(_bf(att), v, preferred_element_type=F32)
             + lax.dot_general(qd, _bf(st), NT_DIMS, preferred_element_type=F32))
        b_tot = b[C - 1:C, :] if d == 0 else b[0:1, :]
        ke = _bf(k * jnp.exp(b_tot - b))
        st = st * jnp.exp(b_tot) + lax.dot_general(v, ke, TN_DIMS, preferred_element_type=F32)
        o_scr[d, rows, :] = o
        return st

    def run(refs, gate_ref, dec_ref, out_ref, n_tok, st_f, st_b):
        dec = dec_ref[0]
        for d in range(2):
            x = jnp.dot(dec, wa_ref[d], preferred_element_type=F32, precision=HIGHEST) + ba_ref[d:d + 1, :]
            log_sig = jnp.minimum(x, 0.0) - jnp.log1p(jnp.exp(-jnp.abs(x)))
            la_scr[d, 0:n_tok, :] = log_sig * (1.0 / GLA_TAU)
        n = n_tok // C

        def body(i, carry):
            sf, sb = carry
            return chunk(refs, i, 0, sf), chunk(refs, n - 1 - i, 1, sb)

        st_f, st_b = lax.fori_loop(0, n, body, (st_f, st_b), unroll=2)
        o = o_scr[0, 0:n_tok, :] + o_scr[1, 0:n_tok, :]
        o = o * lax.rsqrt(jnp.mean(o * o, axis=-1, keepdims=True) + EPS) * ng_ref[...]
        gate = gate_ref[0]
        out_ref[0] = (o * (gate * jax.nn.sigmoid(gate))).astype(out_ref.dtype)
        return st_f, st_b

    zero = jnp.zeros((GLA_DV, GLA_DK), F32)
    st_f, st_b = run((qc_ref, kc_ref, vc_ref), gc_ref, dc_ref, yc_ref, qc_ref.shape[1], zero, zero)
    run((q_ref, k_ref, v_ref), g_ref, d_ref, y_ref, q_ref.shape[1], st_f, st_b)


def _gla(z, zc, wa2, ba, norm_g):
    B, T, _ = z.shape
    Tc = zc.shape[1]
    wa_p = jnp.zeros((2, LANE, GLA_HEADS * GLA_DK), F32)
    wa_p = wa_p.at[0, :GLA_RANK].set(wa2[0]).at[1, GLA_RANK:2 * GLA_RANK].set(wa2[1])

    def seq(n, col, w):
        return pl.BlockSpec((1, n, w), lambda b, h: (b, 0, col // w + h), pipeline_mode=pl.Buffered(1))

    def dec(n):
        return pl.BlockSpec((1, n, LANE), lambda b, h: (b, 0, COL_DEC // LANE), pipeline_mode=pl.Buffered(1))

    def ins(n):
        return [seq(n, COL_GQ, GLA_DK), seq(n, COL_GK, GLA_DK), seq(n, COL_GV, GLA_DV), seq(n, COL_GG, GLA_DV), dec(n)]

    out = lambda n: pl.BlockSpec((1, n, GLA_DV), lambda b, h: (b, 0, h))
    return pl.pallas_call(
        _gla_kernel,
        grid=(B, GLA_HEADS),
        in_specs=ins(T) + ins(Tc) + [
            pl.BlockSpec((2, LANE, GLA_DK), lambda b, h: (0, 0, h)),
            pl.BlockSpec((2, GLA_DK), lambda b, h: (0, h)),
            pl.BlockSpec((1, GLA_DV), lambda b, h: (0, 0)),
        ],
        out_specs=[out(T), out(Tc)],
        out_shape=[jax.ShapeDtypeStruct((B, T, BRANCH_W), BF16), jax.ShapeDtypeStruct((B, Tc, BRANCH_W), BF16)],
        scratch_shapes=[pltpu.VMEM((2, T, GLA_DK), F32), pltpu.VMEM((2, T, GLA_DV), F32)],
        compiler_params=_params("arbitrary", "arbitrary"),
        name="gla",
    )(z, z, z, z, z, zc, zc, zc, zc, zc, wa_p, ba, norm_g.reshape(1, -1))


def _permute_mix_weight(w):
    o = np.cumsum((0,) + MIX_SIZES)
    gq, gk, gv, gg, af, ab, cq, ckv, kr, lg, lx, nq, nk, nv = [w[:, o[i]:o[i + 1]] for i in range(len(MIX_SIZES))]
    half = MLA_ROPE // 2
    kr_sw = jnp.concatenate([kr[:, half:], kr[:, :half]], axis=1)
    pad = jnp.zeros((w.shape[0], MIXP - (COL_DEC + 2 * GLA_RANK)), w.dtype)
    return _bf(jnp.concatenate([gv, gg, lg, lx, nq, nk, nv, gq, gk, cq, ckv, kr, kr_sw, af, ab, pad], axis=1))


def kernel(x, c, ctx, c_ctx, w_ada, b_ada, norm1_g, norm2_g, w_in, gla_wa2, gla_ba, gla_norm_g, mla_q_norm_g, mla_w_uq, mla_kv_norm_g, mla_w_ukv, lru_conv_w, lru_conv_b, lru_wa, lru_ba, lru_wx, lru_bx, lru_lambda, na_rpb, w_branch, w_out, peer_wq, peer_keys, peer_u, peer_v, final_norm_g):
    B, T, D = x.shape
    Tc = ctx.shape[1]
    L = w_ada.shape[0]
    rope_l = _rope_tables(T, True)
    rope_c = _rope_tables(Tc, False)

    n_rows = -(-(B + 1) // SUBLANE) * SUBLANE
    cc = jnp.zeros((n_rows, D), F32).at[:B].set(c).at[B].set(c_ctx)
    mods = _ada(cc, w_ada, b_ada)

    xc = ctx
    pe_l = pe_c = gate_l = gate_c = None
    pu = _bf(peer_u)
    pv = _bf(peer_v)
    for l in range(L):
        update_ctx = l < L - 1
        ml = jnp.split(mods[l, :B], 6, axis=-1)
        mc = jnp.split(jnp.broadcast_to(mods[l, B], (B, 6 * D)), 6, axis=-1)
        w_mix = _permute_mix_weight(w_in[l, :, :MIX_COLS])
        w_gate = _bf(w_in[l, :, MIX_COLS:])
        wb = _bf(w_branch[l])
        wo = _bf(w_out[l])
        wq = _bf(peer_wq[l])
        mla_wq, mla_wkv = _mla_weights(mla_w_uq[l], mla_w_ukv[l])

        xn, h = _norm(x, norm1_g[l], delta=pe_l, gate=gate_l, shift=ml[0], scale=ml[1])
        x = x if xn is None else xn
        xcn, hc = _norm(xc, norm1_g[l], delta=pe_c, gate=gate_c, shift=mc[0], scale=mc[1])
        xc = xc if xcn is None else xcn

        h2d = h.reshape(B * T, D)
        hc2d = hc.reshape(B * Tc, D)
        z = _matmul(h2d, w_mix, out_dtype=F32).reshape(B, T, MIXP)
        zc = _matmul(hc2d, w_mix, out_dtype=F32).reshape(B, Tc, MIXP)

        y_gla, yc_gla = _gla(z, zc, gla_wa2[l], gla_ba[l], gla_norm_g[l])
        y_lru, yc_lru = _lru(z, zc, lru_conv_w[l], lru_conv_b[l], lru_wa[l], lru_ba[l], lru_wx[l], lru_bx[l],
                             lru_lambda[l])
        ql, kl, vl = _mla_prep(z, *rope_l, mla_q_norm_g[l], mla_kv_norm_g[l], mla_wq, mla_wkv)
        qc, kc, vc = _mla_prep(zc, *rope_c, mla_q_norm_g[l], mla_kv_norm_g[l], mla_wq, mla_wkv)
        mla_args = dict(heads=MLA_HEADS, dq=MLA_DQP, dv=MLA_DV, scale=(MLA_NOPE + MLA_ROPE) ** -0.5)
        y_mla = _attention(ql, 0, kc, 0, vc, 0, kl, 0, vl, 0, **mla_args)
        y_na = _na(z, zc, na_rpb[l])
        ys_l = [y_gla, y_mla, y_lru, y_na]
        if update_ctx:
            yc_mla = _attention(qc, 0, kc, 0, vc, 0, **mla_args)
            yc_na = _attention(zc, COL_NQ, zc, COL_NK, zc, COL_NV, heads=NA_HEADS, dq=NA_DH, dv=NA_DH,
                               scale=NA_DH ** -0.5)
            ys_c = [yc_gla, yc_mla, yc_lru, yc_na]

        def channel_mix(xs, hs, ys_s, m, n_tok):
            M = B * n_tok
            gates = _matmul(hs, w_gate, out_dtype=BF16, act="sigmoid")
            mrg = _merge([y.reshape(M, BRANCH_W) for y in ys_s], gates, wb)
            xs = _matmul(mrg, wo, out_dtype=F32, res=xs.reshape(M, D), mod=m[2],
                         rows_per_batch=n_tok).reshape(B, n_tok, D)
            _, h2 = _norm(xs, norm2_g[l], shift=m[3], scale=m[4])
            pe = _peer(h2.reshape(M, D), wq, peer_keys[l], pu, pv, l).reshape(B, n_tok, D)
            return xs, pe

        x, pe_l = channel_mix(x, h2d, ys_l, ml, T)
        gate_l = ml[5]
        if update_ctx:
            xc, pe_c = channel_mix(xc, hc2d, ys_c, mc, Tc)
            gate_c = mc[5]
        else:
            pe_c = gate_c = None
    _, out = _norm(x, final_norm_g, delta=pe_l, gate=gate_l, out_dtype=F32, emit_x=False)
    return out
```

```python
import functools

import jax
import jax.numpy as jnp
import numpy as np
from jax import lax
from jax.experimental import pallas as pl
from jax.experimental.pallas import tpu as pltpu

GRID_W = 64
EPS = 1e-6
NEG_INF = -1e30
N_BRANCH = 4
BRANCH_W = 1024
ROPE_THETA = 10000.0

GLA_HEADS = 4
GLA_DK = 128
GLA_DV = BRANCH_W // GLA_HEADS
GLA_RANK = 16
GLA_TAU = 16.0
GLA_CHUNK = 64

MLA_HEADS = 8
MLA_Q_RANK = 768
MLA_KV_RANK = 256
MLA_NOPE = 128
MLA_ROPE = 64
MLA_DV = BRANCH_W // MLA_HEADS
MLA_DQP = 256

LRU_WIDTH = BRANCH_W
LRU_BLOCKS = 8
LRU_BW = LRU_WIDTH // LRU_BLOCKS
CONV_W = 4
LRU_C = 8.0

NA_HEADS = 8
NA_DH = BRANCH_W // NA_HEADS
NA_WIN_R = 8
NA_WIN_C = 16
NA_QROWS = 8
NA_KROWS = 16

PEER_HEADS = 8
PEER_DQ = 256
PEER_TOPK = 16

MIX_SIZES = (
    GLA_HEADS * GLA_DK, GLA_HEADS * GLA_DK, GLA_HEADS * GLA_DV, GLA_HEADS * GLA_DV, GLA_RANK, GLA_RANK,
    MLA_Q_RANK, MLA_KV_RANK, MLA_ROPE, LRU_WIDTH, LRU_WIDTH,
    NA_HEADS * NA_DH, NA_HEADS * NA_DH, NA_HEADS * NA_DH,
)
MIX_COLS = sum(MIX_SIZES)

COL_GV, COL_GG, COL_LG, COL_LX, COL_NQ, COL_NK, COL_NV = 0, 1024, 2048, 3072, 4096, 5120, 6144
COL_GQ, COL_GK = 7168, 7680
COL_MQKV = 8192
COL_KR = 9216
COL_DEC = 9344
MIXP = 9728

V7X_VMEM_BYTES = 64 * 1024 * 1024
VMEM_LIMIT = V7X_VMEM_BYTES - 8 * 1024 * 1024
LANE = 128
SUBLANE = 8

F32 = jnp.float32
BF16 = jnp.bfloat16
HIGHEST = lax.Precision.HIGHEST
LOG2_E = 1.4426950408889634
NT_DIMS = (((1,), (1,)), ((), ()))
TN_DIMS = (((0,), (0,)), ((), ()))


def _tile(n, pref, mult=SUBLANE):
    if n <= pref:
        return n
    t = (pref // mult) * mult
    while t > mult and n % t:
        t -= mult
    assert n % t == 0, (n, pref, mult)
    return t


def _params(*sem):
    return pltpu.CompilerParams(dimension_semantics=sem, vmem_limit_bytes=VMEM_LIMIT)


def _bf(x):
    return x.astype(BF16)


def _ada_kernel(c_ref, w_ref, b_ref, o_ref):
    cv = c_ref[...]
    a = cv * jax.nn.sigmoid(cv)
    o_ref[0] = jnp.dot(a, w_ref[0], preferred_element_type=F32, precision=HIGHEST) + b_ref[0]


def _ada(cc, w_ada, b_ada):
    L, D, W = w_ada.shape
    R = cc.shape[0]
    tn = _tile(W, 512, LANE)
    return pl.pallas_call(
        _ada_kernel,
        grid=(L, W // tn),
        in_specs=[
            pl.BlockSpec((R, D), lambda l, j: (0, 0)),
            pl.BlockSpec((1, D, tn), lambda l, j: (l, 0, j)),
            pl.BlockSpec((1, 1, tn), lambda l, j: (l, 0, j)),
        ],
        out_specs=pl.BlockSpec((1, R, tn), lambda l, j: (l, 0, j)),
        out_shape=jax.ShapeDtypeStruct((L, R, W), F32),
        compiler_params=_params("arbitrary", "arbitrary"),
        name="ada_mod",
    )(cc, w_ada, b_ada.reshape(L, 1, W))


def _norm_kernel(*refs, has_delta, modulate, emit_x):
    it = iter(refs)
    x_ref = next(it)
    if has_delta:
        d_ref, gate_ref = next(it), next(it)
    g_ref = next(it)
    if modulate:
        shift_ref, scale_ref = next(it), next(it)
    if emit_x:
        xo_ref = next(it)
    h_ref = next(it)
    x = x_ref[0]
    if has_delta:
        x = x + gate_ref[0] * d_ref[0]
    if emit_x:
        xo_ref[0] = x
    y = x * lax.rsqrt(jnp.mean(x * x, axis=-1, keepdims=True) + EPS)
    y = y * g_ref[...]
    if modulate:
        y = y * (1.0 + scale_ref[0]) + shift_ref[0]
    h_ref[0] = y.astype(h_ref.dtype)


def _norm(x, g, *, delta=None, gate=None, shift=None, scale=None, out_dtype=None, emit_x=True):
    out_dtype = BF16 if out_dtype is None else out_dtype
    B, T, D = x.shape
    tt = _tile(T, 256)
    has_delta = delta is not None
    emit_x = emit_x and has_delta
    modulate = shift is not None
    tok = pl.BlockSpec((1, tt, D), lambda b, t: (b, t, 0))
    vec = pl.BlockSpec((1, 1, D), lambda b, t: (b, 0, 0))
    args, specs = [x], [tok]
    if has_delta:
        args += [delta, gate.reshape(B, 1, D)]
        specs += [tok, vec]
    args.append(g.reshape(1, D))
    specs.append(pl.BlockSpec((1, D), lambda b, t: (0, 0)))
    if modulate:
        args += [shift.reshape(B, 1, D), scale.reshape(B, 1, D)]
        specs += [vec, vec]
    out_shape, out_specs = [], []
    if emit_x:
        out_shape.append(jax.ShapeDtypeStruct((B, T, D), F32))
        out_specs.append(tok)
    out_shape.append(jax.ShapeDtypeStruct((B, T, D), out_dtype))
    out_specs.append(tok)
    outs = pl.pallas_call(
        functools.partial(_norm_kernel, has_delta=has_delta, modulate=modulate, emit_x=emit_x),
        grid=(B, T // tt),
        in_specs=specs,
        out_specs=out_specs,
        out_shape=out_shape,
        compiler_params=_params("arbitrary", "arbitrary"),
        name="res_norm_mod",
    )(*args)
    if emit_x:
        return outs[0], outs[1]
    return None, outs[0]


def _mm_kernel(*refs, act, has_res):
    if has_res:
        a_ref, b_ref, r_ref, m_ref, o_ref = refs
    else:
        a_ref, b_ref, o_ref = refs
    acc = jnp.dot(a_ref[...], b_ref[...], preferred_element_type=F32)
    if act == "sigmoid":
        acc = jax.nn.sigmoid(acc)
    if has_res:
        acc = r_ref[...] + m_ref[0] * acc
    o_ref[...] = acc.astype(o_ref.dtype)


def _matmul(a, b, *, out_dtype, act=None, res=None, mod=None, rows_per_batch=None, tm=1024, tn=512):
    M, K = a.shape
    N = b.shape[1]
    tm = _tile(rows_per_batch if rows_per_batch else M, tm)
    tn = _tile(N, tn, LANE)
    has_res = res is not None
    args = [a, b]
    specs = [pl.BlockSpec((tm, K), lambda i, j: (i, 0)), pl.BlockSpec((K, tn), lambda i, j: (0, j))]
    if has_res:
        args += [res, mod.reshape(mod.shape[0], 1, N)]
        specs += [
            pl.BlockSpec((tm, tn), lambda i, j: (i, j)),
            pl.BlockSpec((1, 1, tn), lambda i, j: ((i * tm) // rows_per_batch, 0, j)),
        ]
    return pl.pallas_call(
        functools.partial(_mm_kernel, act=act, has_res=has_res),
        grid=(M // tm, N // tn),
        in_specs=specs,
        out_specs=pl.BlockSpec((tm, tn), lambda i, j: (i, j)),
        out_shape=jax.ShapeDtypeStruct((M, N), out_dtype),
        compiler_params=_params("arbitrary", "arbitrary"),
        name="matmul",
    )(*args)


def _merge_kernel(y0, y1, y2, y3, g0, g1, g2, g3, w_ref, o_ref):
    acc = None
    for i, (y, g) in enumerate(((y0, g0), (y1, g1), (y2, g2), (y3, g3))):
        p = g[...].astype(F32) * jnp.dot(y[...], w_ref[i], preferred_element_type=F32)
        acc = p if acc is None else acc + p
    o_ref[...] = acc.astype(o_ref.dtype)


def _merge(ys, gates, w_branch):
    M = ys[0].shape[0]
    D = w_branch.shape[2]
    tm = _tile(M, 1024)
    tn = _tile(D, 512, LANE)
    nj = D // tn
    y_spec = pl.BlockSpec((tm, BRANCH_W), lambda i, j: (i, 0))
    g_specs = [pl.BlockSpec((tm, tn), functools.partial(lambda i, j, br: (i, br * nj + j), br=br))
               for br in range(N_BRANCH)]
    return pl.pallas_call(
        _merge_kernel,
        grid=(M // tm, nj),
        in_specs=[y_spec] * N_BRANCH + g_specs + [pl.BlockSpec((N_BRANCH, BRANCH_W, tn), lambda i, j: (0, 0, j))],
        out_specs=pl.BlockSpec((tm, tn), lambda i, j: (i, j)),
        out_shape=jax.ShapeDtypeStruct((M, D), BF16),
        compiler_params=_params("arbitrary", "arbitrary"),
        name="merge",
    )(*ys, gates, gates, gates, gates, w_branch)


def _topk_rows(s, k):
    n = s.shape[0]
    iota = lax.broadcasted_iota(jnp.int32, s.shape, 0).astype(F32)
    vals, idxs = [], []
    for _ in range(k):
        m = jnp.max(s, axis=0, keepdims=True)
        am = jnp.min(jnp.where(s == m, iota, float(n)), axis=0, keepdims=True)
        vals.append(m)
        idxs.append(am)
        s = jnp.where(iota == am, -jnp.inf, s)
    return jnp.concatenate(vals, axis=0), jnp.concatenate(idxs, axis=0)


def _candidate_rows(x1, x2):
    K, m = x1.shape
    r1, r2, ok = [], [], []
    a = 0
    while K // (a + 1) > 1:
        nb = K // (a + 1)
        width = -(-nb // SUBLANE) * SUBLANE
        r1.append(jnp.broadcast_to(x1[a:a + 1], (width, m)))
        r2.append(x2[0:width])
        ok.append(lax.broadcasted_iota(jnp.int32, (width, m), 0) < nb)
        a += 1
    assert (K - a) % SUBLANE == 0
    r1.append(x1[a:K])
    r2.append(jnp.broadcast_to(x2[0:1], (K - a, m)))
    ok.append(jnp.full((K - a, m), True))
    return jnp.concatenate(r1, axis=0), jnp.concatenate(r2, axis=0), jnp.concatenate(ok, axis=0)


def _peer_route_kernel(q_ref, k_ref, g_ref, e1_ref, e2_ref):
    half = PEER_DQ // 2
    K = PEER_TOPK
    for c0 in range(0, q_ref.shape[0], LANE):
        cols = slice(c0, c0 + LANE)
        q = q_ref[cols, :]
        v1, i1 = _topk_rows(lax.dot_general(k_ref[0, 0], q[:, :half], NT_DIMS, preferred_element_type=F32,
                                            precision=HIGHEST), K)
        v2, i2 = _topk_rows(lax.dot_general(k_ref[0, 1], q[:, half:], NT_DIMS, preferred_element_type=F32,
                                            precision=HIGHEST), K)
        c1, c2, ok = _candidate_rows(v1, v2)
        top, pos = _topk_rows(jnp.where(ok, c1 + c2, -jnp.inf), K)
        p = jnp.exp(top - top[0:1])
        g_ref[:, cols] = p / jnp.sum(p, axis=0, keepdims=True)
        id1, id2, _ = _candidate_rows(i1, i2)
        row = lax.broadcasted_iota(jnp.int32, id1.shape, 0).astype(F32)
        e1, e2 = [], []
        for r in range(K):
            sel = row == pos[r:r + 1]
            e1.append(jnp.sum(jnp.where(sel, id1, 0.0), axis=0, keepdims=True))
            e2.append(jnp.sum(jnp.where(sel, id2, 0.0), axis=0, keepdims=True))
        e1_ref[:, cols] = jnp.concatenate(e1, axis=0)
        e2_ref[:, cols] = jnp.concatenate(e2, axis=0)


def _peer_route(q, keys):
    M = q.shape[0]
    H, _, nk, half = keys.shape
    tm = _tile(M, 256, LANE)
    out = jax.ShapeDtypeStruct((H * PEER_TOPK, M), F32)
    o_spec = pl.BlockSpec((PEER_TOPK, tm), lambda i, h: (h, i))
    return pl.pallas_call(
        _peer_route_kernel,
        grid=(M // tm, H),
        in_specs=[
            pl.BlockSpec((tm, PEER_DQ), lambda i, h: (i, h)),
            pl.BlockSpec((1, 2, nk, half), lambda i, h: (h, 0, 0, 0)),
        ],
        out_specs=[o_spec, o_spec, o_spec],
        out_shape=[out, out, out],
        compiler_params=_params("arbitrary", "arbitrary"),
        name="peer_route",
    )(q, keys)


def _peer_w_kernel(g_ref, e1_ref, e2_ref, o_ref, gt_ref, e1t_ref, e2t_ref, w_scr, *, nk):
    tm = o_ref.shape[0]
    gt_ref[...] = g_ref[...].T
    e1t_ref[...] = e1_ref[...].T
    e2t_ref[...] = e2_ref[...].T
    key_iota = lax.broadcasted_iota(jnp.int32, (nk, g_ref.shape[0]), 0).astype(F32)

    def body(t, carry):
        row = pl.ds(t, 1)
        a = jnp.where(e1t_ref[row, :] == key_iota, gt_ref[row, :], 0.0).astype(BF16)
        b = jnp.where(e2t_ref[row, :] == key_iota, 1.0, 0.0).astype(BF16)
        w_scr[t] = lax.dot_general(a, b, NT_DIMS, preferred_element_type=F32)
        return carry

    lax.fori_loop(0, tm, body, 0, unroll=32)
    tb = _tile(tm, 64)
    for t0 in range(0, tm, tb):
        planes = jnp.swapaxes(w_scr[t0:t0 + tb], 0, 1)
        for j in range(nk):
            o_ref[t0:t0 + tb, j * nk:(j + 1) * nk] = planes[j].astype(o_ref.dtype)


def _peer_w(g, e1, e2, nk):
    S, M = g.shape
    tm = _tile(M, 256, LANE)
    spec = pl.BlockSpec((S, tm), lambda i: (0, i))
    return pl.pallas_call(
        functools.partial(_peer_w_kernel, nk=nk),
        grid=(M // tm,),
        in_specs=[spec, spec, spec],
        out_specs=pl.BlockSpec((tm, nk * nk), lambda i: (i, 0)),
        out_shape=jax.ShapeDtypeStruct((M, nk * nk), BF16),
        scratch_shapes=[pltpu.VMEM((tm, S), F32)] * 3 + [pltpu.VMEM((tm, nk, nk), F32)],
        compiler_params=_params("arbitrary"),
        name="peer_route_weights",
    )(g, e1, e2)


def _gelu_tanh(x):
    return 0.5 * x * (1.0 + jnp.tanh(0.7978845608028654 * (x + 0.044715 * (x * x * x))))


def _peer_dense_kernel(h_ref, u_ref, v_ref, w_ref, o_ref):
    @pl.when(pl.program_id(1) == 0)
    def _():
        o_ref[...] = jnp.zeros_like(o_ref)

    s = lax.dot_general(h_ref[...], u_ref[...], NT_DIMS, preferred_element_type=F32)
    a = (_gelu_tanh(s) * w_ref[...].astype(F32)).astype(BF16)
    o_ref[...] += jnp.dot(a, v_ref[...], preferred_element_type=F32)


def _peer_dense(h2, u, v, w, layer):
    M, D = h2.shape
    E = u.shape[1]
    tm = _tile(M, 512)
    te = _tile(E, 512, LANE)
    return pl.pallas_call(
        _peer_dense_kernel,
        grid=(M // tm, E // te),
        in_specs=[
            pl.BlockSpec((tm, D), lambda i, e: (i, 0)),
            pl.BlockSpec((None, te, D), lambda i, e: (layer, e, 0)),
            pl.BlockSpec((None, te, D), lambda i, e: (layer, e, 0)),
            pl.BlockSpec((tm, te), lambda i, e: (i, e)),
        ],
        out_specs=pl.BlockSpec((tm, D), lambda i, e: (i, 0)),
        out_shape=jax.ShapeDtypeStruct((M, D), F32),
        compiler_params=_params("arbitrary", "arbitrary"),
        name="peer_dense",
    )(h2, u, v, w)


def _peer(h2, wq, keys, u, v, layer):
    nk = keys.shape[2]
    q = _matmul(h2, wq, out_dtype=F32)
    g, e1, e2 = _peer_route(q, keys)
    return _peer_dense(h2, u, v, _peer_w(g, e1, e2, nk), layer)


def _attn_kernel(*refs, scale, two):
    if two:
        q_ref, k1_ref, v1_ref, k2_ref, v2_ref, o_ref = refs
    else:
        q_ref, k1_ref, v1_ref, o_ref = refs
    tq = q_ref.shape[1]
    tg = _tile(tq, 256)
    for r0 in range(0, tq, tg):
        rows = slice(r0, r0 + tg)
        q = _bf(q_ref[0, rows, :].astype(F32) * (scale * LOG2_E))
        s1 = lax.dot_general(q, _bf(k1_ref[0]), NT_DIMS, preferred_element_type=F32)
        m = jnp.max(s1, axis=-1, keepdims=True)
        if two:
            s2 = lax.dot_general(q, _bf(k2_ref[0]), NT_DIMS, preferred_element_type=F32)
            m = jnp.maximum(m, jnp.max(s2, axis=-1, keepdims=True))
        p1 = jnp.exp2(s1 - m)
        l = jnp.sum(p1, axis=-1, keepdims=True)
        o = jnp.dot(_bf(p1), _bf(v1_ref[0]), preferred_element_type=F32)
        if two:
            p2 = jnp.exp2(s2 - m)
            l = l + jnp.sum(p2, axis=-1, keepdims=True)
            o = o + jnp.dot(_bf(p2), _bf(v2_ref[0]), preferred_element_type=F32)
        o_ref[0, rows, :] = (o / l).astype(o_ref.dtype)


def _attention(q, qcol, k1, k1col, v1, v1col, k2=None, k2col=0, v2=None, v2col=0, *, heads, dq, dv, scale):
    B, Tq, _ = q.shape
    tq = _tile(Tq, 512)
    two = k2 is not None

    def spec(arr, col, w, tiled):
        n = arr.shape[1]
        if tiled:
            return pl.BlockSpec((1, tq, w), lambda b, h, t: (b, t, col // w + h))
        return pl.BlockSpec((1, n, w), lambda b, h, t: (b, 0, col // w + h))

    args = [q, k1, v1]
    specs = [spec(q, qcol, dq, True), spec(k1, k1col, dq, False), spec(v1, v1col, dv, False)]
    if two:
        args += [k2, v2]
        specs += [spec(k2, k2col, dq, False), spec(v2, v2col, dv, False)]
    return pl.pallas_call(
        functools.partial(_attn_kernel, scale=scale, two=two),
        grid=(B, heads, Tq // tq),
        in_specs=specs,
        out_specs=pl.BlockSpec((1, tq, dv), lambda b, h, t: (b, t, h)),
        out_shape=jax.ShapeDtypeStruct((B, Tq, heads * dv), BF16),
        compiler_params=_params("arbitrary", "arbitrary", "arbitrary"),
        name="attention",
    )(*args)


def _mla_prep_kernel(z_ref, zr_ref, c_ref, s_ref, gq_ref, gkv_ref, wq_ref, wkv_ref, q_out, k_out, v_out):
    z = z_ref[0]

    def rms(x, g):
        return x * lax.rsqrt(jnp.mean(x * x, axis=-1, keepdims=True) + EPS) * g

    q = jnp.dot(_bf(rms(z[:, :MLA_Q_RANK], gq_ref[...])), wq_ref[...], preferred_element_type=F32)
    kv = jnp.dot(_bf(rms(z[:, MLA_Q_RANK:], gkv_ref[...])), wkv_ref[...], preferred_element_type=F32)
    cos, sin = c_ref[...], s_ref[...]

    def rope(x):
        return x * cos + pltpu.roll(x, MLA_ROPE, 1) * sin

    kr = rope(zr_ref[0]).astype(k_out.dtype)
    hv = MLA_HEADS * MLA_NOPE
    for h in range(MLA_HEADS):
        lo = h * MLA_DQP
        q_out[0, :, lo:lo + MLA_NOPE] = q[:, lo:lo + MLA_NOPE].astype(q_out.dtype)
        q_out[0, :, lo + MLA_NOPE:lo + MLA_DQP] = rope(q[:, lo + MLA_NOPE:lo + MLA_DQP]).astype(q_out.dtype)
        k_out[0, :, lo:lo + MLA_NOPE] = kv[:, h * MLA_NOPE:(h + 1) * MLA_NOPE].astype(k_out.dtype)
        k_out[0, :, lo + MLA_NOPE:lo + MLA_DQP] = kr
    v_out[0] = kv[:, hv:].astype(v_out.dtype)


def _mla_prep(z, cos_t, sin_t, gq, gkv, wq_p, wkv_p):
    B, T, _ = z.shape
    tt = _tile(T, 256)
    wq_w = MLA_HEADS * MLA_DQP
    wkv_w = MLA_HEADS * (MLA_NOPE + MLA_DV)
    cw = MLA_Q_RANK + MLA_KV_RANK
    tab = pl.BlockSpec((tt, LANE), lambda b, t: (t, 0))
    return pl.pallas_call(
        _mla_prep_kernel,
        grid=(B, T // tt),
        in_specs=[
            pl.BlockSpec((1, tt, cw), lambda b, t: (b, t, COL_MQKV // cw)),
            pl.BlockSpec((1, tt, LANE), lambda b, t: (b, t, COL_KR // LANE)),
            tab, tab,
            pl.BlockSpec((1, MLA_Q_RANK), lambda b, t: (0, 0)),
            pl.BlockSpec((1, MLA_KV_RANK), lambda b, t: (0, 0)),
            pl.BlockSpec((MLA_Q_RANK, wq_w), lambda b, t: (0, 0)),
            pl.BlockSpec((MLA_KV_RANK, wkv_w), lambda b, t: (0, 0)),
        ],
        out_specs=[
            pl.BlockSpec((1, tt, wq_w), lambda b, t: (b, t, 0)),
            pl.BlockSpec((1, tt, wq_w), lambda b, t: (b, t, 0)),
            pl.BlockSpec((1, tt, MLA_HEADS * MLA_DV), lambda b, t: (b, t, 0)),
        ],
        out_shape=[
            jax.ShapeDtypeStruct((B, T, wq_w), BF16),
            jax.ShapeDtypeStruct((B, T, wq_w), BF16),
            jax.ShapeDtypeStruct((B, T, MLA_HEADS * MLA_DV), BF16),
        ],
        compiler_params=_params("arbitrary", "arbitrary"),
        name="mla_prep",
    )(z, z, cos_t, sin_t, gq.reshape(1, -1), gkv.reshape(1, -1), wq_p, wkv_p)


def _mla_weights(w_uq, w_ukv):
    half = MLA_ROPE // 2
    wq = w_uq.reshape(MLA_Q_RANK, MLA_HEADS, MLA_NOPE + MLA_ROPE)
    pe = wq[..., MLA_NOPE:]
    pe_sw = jnp.concatenate([pe[..., half:], pe[..., :half]], axis=-1)
    wq_p = jnp.concatenate([wq[..., :MLA_NOPE], pe, pe_sw], axis=-1).reshape(MLA_Q_RANK, MLA_HEADS * MLA_DQP)
    wkv = w_ukv.reshape(MLA_KV_RANK, MLA_HEADS, MLA_NOPE + MLA_DV)
    wkv_p = jnp.concatenate([wkv[..., :MLA_NOPE].reshape(MLA_KV_RANK, -1),
                             wkv[..., MLA_NOPE:].reshape(MLA_KV_RANK, -1)], axis=-1)
    return _bf(wq_p), _bf(wkv_p)


def _rope_tables(n_tok, with_pos):
    n_freq = MLA_ROPE // 4
    zeros = jnp.zeros((n_tok, LANE - MLA_ROPE), F32)
    if not with_pos:
        return (jnp.concatenate([jnp.ones((n_tok, MLA_ROPE), F32), zeros], axis=1),
                jnp.zeros((n_tok, LANE), F32))
    t = jnp.arange(n_tok, dtype=jnp.int32)
    row = (t // GRID_W).astype(F32)
    col = (t % GRID_W).astype(F32)
    inv = ROPE_THETA ** (-jnp.arange(n_freq, dtype=F32) / n_freq)
    ang = jnp.concatenate([row[:, None] * inv, col[:, None] * inv], axis=-1)
    cos, sin = jnp.cos(ang), jnp.sin(ang)
    return (jnp.concatenate([cos, cos, zeros], axis=1), jnp.concatenate([-sin, sin, zeros], axis=1))


def _na_kernel(q_ref, k_ref, v_ref, kc_ref, vc_ref, bm_ref, o_ref, *, scale, rows):
    nblk = rows // NA_QROWS
    nq = NA_QROWS * GRID_W
    nkw = NA_KROWS * GRID_W
    kc = _bf(kc_ref[0])
    vc = _bf(vc_ref[0])

    def body(blk, carry):
        q0 = pl.multiple_of(blk * nq, nq)
        kb = jnp.clip(NA_QROWS * blk - NA_WIN_R // 2, 0, rows - NA_KROWS)
        k0 = pl.multiple_of(kb * GRID_W, (NA_WIN_R // 2) * GRID_W)
        pat = jnp.where(blk == 0, 0, jnp.where(blk == nblk - 1, 2, 1))
        q = _bf(q_ref[0, pl.ds(q0, nq), :] * scale)
        kw = _bf(k_ref[0, pl.ds(k0, nkw), :])
        vw = _bf(v_ref[0, pl.ds(k0, nkw), :])
        bm = bm_ref[0, pat]
        s = lax.dot_general(q, kw, NT_DIMS, preferred_element_type=F32)
        s = jnp.where(bm > 0.5 * NEG_INF, s + bm, NEG_INF)
        sc = lax.dot_general(q, kc, NT_DIMS, preferred_element_type=F32)
        m = jnp.maximum(jnp.max(s, axis=-1, keepdims=True), jnp.max(sc, axis=-1, keepdims=True))
        p = jnp.exp(s - m)
        pc = jnp.exp(sc - m)
        l = jnp.sum(p, axis=-1, keepdims=True) + jnp.sum(pc, axis=-1, keepdims=True)
        o = jnp.dot(_bf(p), vw, preferred_element_type=F32) + jnp.dot(_bf(pc), vc, preferred_element_type=F32)
        o_ref[0, pl.ds(q0, nq), :] = (o / l).astype(o_ref.dtype)
        return carry

    lax.fori_loop(0, nblk, body, 0, unroll=2)


def _na_bias_table(rpb, rows):
    nblk = rows // NA_QROWS
    col = np.arange(GRID_W)
    c_start = np.clip(col - NA_WIN_C // 2, 0, GRID_W - NA_WIN_C)
    in_win = (col[None, :] >= c_start[:, None]) & (col[None, :] < c_start[:, None] + NA_WIN_C)
    dc = np.clip(col[None, :] - col[:, None] + NA_WIN_C - 1, 0, 2 * NA_WIN_C - 2)
    dc_onehot = (dc[:, :, None] == np.arange(2 * NA_WIN_C - 1)).astype(np.float32)
    toeplitz = jnp.einsum("hab,qkb->haqk", rpb.astype(F32), dc_onehot, precision=HIGHEST)
    n_dr = 2 * NA_WIN_R - 1
    sel = np.zeros((3, NA_QROWS, NA_KROWS, n_dr), np.float32)
    valid = np.zeros((3, NA_QROWS, NA_KROWS), bool)
    for p, blk in enumerate((0, min(1, nblk - 1), nblk - 1)):
        kb = int(np.clip(NA_QROWS * blk - NA_WIN_R // 2, 0, rows - NA_KROWS))
        for rq in range(NA_QROWS):
            r = NA_QROWS * blk + rq
            r_start = int(np.clip(r - NA_WIN_R // 2, 0, rows - NA_WIN_R))
            for rk in range(NA_KROWS):
                kr = kb + rk
                if r_start <= kr < r_start + NA_WIN_R:
                    sel[p, rq, rk, kr - r + NA_WIN_R - 1] = 1.0
                    valid[p, rq, rk] = True
    bias = jnp.einsum("prka,haqc->hprqkc", sel, toeplitz, precision=HIGHEST)
    mask = valid[None, :, :, None, :, None] & in_win[None, None, None, :, None, :]
    H = rpb.shape[0]
    return jnp.where(mask, bias, NEG_INF).reshape(H, 3, NA_QROWS * GRID_W, NA_KROWS * GRID_W)


def _na(z, zc, rpb):
    B, T, _ = z.shape
    Tc = zc.shape[1]
    rows = T // GRID_W
    assert rows % NA_QROWS == 0 and rows >= NA_KROWS
    bm = _na_bias_table(rpb, rows)
    dh = NA_DH

    def seq(n, col):
        return pl.BlockSpec((1, n, dh), lambda b, h: (b, 0, col // dh + h))

    return pl.pallas_call(
        functools.partial(_na_kernel, scale=dh ** -0.5, rows=rows),
        grid=(B, NA_HEADS),
        in_specs=[seq(T, COL_NQ), seq(T, COL_NK), seq(T, COL_NV), seq(Tc, COL_NK), seq(Tc, COL_NV),
                  pl.BlockSpec((1,) + bm.shape[1:], lambda b, h: (h, 0, 0, 0))],
        out_specs=pl.BlockSpec((1, T, dh), lambda b, h: (b, 0, h)),
        out_shape=jax.ShapeDtypeStruct((B, T, NA_HEADS * dh), BF16),
        compiler_params=_params("arbitrary", "arbitrary"),
        name="neighbourhood_attention",
    )(z, z, z, zc, zc, bm)


def _lru_kernel(g_ref, x_ref, gc_ref, xc_ref, cw_ref, cb_ref, wa_ref, ba_ref, wx_ref, bx_ref, lam_ref,
                y_ref, yc_ref, a_scr, u_scr, h_scr):
    row8 = lax.broadcasted_iota(jnp.int32, (SUBLANE, LANE), 0)

    def scan_pair(n_tok, h0f, h0b):
        nb = n_tok // SUBLANE

        def body(i, carry):
            hf, hb = carry
            rf = pl.multiple_of(i * SUBLANE, SUBLANE)
            rb = pl.multiple_of((nb - 1 - i) * SUBLANE, SUBLANE)
            A, U = a_scr[0, pl.ds(rf, SUBLANE), :], u_scr[0, pl.ds(rf, SUBLANE), :]
            Ab, Ub = a_scr[1, pl.ds(rb, SUBLANE), :], u_scr[1, pl.ds(rb, SUBLANE), :]
            for s in (1, 2, 4):
                m = row8 >= s
                U = jnp.where(m, A * pltpu.roll(U, s, 0) + U, U)
                A = jnp.where(m, A * pltpu.roll(A, s, 0), A)
                mb = row8 < SUBLANE - s
                Ub = jnp.where(mb, Ab * pltpu.roll(Ub, SUBLANE - s, 0) + Ub, Ub)
                Ab = jnp.where(mb, Ab * pltpu.roll(Ab, SUBLANE - s, 0), Ab)
            hbf = A * hf + U
            hbb = Ab * hb + Ub
            h_scr[0, pl.ds(rf, SUBLANE), :] = hbf
            h_scr[1, pl.ds(rb, SUBLANE), :] = hbb
            return hbf[SUBLANE - 1:SUBLANE, :], hbb[0:1, :]

        return lax.fori_loop(0, nb, body, (h0f, h0b), unroll=4)

    def run(gate_ref, zx_ref, out_ref, n_tok, h0f, h0b):
        x = zx_ref[0]
        t = lax.broadcasted_iota(jnp.int32, x.shape, 0)
        w = cw_ref[...]
        xc = (w[0:1] * jnp.where(t >= 2, pltpu.roll(x, 2, 0), 0.0)
              + w[1:2] * jnp.where(t >= 1, pltpu.roll(x, 1, 0), 0.0)
              + w[2:3] * x
              + w[3:4] * jnp.where(t < n_tok - 1, pltpu.roll(x, n_tok - 1, 0), 0.0)) + cb_ref[...]
        xcb = _bf(xc)
        for d in range(2):
            r = jax.nn.sigmoid(jnp.dot(xcb, wa_ref[d, 0], preferred_element_type=F32) + ba_ref[d:d + 1, :])
            i = jax.nn.sigmoid(jnp.dot(xcb, wx_ref[d, 0], preferred_element_type=F32) + bx_ref[d:d + 1, :])
            nl = -lam_ref[d:d + 1, :]
            softplus = jnp.maximum(nl, 0.0) + jnp.log1p(jnp.exp(-jnp.abs(nl)))
            log_a = -LRU_C * r * softplus
            a = jnp.exp(log_a)
            a_scr[d, 0:n_tok, :] = a
            u_scr[d, 0:n_tok, :] = jnp.sqrt(1.0 - jnp.exp(2.0 * log_a)) * i * xc
        hf, hb = scan_pair(n_tok, h0f, h0b)
        h = h_scr[0, 0:n_tok, :] + h_scr[1, 0:n_tok, :]
        out_ref[0] = (_gelu_tanh(gate_ref[0]) * h).astype(out_ref.dtype)
        return hf, hb

    zero = jnp.zeros((1, LANE), F32)
    hf, hb = run(gc_ref, xc_ref, yc_ref, xc_ref.shape[1], zero, zero)
    run(g_ref, x_ref, y_ref, x_ref.shape[1], hf, hb)


def _lru(z, zc, conv_w, conv_b, wa, ba, wx, bx, lam):
    B, T, _ = z.shape
    Tc = zc.shape[1]
    bw = LRU_BW

    def seq(n, col):
        return pl.BlockSpec((1, n, bw), lambda b, j: (b, 0, col // bw + j))

    vec2 = pl.BlockSpec((2, bw), lambda b, j: (0, j))
    mat = pl.BlockSpec((2, 1, bw, bw), lambda b, j: (0, j, 0, 0))
    out = lambda n: pl.BlockSpec((1, n, bw), lambda b, j: (b, 0, j))
    return pl.pallas_call(
        _lru_kernel,
        grid=(B, LRU_BLOCKS),
        in_specs=[seq(T, COL_LG), seq(T, COL_LX), seq(Tc, COL_LG), seq(Tc, COL_LX),
                  pl.BlockSpec((CONV_W, bw), lambda b, j: (0, j)), pl.BlockSpec((1, bw), lambda b, j: (0, j)),
                  mat, vec2, mat, vec2, vec2],
        out_specs=[out(T), out(Tc)],
        out_shape=[jax.ShapeDtypeStruct((B, T, LRU_WIDTH), BF16), jax.ShapeDtypeStruct((B, Tc, LRU_WIDTH), BF16)],
        scratch_shapes=[pltpu.VMEM((2, T, bw), F32)] * 3,
        compiler_params=_params("arbitrary", "arbitrary"),
        name="rg_lru",
    )(z, z, zc, zc, conv_w, conv_b.reshape(1, -1), _bf(wa), ba, _bf(wx), bx, lam)


def _gla_kernel(q_ref, k_ref, v_ref, g_ref, d_ref, qc_ref, kc_ref, vc_ref, gc_ref, dc_ref,
                wa_ref, ba_ref, ng_ref, y_ref, yc_ref, b_scr, s_scr):
    C = GLA_CHUNK
    ri = lax.broadcasted_iota(jnp.int32, (C, C), 0)
    ci = lax.broadcasted_iota(jnp.int32, (C, C), 1)
    keep = (ri >= ci, ri <= ci)
    tri = (keep[0].astype(F32), keep[1].astype(F32))
    scale = GLA_DK ** -0.5

    def chunk_rows(c):
        return pl.ds(pl.multiple_of(c * C, C), C)

    def run(refs, gate_ref, dec_ref, out_ref, n_tok, st_f, st_b):
        qr, kr, vr = refs
        n = n_tok // C
        dec = dec_ref[0]
        for d in range(2):
            x = jnp.dot(dec, wa_ref[d], preferred_element_type=F32, precision=HIGHEST) + ba_ref[d:d + 1, :]
            log_sig = jnp.minimum(x, 0.0) - jnp.log1p(jnp.exp(-jnp.abs(x)))
            b_scr[d, 0:n_tok, :] = log_sig * (1.0 / GLA_TAU)

        def cumulate(c, carry):
            rows = chunk_rows(c)
            for d in range(2):
                b_scr[d, rows, :] = jnp.dot(tri[d], b_scr[d, rows, :], preferred_element_type=F32,
                                            precision=HIGHEST)
            return carry

        lax.fori_loop(0, n, cumulate, 0, unroll=4)

        def advance(i, states):
            out = []
            for d, st in enumerate(states):
                c = i if d == 0 else n - 1 - i
                rows = chunk_rows(c)
                b = b_scr[d, rows, :]
                b_tot = b[C - 1:C, :] if d == 0 else b[0:1, :]
                ke = _bf(kr[0, rows, :] * jnp.exp(b_tot - b))
                s_scr[d, c] = st.astype(s_scr.dtype)
                out.append(st * jnp.exp(b_tot)
                           + lax.dot_general(_bf(vr[0, rows, :]), ke, TN_DIMS, preferred_element_type=F32))
            return tuple(out)

        st_f, st_b = lax.fori_loop(0, n, advance, (st_f, st_b), unroll=4)

        def emit(c, carry):
            rows = chunk_rows(c)
            q = qr[0, rows, :] * scale
            k = kr[0, rows, :]
            v = _bf(vr[0, rows, :])
            o = None
            for d in range(2):
                b = b_scr[d, rows, :]
                qd = _bf(q * jnp.exp(b))
                att = lax.dot_general(qd, _bf(k * jnp.exp(-b)), NT_DIMS, preferred_element_type=F32)
                att = jnp.where(keep[d], att, 0.0)
                od = (jnp.dot(_bf(att), v, preferred_element_type=F32)
                      + lax.dot_general(qd, s_scr[d, c], NT_DIMS, preferred_element_type=F32))
                o = od if o is None else o + od
            o = o * lax.rsqrt(jnp.mean(o * o, axis=-1, keepdims=True) + EPS) * ng_ref[...]
            gate = gate_ref[0, rows, :]
            out_ref[0, rows, :] = (o * (gate * jax.nn.sigmoid(gate))).astype(out_ref.dtype)
            return carry

        lax.fori_loop(0, n, emit, 0, unroll=2)
        return st_f, st_b

    zero = jnp.zeros((GLA_DV, GLA_DK), F32)
    st_f, st_b = run((qc_ref, kc_ref, vc_ref), gc_ref, dc_ref, yc_ref, qc_ref.shape[1], zero, zero)
    run((q_ref, k_ref, v_ref), g_ref, d_ref, y_ref, q_ref.shape[1], st_f, st_b)


def _gla(z, zc, wa2, ba, norm_g):
    B, T, _ = z.shape
    Tc = zc.shape[1]
    wa_p = jnp.zeros((2, LANE, GLA_HEADS * GLA_DK), F32)
    wa_p = wa_p.at[0, :GLA_RANK].set(wa2[0]).at[1, GLA_RANK:2 * GLA_RANK].set(wa2[1])

    def seq(n, col, w):
        return pl.BlockSpec((1, n, w), lambda b, h: (b, 0, col // w + h), pipeline_mode=pl.Buffered(1))

    def dec(n):
        return pl.BlockSpec((1, n, LANE), lambda b, h: (b, 0, COL_DEC // LANE), pipeline_mode=pl.Buffered(1))

    def ins(n):
        return [seq(n, COL_GQ, GLA_DK), seq(n, COL_GK, GLA_DK), seq(n, COL_GV, GLA_DV), seq(n, COL_GG, GLA_DV), dec(n)]

    out = lambda n: pl.BlockSpec((1, n, GLA_DV), lambda b, h: (b, 0, h))
    return pl.pallas_call(
        _gla_kernel,
        grid=(B, GLA_HEADS),
        in_specs=ins(T) + ins(Tc) + [
            pl.BlockSpec((2, LANE, GLA_DK), lambda b, h: (0, 0, h)),
            pl.BlockSpec((2, GLA_DK), lambda b, h: (0, h)),
            pl.BlockSpec((1, GLA_DV), lambda b, h: (0, 0)),
        ],
        out_specs=[out(T), out(Tc)],
        out_shape=[jax.ShapeDtypeStruct((B, T, BRANCH_W), BF16), jax.ShapeDtypeStruct((B, Tc, BRANCH_W), BF16)],
        scratch_shapes=[pltpu.VMEM((2, T, GLA_DK), F32), pltpu.VMEM((2, T // GLA_CHUNK, GLA_DV, GLA_DK), BF16)],
        compiler_params=_params("arbitrary", "arbitrary"),
        name="gla",
    )(z, z, z, z, z, zc, zc, zc, zc, zc, wa_p, ba, norm_g.reshape(1, -1))


def _permute_mix_weight(w):
    o = np.cumsum((0,) + MIX_SIZES)
    gq, gk, gv, gg, af, ab, cq, ckv, kr, lg, lx, nq, nk, nv = [w[:, o[i]:o[i + 1]] for i in range(len(MIX_SIZES))]
    half = MLA_ROPE // 2
    kr_sw = jnp.concatenate([kr[:, half:], kr[:, :half]], axis=1)
    pad = jnp.zeros((w.shape[0], MIXP - (COL_DEC + 2 * GLA_RANK)), w.dtype)
    return _bf(jnp.concatenate([gv, gg, lg, lx, nq, nk, nv, gq, gk, cq, ckv, kr, kr_sw, af, ab, pad], axis=1))


def kernel(x, c, ctx, c_ctx, w_ada, b_ada, norm1_g, norm2_g, w_in, gla_wa2, gla_ba, gla_norm_g, mla_q_norm_g, mla_w_uq, mla_kv_norm_g, mla_w_ukv, lru_conv_w, lru_conv_b, lru_wa, lru_ba, lru_wx, lru_bx, lru_lambda, na_rpb, w_branch, w_out, peer_wq, peer_keys, peer_u, peer_v, final_norm_g):
    B, T, D = x.shape
    Tc = ctx.shape[1]
    L = w_ada.shape[0]
    rope_l = _rope_tables(T, True)
    rope_c = _rope_tables(Tc, False)

    n_rows = -(-(B + 1) // SUBLANE) * SUBLANE
    cc = jnp.zeros((n_rows, D), F32).at[:B].set(c).at[B].set(c_ctx)
    mods = _ada(cc, w_ada, b_ada)

    xc = ctx
    pe_l = pe_c = gate_l = gate_c = None
    pu = _bf(peer_u)
    pv = _bf(peer_v)
    for l in range(L):
        update_ctx = l < L - 1
        ml = jnp.split(mods[l, :B], 6, axis=-1)
        mc = jnp.split(jnp.broadcast_to(mods[l, B], (B, 6 * D)), 6, axis=-1)
        w_mix = _permute_mix_weight(w_in[l, :, :MIX_COLS])
        w_gate = _bf(w_in[l, :, MIX_COLS:])
        wb = _bf(w_branch[l])
        wo = _bf(w_out[l])
        wq = _bf(peer_wq[l])
        mla_wq, mla_wkv = _mla_weights(mla_w_uq[l], mla_w_ukv[l])

        xn, h = _norm(x, norm1_g[l], delta=pe_l, gate=gate_l, shift=ml[0], scale=ml[1])
        x = x if xn is None else xn
        xcn, hc = _norm(xc, norm1_g[l], delta=pe_c, gate=gate_c, shift=mc[0], scale=mc[1])
        xc = xc if xcn is None else xcn

        h2d = h.reshape(B * T, D)
        hc2d = hc.reshape(B * Tc, D)
        z = _matmul(h2d, w_mix, out_dtype=F32).reshape(B, T, MIXP)
        zc = _matmul(hc2d, w_mix, out_dtype=F32).reshape(B, Tc, MIXP)

        y_gla, yc_gla = _gla(z, zc, gla_wa2[l], gla_ba[l], gla_norm_g[l])
        y_lru, yc_lru = _lru(z, zc, lru_conv_w[l], lru_conv_b[l], lru_wa[l], lru_ba[l], lru_wx[l], lru_bx[l],
                             lru_lambda[l])
        ql, kl, vl = _mla_prep(z, *rope_l, mla_q_norm_g[l], mla_kv_norm_g[l], mla_wq, mla_wkv)
        qc, kc, vc = _mla_prep(zc, *rope_c, mla_q_norm_g[l], mla_kv_norm_g[l], mla_wq, mla_wkv)
        mla_args = dict(heads=MLA_HEADS, dq=MLA_DQP, dv=MLA_DV, scale=(MLA_NOPE + MLA_ROPE) ** -0.5)
        y_mla = _attention(ql, 0, kc, 0, vc, 0, kl, 0, vl, 0, **mla_args)
        y_na = _na(z, zc, na_rpb[l])
        ys_l = [y_gla, y_mla, y_lru, y_na]
        if update_ctx:
            yc_mla = _attention(qc, 0, kc, 0, vc, 0, **mla_args)
            yc_na = _attention(zc, COL_NQ, zc, COL_NK, zc, COL_NV, heads=NA_HEADS, dq=NA_DH, dv=NA_DH,
                               scale=NA_DH ** -0.5)
            ys_c = [yc_gla, yc_mla, yc_lru, yc_na]

        def channel_mix(xs, hs, ys_s, m, n_tok):
            M = B * n_tok
            gates = _matmul(hs, w_gate, out_dtype=BF16, act="sigmoid")
            mrg = _merge([y.reshape(M, BRANCH_W) for y in ys_s], gates, wb)
            xs = _matmul(mrg, wo, out_dtype=F32, res=xs.reshape(M, D), mod=m[2],
                         rows_per_batch=n_tok).reshape(B, n_tok, D)
            _, h2 = _norm(xs, norm2_g[l], shift=m[3], scale=m[4])
            pe = _peer(h2.reshape(M, D), wq, peer_keys[l], pu, pv, l).reshape(B, n_tok, D)
            return xs, pe

        x, pe_l = channel_mix(x, h2d, ys_l, ml, T)
        gate_l = ml[5]
        if update_ctx:
            xc, pe_c = channel_mix(xc, hc2d, ys_c, mc, Tc)
            gate_c = mc[5]
        else:
            pe_c = gate_c = None
    _, out = _norm(x, final_norm_g, delta=pe_l, gate=gate_l, out_dtype=F32, emit_x=False)
    return out
```

```python
import functools

import jax
import jax.numpy as jnp
import numpy as np
from jax import lax
from jax.experimental import pallas as pl
from jax.experimental.pallas import tpu as pltpu

GRID_W = 64
EPS = 1e-6
NEG_INF = -1e30
N_BRANCH = 4
BRANCH_W = 1024
ROPE_THETA = 10000.0

GLA_HEADS = 4
GLA_DK = 128
GLA_DV = BRANCH_W // GLA_HEADS
GLA_RANK = 16
GLA_TAU = 16.0
GLA_CHUNK = 64

MLA_HEADS = 8
MLA_Q_RANK = 768
MLA_KV_RANK = 256
MLA_NOPE = 128
MLA_ROPE = 64
MLA_DV = BRANCH_W // MLA_HEADS
MLA_DQP = 256

LRU_WIDTH = BRANCH_W
LRU_BLOCKS = 8
LRU_BW = LRU_WIDTH // LRU_BLOCKS
CONV_W = 4
LRU_C = 8.0

NA_HEADS = 8
NA_DH = BRANCH_W // NA_HEADS
NA_WIN_R = 8
NA_WIN_C = 16
NA_QROWS = 8
NA_KROWS = 16

PEER_HEADS = 8
PEER_DQ = 256
PEER_TOPK = 16

MIX_SIZES = (
    GLA_HEADS * GLA_DK, GLA_HEADS * GLA_DK, GLA_HEADS * GLA_DV, GLA_HEADS * GLA_DV, GLA_RANK, GLA_RANK,
    MLA_Q_RANK, MLA_KV_RANK, MLA_ROPE, LRU_WIDTH, LRU_WIDTH,
    NA_HEADS * NA_DH, NA_HEADS * NA_DH, NA_HEADS * NA_DH,
)
MIX_COLS = sum(MIX_SIZES)

COL_GV, COL_GG, COL_LG, COL_LX, COL_NQ, COL_NK, COL_NV = 0, 1024, 2048, 3072, 4096, 5120, 6144
COL_GQ, COL_GK = 7168, 7680
COL_MQKV = 8192
COL_KR = 9216
COL_DEC = 9344
MIXP = 9728

V7X_VMEM_BYTES = 64 * 1024 * 1024
VMEM_LIMIT = V7X_VMEM_BYTES - 8 * 1024 * 1024
LANE = 128
SUBLANE = 8

F32 = jnp.float32
BF16 = jnp.bfloat16
HIGHEST = lax.Precision.HIGHEST
LOG2_E = 1.4426950408889634
NT_DIMS = (((1,), (1,)), ((), ()))
TN_DIMS = (((0,), (0,)), ((), ()))


def _tile(n, pref, mult=SUBLANE):
    if n <= pref:
        return n
    t = (pref // mult) * mult
    while t > mult and n % t:
        t -= mult
    assert n % t == 0, (n, pref, mult)
    return t


def _params(*sem):
    return pltpu.CompilerParams(dimension_semantics=sem, vmem_limit_bytes=VMEM_LIMIT)


def _bf(x):
    return x.astype(BF16)


def _ada_kernel(c_ref, w_ref, b_ref, o_ref):
    cv = c_ref[...]
    a = cv * jax.nn.sigmoid(cv)
    o_ref[0] = jnp.dot(a, w_ref[0], preferred_element_type=F32, precision=HIGHEST) + b_ref[0]


def _ada(cc, w_ada, b_ada):
    L, D, W = w_ada.shape
    R = cc.shape[0]
    tn = _tile(W, 512, LANE)
    return pl.pallas_call(
        _ada_kernel,
        grid=(L, W // tn),
        in_specs=[
            pl.BlockSpec((R, D), lambda l, j: (0, 0)),
            pl.BlockSpec((1, D, tn), lambda l, j: (l, 0, j)),
            pl.BlockSpec((1, 1, tn), lambda l, j: (l, 0, j)),
        ],
        out_specs=pl.BlockSpec((1, R, tn), lambda l, j: (l, 0, j)),
        out_shape=jax.ShapeDtypeStruct((L, R, W), F32),
        compiler_params=_params("arbitrary", "arbitrary"),
        name="ada_mod",
    )(cc, w_ada, b_ada.reshape(L, 1, W))


def _norm_kernel(*refs, has_delta, modulate, emit_x):
    it = iter(refs)
    x_ref = next(it)
    if has_delta:
        d_ref, gate_ref = next(it), next(it)
    g_ref = next(it)
    if modulate:
        shift_ref, scale_ref = next(it), next(it)
    if emit_x:
        xo_ref = next(it)
    h_ref = next(it)
    x = x_ref[0]
    if has_delta:
        x = x + gate_ref[0] * d_ref[0]
    if emit_x:
        xo_ref[0] = x
    y = x * lax.rsqrt(jnp.mean(x * x, axis=-1, keepdims=True) + EPS)
    y = y * g_ref[...]
    if modulate:
        y = y * (1.0 + scale_ref[0]) + shift_ref[0]
    h_ref[0] = y.astype(h_ref.dtype)


def _norm(x, g, *, delta=None, gate=None, shift=None, scale=None, out_dtype=None, emit_x=True):
    out_dtype = BF16 if out_dtype is None else out_dtype
    B, T, D = x.shape
    tt = _tile(T, 256)
    has_delta = delta is not None
    emit_x = emit_x and has_delta
    modulate = shift is not None
    tok = pl.BlockSpec((1, tt, D), lambda b, t: (b, t, 0))
    vec = pl.BlockSpec((1, 1, D), lambda b, t: (b, 0, 0))
    args, specs = [x], [tok]
    if has_delta:
        args += [delta, gate.reshape(B, 1, D)]
        specs += [tok, vec]
    args.append(g.reshape(1, D))
    specs.append(pl.BlockSpec((1, D), lambda b, t: (0, 0)))
    if modulate:
        args += [shift.reshape(B, 1, D), scale.reshape(B, 1, D)]
        specs += [vec, vec]
    out_shape, out_specs = [], []
    if emit_x:
        out_shape.append(jax.ShapeDtypeStruct((B, T, D), F32))
        out_specs.append(tok)
    out_shape.append(jax.ShapeDtypeStruct((B, T, D), out_dtype))
    out_specs.append(tok)
    outs = pl.pallas_call(
        functools.partial(_norm_kernel, has_delta=has_delta, modulate=modulate, emit_x=emit_x),
        grid=(B, T // tt),
        in_specs=specs,
        out_specs=out_specs,
        out_shape=out_shape,
        compiler_params=_params("arbitrary", "arbitrary"),
        name="res_norm_mod",
    )(*args)
    if emit_x:
        return outs[0], outs[1]
    return None, outs[0]


def _mm_kernel(*refs, act, has_res):
    if has_res:
        a_ref, b_ref, r_ref, m_ref, o_ref = refs
    else:
        a_ref, b_ref, o_ref = refs
    acc = jnp.dot(a_ref[...], b_ref[...], preferred_element_type=F32)
    if act == "sigmoid":
        acc = jax.nn.sigmoid(acc)
    if has_res:
        acc = r_ref[...] + m_ref[0] * acc
    o_ref[...] = acc.astype(o_ref.dtype)


def _matmul(a, b, *, out_dtype, act=None, res=None, mod=None, rows_per_batch=None, tm=1024, tn=1024):
    M, K = a.shape
    N = b.shape[1]
    tm = _tile(rows_per_batch if rows_per_batch else M, tm)
    tn = _tile(N, tn, LANE)
    has_res = res is not None
    args = [a, b]
    specs = [pl.BlockSpec((tm, K), lambda i, j: (i, 0)), pl.BlockSpec((K, tn), lambda i, j: (0, j))]
    if has_res:
        args += [res, mod.reshape(mod.shape[0], 1, N)]
        specs += [
            pl.BlockSpec((tm, tn), lambda i, j: (i, j)),
            pl.BlockSpec((1, 1, tn), lambda i, j: ((i * tm) // rows_per_batch, 0, j)),
        ]
    return pl.pallas_call(
        functools.partial(_mm_kernel, act=act, has_res=has_res),
        grid=(M // tm, N // tn),
        in_specs=specs,
        out_specs=pl.BlockSpec((tm, tn), lambda i, j: (i, j)),
        out_shape=jax.ShapeDtypeStruct((M, N), out_dtype),
        compiler_params=_params("arbitrary", "arbitrary"),
        name="matmul",
    )(*args)


def _merge_kernel(y0, y1, y2, y3, g0, g1, g2, g3, w_ref, o_ref):
    acc = None
    for i, (y, g) in enumerate(((y0, g0), (y1, g1), (y2, g2), (y3, g3))):
        p = g[...].astype(F32) * jnp.dot(y[...], w_ref[i], preferred_element_type=F32)
        acc = p if acc is None else acc + p
    o_ref[...] = acc.astype(o_ref.dtype)


def _merge(ys, gates, w_branch):
    M = ys[0].shape[0]
    D = w_branch.shape[2]
    tm = _tile(M, 1024)
    tn = _tile(D, 512, LANE)
    nj = D // tn
    y_spec = pl.BlockSpec((tm, BRANCH_W), lambda i, j: (i, 0))
    g_specs = [pl.BlockSpec((tm, tn), functools.partial(lambda i, j, br: (i, br * nj + j), br=br))
               for br in range(N_BRANCH)]
    return pl.pallas_call(
        _merge_kernel,
        grid=(M // tm, nj),
        in_specs=[y_spec] * N_BRANCH + g_specs + [pl.BlockSpec((N_BRANCH, BRANCH_W, tn), lambda i, j: (0, 0, j))],
        out_specs=pl.BlockSpec((tm, tn), lambda i, j: (i, j)),
        out_shape=jax.ShapeDtypeStruct((M, D), BF16),
        compiler_params=_params("arbitrary", "arbitrary"),
        name="merge",
    )(*ys, gates, gates, gates, gates, w_branch)


def _topk_rows(s, k):
    n = s.shape[0]
    iota = lax.broadcasted_iota(jnp.int32, s.shape, 0).astype(F32)
    vals, idxs = [], []
    for _ in range(k):
        m = jnp.max(s, axis=0, keepdims=True)
        am = jnp.min(jnp.where(s == m, iota, float(n)), axis=0, keepdims=True)
        vals.append(m)
        idxs.append(am)
        s = jnp.where(iota == am, -jnp.inf, s)
    return jnp.concatenate(vals, axis=0), jnp.concatenate(idxs, axis=0)


def _candidate_rows(x1, x2):
    K, m = x1.shape
    r1, r2, ok = [], [], []
    a = 0
    while K // (a + 1) > 1:
        nb = K // (a + 1)
        width = -(-nb // SUBLANE) * SUBLANE
        r1.append(jnp.broadcast_to(x1[a:a + 1], (width, m)))
        r2.append(x2[0:width])
        ok.append(lax.broadcasted_iota(jnp.int32, (width, m), 0) < nb)
        a += 1
    assert (K - a) % SUBLANE == 0
    r1.append(x1[a:K])
    r2.append(jnp.broadcast_to(x2[0:1], (K - a, m)))
    ok.append(jnp.full((K - a, m), True))
    return jnp.concatenate(r1, axis=0), jnp.concatenate(r2, axis=0), jnp.concatenate(ok, axis=0)


def _peer_route_kernel(q_ref, k_ref, g_ref, e1_ref, e2_ref):
    half = PEER_DQ // 2
    K = PEER_TOPK
    for c0 in range(0, q_ref.shape[0], LANE):
        cols = slice(c0, c0 + LANE)
        q = q_ref[cols, :]
        v1, i1 = _topk_rows(lax.dot_general(k_ref[0, 0], q[:, :half], NT_DIMS, preferred_element_type=F32,
                                            precision=HIGHEST), K)
        v2, i2 = _topk_rows(lax.dot_general(k_ref[0, 1], q[:, half:], NT_DIMS, preferred_element_type=F32,
                                            precision=HIGHEST), K)
        c1, c2, ok = _candidate_rows(v1, v2)
        top, pos = _topk_rows(jnp.where(ok, c1 + c2, -jnp.inf), K)
        p = jnp.exp(top - top[0:1])
        g_ref[:, cols] = p / jnp.sum(p, axis=0, keepdims=True)
        id1, id2, _ = _candidate_rows(i1, i2)
        row = lax.broadcasted_iota(jnp.int32, id1.shape, 0).astype(F32)
        e1, e2 = [], []
        for r in range(K):
            sel = row == pos[r:r + 1]
            e1.append(jnp.sum(jnp.where(sel, id1, 0.0), axis=0, keepdims=True))
            e2.append(jnp.sum(jnp.where(sel, id2, 0.0), axis=0, keepdims=True))
        e1_ref[:, cols] = jnp.concatenate(e1, axis=0)
        e2_ref[:, cols] = jnp.concatenate(e2, axis=0)


def _peer_route(q, keys):
    M = q.shape[0]
    H, _, nk, half = keys.shape
    tm = _tile(M, 256, LANE)
    out = jax.ShapeDtypeStruct((H * PEER_TOPK, M), F32)
    o_spec = pl.BlockSpec((PEER_TOPK, tm), lambda i, h: (h, i))
    return pl.pallas_call(
        _peer_route_kernel,
        grid=(M // tm, H),
        in_specs=[
            pl.BlockSpec((tm, PEER_DQ), lambda i, h: (i, h)),
            pl.BlockSpec((1, 2, nk, half), lambda i, h: (h, 0, 0, 0)),
        ],
        out_specs=[o_spec, o_spec, o_spec],
        out_shape=[out, out, out],
        compiler_params=_params("arbitrary", "arbitrary"),
        name="peer_route",
    )(q, keys)


def _peer_w_kernel(g_ref, e1_ref, e2_ref, o_ref, gt_ref, e1t_ref, e2t_ref, w_scr, *, nk):
    tm = o_ref.shape[0]
    gt_ref[...] = g_ref[...].T
    e1t_ref[...] = e1_ref[...].T
    e2t_ref[...] = e2_ref[...].T
    key_iota = lax.broadcasted_iota(jnp.int32, (nk, g_ref.shape[0]), 0).astype(F32)

    def body(t, carry):
        row = pl.ds(t, 1)
        a = jnp.where(e1t_ref[row, :] == key_iota, gt_ref[row, :], 0.0).astype(BF16)
        b = jnp.where(e2t_ref[row, :] == key_iota, 1.0, 0.0).astype(BF16)
        w_scr[t] = lax.dot_general(a, b, NT_DIMS, preferred_element_type=F32)
        return carry

    lax.fori_loop(0, tm, body, 0, unroll=32)
    tb = _tile(tm, 64)
    for t0 in range(0, tm, tb):
        planes = jnp.swapaxes(w_scr[t0:t0 + tb], 0, 1)
        for j in range(nk):
            o_ref[t0:t0 + tb, j * nk:(j + 1) * nk] = planes[j].astype(o_ref.dtype)


def _peer_w(g, e1, e2, nk):
    S, M = g.shape
    tm = _tile(M, 256, LANE)
    spec = pl.BlockSpec((S, tm), lambda i: (0, i))
    return pl.pallas_call(
        functools.partial(_peer_w_kernel, nk=nk),
        grid=(M // tm,),
        in_specs=[spec, spec, spec],
        out_specs=pl.BlockSpec((tm, nk * nk), lambda i: (i, 0)),
        out_shape=jax.ShapeDtypeStruct((M, nk * nk), BF16),
        scratch_shapes=[pltpu.VMEM((tm, S), F32)] * 3 + [pltpu.VMEM((tm, nk, nk), F32)],
        compiler_params=_params("arbitrary"),
        name="peer_route_weights",
    )(g, e1, e2)


def _gelu_tanh(x):
    return 0.5 * x * (1.0 + jnp.tanh(0.7978845608028654 * (x + 0.044715 * (x * x * x))))


def _peer_dense_kernel(h_ref, u_ref, v_ref, w_ref, o_ref):
    @pl.when(pl.program_id(1) == 0)
    def _():
        o_ref[...] = jnp.zeros_like(o_ref)

    s = lax.dot_general(h_ref[...], u_ref[...], NT_DIMS, preferred_element_type=F32)
    a = (_gelu_tanh(s) * w_ref[...].astype(F32)).astype(BF16)
    o_ref[...] += jnp.dot(a, v_ref[...], preferred_element_type=F32)


def _peer_dense(h2, u, v, w, layer):
    M, D = h2.shape
    E = u.shape[1]
    tm = _tile(M, 1024)
    te = _tile(E, 512, LANE)
    once = pl.Buffered(1)
    return pl.pallas_call(
        _peer_dense_kernel,
        grid=(M // tm, E // te),
        in_specs=[
            pl.BlockSpec((tm, D), lambda i, e: (i, 0), pipeline_mode=once),
            pl.BlockSpec((None, te, D), lambda i, e: (layer, e, 0)),
            pl.BlockSpec((None, te, D), lambda i, e: (layer, e, 0)),
            pl.BlockSpec((tm, te), lambda i, e: (i, e)),
        ],
        out_specs=pl.BlockSpec((tm, D), lambda i, e: (i, 0), pipeline_mode=once),
        out_shape=jax.ShapeDtypeStruct((M, D), F32),
        compiler_params=_params("arbitrary", "arbitrary"),
        name="peer_dense",
    )(h2, u, v, w)


def _peer(h2, wq, keys, u, v, layer):
    nk = keys.shape[2]
    q = _matmul(h2, wq, out_dtype=F32)
    g, e1, e2 = _peer_route(q, keys)
    return _peer_dense(h2, u, v, _peer_w(g, e1, e2, nk), layer)


def _attn_kernel(*refs, scale, two):
    if two:
        q_ref, k1_ref, v1_ref, k2_ref, v2_ref, o_ref = refs
    else:
        q_ref, k1_ref, v1_ref, o_ref = refs
    tq = q_ref.shape[1]
    tg = _tile(tq, 256)
    for r0 in range(0, tq, tg):
        rows = slice(r0, r0 + tg)
        q = _bf(q_ref[0, rows, :].astype(F32) * (scale * LOG2_E))
        s1 = lax.dot_general(q, _bf(k1_ref[0]), NT_DIMS, preferred_element_type=F32)
        m = jnp.max(s1, axis=-1, keepdims=True)
        if two:
            s2 = lax.dot_general(q, _bf(k2_ref[0]), NT_DIMS, preferred_element_type=F32)
            m = jnp.maximum(m, jnp.max(s2, axis=-1, keepdims=True))
        p1 = jnp.exp2(s1 - m)
        l = jnp.sum(p1, axis=-1, keepdims=True)
        o = jnp.dot(_bf(p1), _bf(v1_ref[0]), preferred_element_type=F32)
        if two:
            p2 = jnp.exp2(s2 - m)
            l = l + jnp.sum(p2, axis=-1, keepdims=True)
            o = o + jnp.dot(_bf(p2), _bf(v2_ref[0]), preferred_element_type=F32)
        o_ref[0, rows, :] = (o / l).astype(o_ref.dtype)


def _attention(q, qcol, k1, k1col, v1, v1col, k2=None, k2col=0, v2=None, v2col=0, *, heads, dq, dv, scale):
    B, Tq, _ = q.shape
    tq = _tile(Tq, 512)
    two = k2 is not None

    def spec(arr, col, w, tiled):
        n = arr.shape[1]
        if tiled:
            return pl.BlockSpec((1, tq, w), lambda b, h, t: (b, t, col // w + h))
        return pl.BlockSpec((1, n, w), lambda b, h, t: (b, 0, col // w + h))

    args = [q, k1, v1]
    specs = [spec(q, qcol, dq, True), spec(k1, k1col, dq, False), spec(v1, v1col, dv, False)]
    if two:
        args += [k2, v2]
        specs += [spec(k2, k2col, dq, False), spec(v2, v2col, dv, False)]
    return pl.pallas_call(
        functools.partial(_attn_kernel, scale=scale, two=two),
        grid=(B, heads, Tq // tq),
        in_specs=specs,
        out_specs=pl.BlockSpec((1, tq, dv), lambda b, h, t: (b, t, h)),
        out_shape=jax.ShapeDtypeStruct((B, Tq, heads * dv), BF16),
        compiler_params=_params("arbitrary", "arbitrary", "arbitrary"),
        name="attention",
    )(*args)


def _mla_prep_kernel(z_ref, zr_ref, c_ref, s_ref, gq_ref, gkv_ref, wq_ref, wkv_ref, q_out, k_out, v_out):
    z = z_ref[0]

    def rms(x, g):
        return x * lax.rsqrt(jnp.mean(x * x, axis=-1, keepdims=True) + EPS) * g

    q = jnp.dot(_bf(rms(z[:, :MLA_Q_RANK], gq_ref[...])), wq_ref[...], preferred_element_type=F32)
    kv = jnp.dot(_bf(rms(z[:, MLA_Q_RANK:], gkv_ref[...])), wkv_ref[...], preferred_element_type=F32)
    cos, sin = c_ref[...], s_ref[...]

    def rope(x):
        return x * cos + pltpu.roll(x, MLA_ROPE, 1) * sin

    kr = rope(zr_ref[0]).astype(k_out.dtype)
    hv = MLA_HEADS * MLA_NOPE
    for h in range(MLA_HEADS):
        lo = h * MLA_DQP
        q_out[0, :, lo:lo + MLA_NOPE] = q[:, lo:lo + MLA_NOPE].astype(q_out.dtype)
        q_out[0, :, lo + MLA_NOPE:lo + MLA_DQP] = rope(q[:, lo + MLA_NOPE:lo + MLA_DQP]).astype(q_out.dtype)
        k_out[0, :, lo:lo + MLA_NOPE] = kv[:, h * MLA_NOPE:(h + 1) * MLA_NOPE].astype(k_out.dtype)
        k_out[0, :, lo + MLA_NOPE:lo + MLA_DQP] = kr
    v_out[0] = kv[:, hv:].astype(v_out.dtype)


def _mla_prep(z, cos_t, sin_t, gq, gkv, wq_p, wkv_p):
    B, T, _ = z.shape
    tt = _tile(T, 256)
    wq_w = MLA_HEADS * MLA_DQP
    wkv_w = MLA_HEADS * (MLA_NOPE + MLA_DV)
    cw = MLA_Q_RANK + MLA_KV_RANK
    tab = pl.BlockSpec((tt, LANE), lambda b, t: (t, 0))
    return pl.pallas_call(
        _mla_prep_kernel,
        grid=(B, T // tt),
        in_specs=[
            pl.BlockSpec((1, tt, cw), lambda b, t: (b, t, COL_MQKV // cw)),
            pl.BlockSpec((1, tt, LANE), lambda b, t: (b, t, COL_KR // LANE)),
            tab, tab,
            pl.BlockSpec((1, MLA_Q_RANK), lambda b, t: (0, 0)),
            pl.BlockSpec((1, MLA_KV_RANK), lambda b, t: (0, 0)),
            pl.BlockSpec((MLA_Q_RANK, wq_w), lambda b, t: (0, 0)),
            pl.BlockSpec((MLA_KV_RANK, wkv_w), lambda b, t: (0, 0)),
        ],
        out_specs=[
            pl.BlockSpec((1, tt, wq_w), lambda b, t: (b, t, 0)),
            pl.BlockSpec((1, tt, wq_w), lambda b, t: (b, t, 0)),
            pl.BlockSpec((1, tt, MLA_HEADS * MLA_DV), lambda b, t: (b, t, 0)),
        ],
        out_shape=[
            jax.ShapeDtypeStruct((B, T, wq_w), BF16),
            jax.ShapeDtypeStruct((B, T, wq_w), BF16),
            jax.ShapeDtypeStruct((B, T, MLA_HEADS * MLA_DV), BF16),
        ],
        compiler_params=_params("arbitrary", "arbitrary"),
        name="mla_prep",
    )(z, z, cos_t, sin_t, gq.reshape(1, -1), gkv.reshape(1, -1), wq_p, wkv_p)


def _mla_weights(w_uq, w_ukv):
    half = MLA_ROPE // 2
    wq = w_uq.reshape(MLA_Q_RANK, MLA_HEADS, MLA_NOPE + MLA_ROPE)
    pe = wq[..., MLA_NOPE:]
    pe_sw = jnp.concatenate([pe[..., half:], pe[..., :half]], axis=-1)
    wq_p = jnp.concatenate([wq[..., :MLA_NOPE], pe, pe_sw], axis=-1).reshape(MLA_Q_RANK, MLA_HEADS * MLA_DQP)
    wkv = w_ukv.reshape(MLA_KV_RANK, MLA_HEADS, MLA_NOPE + MLA_DV)
    wkv_p = jnp.concatenate([wkv[..., :MLA_NOPE].reshape(MLA_KV_RANK, -1),
                             wkv[..., MLA_NOPE:].reshape(MLA_KV_RANK, -1)], axis=-1)
    return _bf(wq_p), _bf(wkv_p)


def _rope_tables(n_tok, with_pos):
    n_freq = MLA_ROPE // 4
    zeros = jnp.zeros((n_tok, LANE - MLA_ROPE), F32)
    if not with_pos:
        return (jnp.concatenate([jnp.ones((n_tok, MLA_ROPE), F32), zeros], axis=1),
                jnp.zeros((n_tok, LANE), F32))
    t = jnp.arange(n_tok, dtype=jnp.int32)
    row = (t // GRID_W).astype(F32)
    col = (t % GRID_W).astype(F32)
    inv = ROPE_THETA ** (-jnp.arange(n_freq, dtype=F32) / n_freq)
    ang = jnp.concatenate([row[:, None] * inv, col[:, None] * inv], axis=-1)
    cos, sin = jnp.cos(ang), jnp.sin(ang)
    return (jnp.concatenate([cos, cos, zeros], axis=1), jnp.concatenate([-sin, sin, zeros], axis=1))


def _na_kernel(q_ref, k_ref, v_ref, kc_ref, vc_ref, bm_ref, o_ref, *, scale, rows):
    nblk = rows // NA_QROWS
    nq = NA_QROWS * GRID_W
    nkw = NA_KROWS * GRID_W
    kc = _bf(kc_ref[0])
    vc = _bf(vc_ref[0])

    def body(blk, carry):
        q0 = pl.multiple_of(blk * nq, nq)
        kb = jnp.clip(NA_QROWS * blk - NA_WIN_R // 2, 0, rows - NA_KROWS)
        k0 = pl.multiple_of(kb * GRID_W, (NA_WIN_R // 2) * GRID_W)
        pat = jnp.where(blk == 0, 0, jnp.where(blk == nblk - 1, 2, 1))
        q = _bf(q_ref[0, pl.ds(q0, nq), :] * scale)
        kw = _bf(k_ref[0, pl.ds(k0, nkw), :])
        vw = _bf(v_ref[0, pl.ds(k0, nkw), :])
        bm = bm_ref[0, pat]
        s = lax.dot_general(q, kw, NT_DIMS, preferred_element_type=F32)
        s = jnp.where(bm > 0.5 * NEG_INF, s + bm, NEG_INF)
        sc = lax.dot_general(q, kc, NT_DIMS, preferred_element_type=F32)
        m = jnp.maximum(jnp.max(s, axis=-1, keepdims=True), jnp.max(sc, axis=-1, keepdims=True))
        p = jnp.exp(s - m)
        pc = jnp.exp(sc - m)
        l = jnp.sum(p, axis=-1, keepdims=True) + jnp.sum(pc, axis=-1, keepdims=True)
        o = jnp.dot(_bf(p), vw, preferred_element_type=F32) + jnp.dot(_bf(pc), vc, preferred_element_type=F32)
        o_ref[0, pl.ds(q0, nq), :] = (o / l).astype(o_ref.dtype)
        return carry

    lax.fori_loop(0, nblk, body, 0, unroll=2)


def _na_bias_table(rpb, rows):
    nblk = rows // NA_QROWS
    col = np.arange(GRID_W)
    c_start = np.clip(col - NA_WIN_C // 2, 0, GRID_W - NA_WIN_C)
    in_win = (col[None, :] >= c_start[:, None]) & (col[None, :] < c_start[:, None] + NA_WIN_C)
    dc = np.clip(col[None, :] - col[:, None] + NA_WIN_C - 1, 0, 2 * NA_WIN_C - 2)
    dc_onehot = (dc[:, :, None] == np.arange(2 * NA_WIN_C - 1)).astype(np.float32)
    toeplitz = jnp.einsum("hab,qkb->haqk", rpb.astype(F32), dc_onehot, precision=HIGHEST)
    n_dr = 2 * NA_WIN_R - 1
    sel = np.zeros((3, NA_QROWS, NA_KROWS, n_dr), np.float32)
    valid = np.zeros((3, NA_QROWS, NA_KROWS), bool)
    for p, blk in enumerate((0, min(1, nblk - 1), nblk - 1)):
        kb = int(np.clip(NA_QROWS * blk - NA_WIN_R // 2, 0, rows - NA_KROWS))
        for rq in range(NA_QROWS):
            r = NA_QROWS * blk + rq
            r_start = int(np.clip(r - NA_WIN_R // 2, 0, rows - NA_WIN_R))
            for rk in range(NA_KROWS):
                kr = kb + rk
                if r_start <= kr < r_start + NA_WIN_R:
                    sel[p, rq, rk, kr - r + NA_WIN_R - 1] = 1.0
                    valid[p, rq, rk] = True
    bias = jnp.einsum("prka,haqc->hprqkc", sel, toeplitz, precision=HIGHEST)
    mask = valid[None, :, :, None, :, None] & in_win[None, None, None, :, None, :]
    H = rpb.shape[0]
    return jnp.where(mask, bias, NEG_INF).reshape(H, 3, NA_QROWS * GRID_W, NA_KROWS * GRID_W)


def _na(z, zc, rpb):
    B, T, _ = z.shape
    Tc = zc.shape[1]
    rows = T // GRID_W
    assert rows % NA_QROWS == 0 and rows >= NA_KROWS
    bm = _na_bias_table(rpb, rows)
    dh = NA_DH

    def seq(n, col):
        return pl.BlockSpec((1, n, dh), lambda b, h: (b, 0, col // dh + h))

    return pl.pallas_call(
        functools.partial(_na_kernel, scale=dh ** -0.5, rows=rows),
        grid=(B, NA_HEADS),
        in_specs=[seq(T, COL_NQ), seq(T, COL_NK), seq(T, COL_NV), seq(Tc, COL_NK), seq(Tc, COL_NV),
                  pl.BlockSpec((1,) + bm.shape[1:], lambda b, h: (h, 0, 0, 0))],
        out_specs=pl.BlockSpec((1, T, dh), lambda b, h: (b, 0, h)),
        out_shape=jax.ShapeDtypeStruct((B, T, NA_HEADS * dh), BF16),
        compiler_params=_params("arbitrary", "arbitrary"),
        name="neighbourhood_attention",
    )(z, z, z, zc, zc, bm)


def _lru_kernel(g_ref, x_ref, gc_ref, xc_ref, cw_ref, cb_ref, wa_ref, ba_ref, wx_ref, bx_ref, lam_ref,
                y_ref, yc_ref, a_scr, u_scr, h_scr):
    row8 = lax.broadcasted_iota(jnp.int32, (SUBLANE, LANE), 0)

    def scan_pair(n_tok, h0f, h0b):
        nb = n_tok // SUBLANE

        def body(i, carry):
            hf, hb = carry
            rf = pl.multiple_of(i * SUBLANE, SUBLANE)
            rb = pl.multiple_of((nb - 1 - i) * SUBLANE, SUBLANE)
            A, U = a_scr[0, pl.ds(rf, SUBLANE), :], u_scr[0, pl.ds(rf, SUBLANE), :]
            Ab, Ub = a_scr[1, pl.ds(rb, SUBLANE), :], u_scr[1, pl.ds(rb, SUBLANE), :]
            for s in (1, 2, 4):
                m = row8 >= s
                U = jnp.where(m, A * pltpu.roll(U, s, 0) + U, U)
                A = jnp.where(m, A * pltpu.roll(A, s, 0), A)
                mb = row8 < SUBLANE - s
                Ub = jnp.where(mb, Ab * pltpu.roll(Ub, SUBLANE - s, 0) + Ub, Ub)
                Ab = jnp.where(mb, Ab * pltpu.roll(Ab, SUBLANE - s, 0), Ab)
            hbf = A * hf + U
            hbb = Ab * hb + Ub
            h_scr[0, pl.ds(rf, SUBLANE), :] = hbf
            h_scr[1, pl.ds(rb, SUBLANE), :] = hbb
            return hbf[SUBLANE - 1:SUBLANE, :], hbb[0:1, :]

        return lax.fori_loop(0, nb, body, (h0f, h0b), unroll=4)

    def run(gate_ref, zx_ref, out_ref, n_tok, h0f, h0b):
        x = zx_ref[0]
        t = lax.broadcasted_iota(jnp.int32, x.shape, 0)
        w = cw_ref[...]
        xc = (w[0:1] * jnp.where(t >= 2, pltpu.roll(x, 2, 0), 0.0)
              + w[1:2] * jnp.where(t >= 1, pltpu.roll(x, 1, 0), 0.0)
              + w[2:3] * x
              + w[3:4] * jnp.where(t < n_tok - 1, pltpu.roll(x, n_tok - 1, 0), 0.0)) + cb_ref[...]
        xcb = _bf(xc)
        for d in range(2):
            r = jax.nn.sigmoid(jnp.dot(xcb, wa_ref[d, 0], preferred_element_type=F32) + ba_ref[d:d + 1, :])
            i = jax.nn.sigmoid(jnp.dot(xcb, wx_ref[d, 0], preferred_element_type=F32) + bx_ref[d:d + 1, :])
            nl = -lam_ref[d:d + 1, :]
            softplus = jnp.maximum(nl, 0.0) + jnp.log1p(jnp.exp(-jnp.abs(nl)))
            log_a = -LRU_C * r * softplus
            a = jnp.exp(log_a)
            a_scr[d, 0:n_tok, :] = a
            u_scr[d, 0:n_tok, :] = jnp.sqrt(1.0 - jnp.exp(2.0 * log_a)) * i * xc
        hf, hb = scan_pair(n_tok, h0f, h0b)
        h = h_scr[0, 0:n_tok, :] + h_scr[1, 0:n_tok, :]
        out_ref[0] = (_gelu_tanh(gate_ref[0]) * h).astype(out_ref.dtype)
        return hf, hb

    zero = jnp.zeros((1, LANE), F32)
    hf, hb = run(gc_ref, xc_ref, yc_ref, xc_ref.shape[1], zero, zero)
    run(g_ref, x_ref, y_ref, x_ref.shape[1], hf, hb)


def _lru(z, zc, conv_w, conv_b, wa, ba, wx, bx, lam):
    B, T, _ = z.shape
    Tc = zc.shape[1]
    bw = LRU_BW

    def seq(n, col):
        return pl.BlockSpec((1, n, bw), lambda b, j: (b, 0, col // bw + j))

    vec2 = pl.BlockSpec((2, bw), lambda b, j: (0, j))
    mat = pl.BlockSpec((2, 1, bw, bw), lambda b, j: (0, j, 0, 0))
    out = lambda n: pl.BlockSpec((1, n, bw), lambda b, j: (b, 0, j))
    return pl.pallas_call(
        _lru_kernel,
        grid=(B, LRU_BLOCKS),
        in_specs=[seq(T, COL_LG), seq(T, COL_LX), seq(Tc, COL_LG), seq(Tc, COL_LX),
                  pl.BlockSpec((CONV_W, bw), lambda b, j: (0, j)), pl.BlockSpec((1, bw), lambda b, j: (0, j)),
                  mat, vec2, mat, vec2, vec2],
        out_specs=[out(T), out(Tc)],
        out_shape=[jax.ShapeDtypeStruct((B, T, LRU_WIDTH), BF16), jax.ShapeDtypeStruct((B, Tc, LRU_WIDTH), BF16)],
        scratch_shapes=[pltpu.VMEM((2, T, bw), F32)] * 3,
        compiler_params=_params("arbitrary", "arbitrary"),
        name="rg_lru",
    )(z, z, zc, zc, conv_w, conv_b.reshape(1, -1), _bf(wa), ba, _bf(wx), bx, lam)


def _gla_kernel(q_ref, k_ref, v_ref, g_ref, d_ref, qc_ref, kc_ref, vc_ref, gc_ref, dc_ref,
                wa_ref, ba_ref, ng_ref, y_ref, yc_ref, b_scr, s_scr):
    C = GLA_CHUNK
    ri = lax.broadcasted_iota(jnp.int32, (C, C), 0)
    ci = lax.broadcasted_iota(jnp.int32, (C, C), 1)
    keep = (ri >= ci, ri <= ci)
    tri = (keep[0].astype(F32), keep[1].astype(F32))
    scale = GLA_DK ** -0.5

    def chunk_rows(c):
        return pl.ds(pl.multiple_of(c * C, C), C)

    def run(refs, gate_ref, dec_ref, out_ref, n_tok, st_f, st_b):
        qr, kr, vr = refs
        n = n_tok // C
        dec = dec_ref[0]
        for d in range(2):
            x = jnp.dot(dec, wa_ref[d], preferred_element_type=F32, precision=HIGHEST) + ba_ref[d:d + 1, :]
            log_sig = jnp.minimum(x, 0.0) - jnp.log1p(jnp.exp(-jnp.abs(x)))
            b_scr[d, 0:n_tok, :] = log_sig * (1.0 / GLA_TAU)

        def cumulate(c, carry):
            rows = chunk_rows(c)
            for d in range(2):
                b_scr[d, rows, :] = jnp.dot(tri[d], b_scr[d, rows, :], preferred_element_type=F32,
                                            precision=HIGHEST)
            return carry

        lax.fori_loop(0, n, cumulate, 0, unroll=4)

        def advance(i, states):
            out = []
            for d, st in enumerate(states):
                c = i if d == 0 else n - 1 - i
                rows = chunk_rows(c)
                b = b_scr[d, rows, :]
                b_tot = b[C - 1:C, :] if d == 0 else b[0:1, :]
                ke = _bf(kr[0, rows, :] * jnp.exp(b_tot - b))
                s_scr[d, c] = st.astype(s_scr.dtype)
                out.append(st * jnp.exp(b_tot)
                           + lax.dot_general(_bf(vr[0, rows, :]), ke, TN_DIMS, preferred_element_type=F32))
            return tuple(out)

        st_f, st_b = lax.fori_loop(0, n, advance, (st_f, st_b), unroll=4)

        def emit(c, carry):
            rows = chunk_rows(c)
            q = qr[0, rows, :] * scale
            k = kr[0, rows, :]
            v = _bf(vr[0, rows, :])
            o = None
            for d in range(2):
                b = b_scr[d, rows, :]
                qd = _bf(q * jnp.exp(b))
                att = lax.dot_general(qd, _bf(k * jnp.exp(-b)), NT_DIMS, preferred_element_type=F32)
                att = jnp.where(keep[d], att, 0.0)
                od = (jnp.dot(_bf(att), v, preferred_element_type=F32)
                      + lax.dot_general(qd, s_scr[d, c], NT_DIMS, preferred_element_type=F32))
                o = od if o is None else o + od
            o = o * lax.rsqrt(jnp.mean(o * o, axis=-1, keepdims=True) + EPS) * ng_ref[...]
            gate = gate_ref[0, rows, :]
            out_ref[0, rows, :] = (o * (gate * jax.nn.sigmoid(gate))).astype(out_ref.dtype)
            return carry

        lax.fori_loop(0, n, emit, 0, unroll=2)
        return st_f, st_b

    zero = jnp.zeros((GLA_DV, GLA_DK), F32)
    st_f, st_b = run((qc_ref, kc_ref, vc_ref), gc_ref, dc_ref, yc_ref, qc_ref.shape[1], zero, zero)
    run((q_ref, k_ref, v_ref), g_ref, d_ref, y_ref, q_ref.shape[1], st_f, st_b)


def _gla(z, zc, wa2, ba, norm_g):
    B, T, _ = z.shape
    Tc = zc.shape[1]
    wa_p = jnp.zeros((2, LANE, GLA_HEADS * GLA_DK), F32)
    wa_p = wa_p.at[0, :GLA_RANK].set(wa2[0]).at[1, GLA_RANK:2 * GLA_RANK].set(wa2[1])

    def seq(n, col, w):
        return pl.BlockSpec((1, n, w), lambda b, h: (b, 0, col // w + h), pipeline_mode=pl.Buffered(1))

    def dec(n):
        return pl.BlockSpec((1, n, LANE), lambda b, h: (b, 0, COL_DEC // LANE), pipeline_mode=pl.Buffered(1))

    def ins(n):
        return [seq(n, COL_GQ, GLA_DK), seq(n, COL_GK, GLA_DK), seq(n, COL_GV, GLA_DV), seq(n, COL_GG, GLA_DV), dec(n)]

    out = lambda n: pl.BlockSpec((1, n, GLA_DV), lambda b, h: (b, 0, h))
    return pl.pallas_call(
        _gla_kernel,
        grid=(B, GLA_HEADS),
        in_specs=ins(T) + ins(Tc) + [
            pl.BlockSpec((2, LANE, GLA_DK), lambda b, h: (0, 0, h)),
            pl.BlockSpec((2, GLA_DK), lambda b, h: (0, h)),
            pl.BlockSpec((1, GLA_DV), lambda b, h: (0, 0)),
        ],
        out_specs=[out(T), out(Tc)],
        out_shape=[jax.ShapeDtypeStruct((B, T, BRANCH_W), BF16), jax.ShapeDtypeStruct((B, Tc, BRANCH_W), BF16)],
        scratch_shapes=[pltpu.VMEM((2, T, GLA_DK), F32), pltpu.VMEM((2, T // GLA_CHUNK, GLA_DV, GLA_DK), BF16)],
        compiler_params=_params("arbitrary", "arbitrary"),
        name="gla",
    )(z, z, z, z, z, zc, zc, zc, zc, zc, wa_p, ba, norm_g.reshape(1, -1))


def _permute_mix_weight(w):
    o = np.cumsum((0,) + MIX_SIZES)
    gq, gk, gv, gg, af, ab, cq, ckv, kr, lg, lx, nq, nk, nv = [w[:, o[i]:o[i + 1]] for i in range(len(MIX_SIZES))]
    half = MLA_ROPE // 2
    kr_sw = jnp.concatenate([kr[:, half:], kr[:, :half]], axis=1)
    pad = jnp.zeros((w.shape[0], MIXP - (COL_DEC + 2 * GLA_RANK)), w.dtype)
    return _bf(jnp.concatenate([gv, gg, lg, lx, nq, nk, nv, gq, gk, cq, ckv, kr, kr_sw, af, ab, pad], axis=1))


def kernel(x, c, ctx, c_ctx, w_ada, b_ada, norm1_g, norm2_g, w_in, gla_wa2, gla_ba, gla_norm_g, mla_q_norm_g, mla_w_uq, mla_kv_norm_g, mla_w_ukv, lru_conv_w, lru_conv_b, lru_wa, lru_ba, lru_wx, lru_bx, lru_lambda, na_rpb, w_branch, w_out, peer_wq, peer_keys, peer_u, peer_v, final_norm_g):
    B, T, D = x.shape
    Tc = ctx.shape[1]
    L = w_ada.shape[0]
    rope_l = _rope_tables(T, True)
    rope_c = _rope_tables(Tc, False)

    n_rows = -(-(B + 1) // SUBLANE) * SUBLANE
    cc = jnp.zeros((n_rows, D), F32).at[:B].set(c).at[B].set(c_ctx)
    mods = _ada(cc, w_ada, b_ada)

    xc = ctx
    pe_l = pe_c = gate_l = gate_c = None
    pu = _bf(peer_u)
    pv = _bf(peer_v)
    for l in range(L):
        update_ctx = l < L - 1
        ml = jnp.split(mods[l, :B], 6, axis=-1)
        mc = jnp.split(jnp.broadcast_to(mods[l, B], (B, 6 * D)), 6, axis=-1)
        w_mix = _permute_mix_weight(w_in[l, :, :MIX_COLS])
        w_gate = _bf(w_in[l, :, MIX_COLS:])
        wb = _bf(w_branch[l])
        wo = _bf(w_out[l])
        wq = _bf(peer_wq[l])
        mla_wq, mla_wkv = _mla_weights(mla_w_uq[l], mla_w_ukv[l])

        xn, h = _norm(x, norm1_g[l], delta=pe_l, gate=gate_l, shift=ml[0], scale=ml[1])
        x = x if xn is None else xn
        xcn, hc = _norm(xc, norm1_g[l], delta=pe_c, gate=gate_c, shift=mc[0], scale=mc[1])
        xc = xc if xcn is None else xcn

        h2d = h.reshape(B * T, D)
        hc2d = hc.reshape(B * Tc, D)
        z = _matmul(h2d, w_mix, out_dtype=F32).reshape(B, T, MIXP)
        zc = _matmul(hc2d, w_mix, out_dtype=F32).reshape(B, Tc, MIXP)

        y_gla, yc_gla = _gla(z, zc, gla_wa2[l], gla_ba[l], gla_norm_g[l])
        y_lru, yc_lru = _lru(z, zc, lru_conv_w[l], lru_conv_b[l], lru_wa[l], lru_ba[l], lru_wx[l], lru_bx[l],
                             lru_lambda[l])
        ql, kl, vl = _mla_prep(z, *rope_l, mla_q_norm_g[l], mla_kv_norm_g[l], mla_wq, mla_wkv)
        qc, kc, vc = _mla_prep(zc, *rope_c, mla_q_norm_g[l], mla_kv_norm_g[l], mla_wq, mla_wkv)
        mla_args = dict(heads=MLA_HEADS, dq=MLA_DQP, dv=MLA_DV, scale=(MLA_NOPE + MLA_ROPE) ** -0.5)
        y_mla = _attention(ql, 0, kc, 0, vc, 0, kl, 0, vl, 0, **mla_args)
        y_na = _na(z, zc, na_rpb[l])
        ys_l = [y_gla, y_mla, y_lru, y_na]
        if update_ctx:
            yc_mla = _attention(qc, 0, kc, 0, vc, 0, **mla_args)
            yc_na = _attention(zc, COL_NQ, zc, COL_NK, zc, COL_NV, heads=NA_HEADS, dq=NA_DH, dv=NA_DH,
                               scale=NA_DH ** -0.5)
            ys_c = [yc_gla, yc_mla, yc_lru, yc_na]

        def channel_mix(xs, hs, ys_s, m, n_tok):
            M = B * n_tok
            gates = _matmul(hs, w_gate, out_dtype=BF16, act="sigmoid")
            mrg = _merge([y.reshape(M, BRANCH_W) for y in ys_s], gates, wb)
            xs = _matmul(mrg, wo, out_dtype=F32, res=xs.reshape(M, D), mod=m[2],
                         rows_per_batch=n_tok).reshape(B, n_tok, D)
            _, h2 = _norm(xs, norm2_g[l], shift=m[3], scale=m[4])
            pe = _peer(h2.reshape(M, D), wq, peer_keys[l], pu, pv, l).reshape(B, n_tok, D)
            return xs, pe

        x, pe_l = channel_mix(x, h2d, ys_l, ml, T)
        gate_l = ml[5]
        if update_ctx:
            xc, pe_c = channel_mix(xc, hc2d, ys_c, mc, Tc)
            gate_c = mc[5]
        else:
            pe_c = gate_c = None
    _, out = _norm(x, final_norm_g, delta=pe_l, gate=gate_l, out_dtype=F32, emit_x=False)
    return out
```

```python
import functools

import jax
import jax.numpy as jnp
import numpy as np
from jax import lax
from jax.experimental import pallas as pl
from jax.experimental.pallas import tpu as pltpu

GRID_W = 64
EPS = 1e-6
NEG_INF = -1e30
N_BRANCH = 4
BRANCH_W = 1024
ROPE_THETA = 10000.0

GLA_HEADS = 4
GLA_DK = 128
GLA_DV = BRANCH_W // GLA_HEADS
GLA_RANK = 16
GLA_TAU = 16.0
GLA_CHUNK = 64

MLA_HEADS = 8
MLA_Q_RANK = 768
MLA_KV_RANK = 256
MLA_NOPE = 128
MLA_ROPE = 64
MLA_DV = BRANCH_W // MLA_HEADS
MLA_DQP = 256

LRU_WIDTH = BRANCH_W
LRU_BLOCKS = 8
LRU_BW = LRU_WIDTH // LRU_BLOCKS
CONV_W = 4
LRU_C = 8.0

NA_HEADS = 8
NA_DH = BRANCH_W // NA_HEADS
NA_WIN_R = 8
NA_WIN_C = 16
NA_QROWS = 8
NA_KROWS = 16

PEER_HEADS = 8
PEER_DQ = 256
PEER_TOPK = 16

MIX_SIZES = (
    GLA_HEADS * GLA_DK, GLA_HEADS * GLA_DK, GLA_HEADS * GLA_DV, GLA_HEADS * GLA_DV, GLA_RANK, GLA_RANK,
    MLA_Q_RANK, MLA_KV_RANK, MLA_ROPE, LRU_WIDTH, LRU_WIDTH,
    NA_HEADS * NA_DH, NA_HEADS * NA_DH, NA_HEADS * NA_DH,
)
MIX_COLS = sum(MIX_SIZES)

COL_GV, COL_GG, COL_LG, COL_LX, COL_NQ, COL_NK, COL_NV = 0, 1024, 2048, 3072, 4096, 5120, 6144
COL_GQ, COL_GK = 7168, 7680
COL_MQKV = 8192
COL_KR = 9216
COL_DEC = 9344
MIXP = 9728

V7X_VMEM_BYTES = 64 * 1024 * 1024
VMEM_LIMIT = V7X_VMEM_BYTES - 8 * 1024 * 1024
LANE = 128
SUBLANE = 8
ROW_ALIGN = 32

F32 = jnp.float32
BF16 = jnp.bfloat16
HIGHEST = lax.Precision.HIGHEST
LOG2_E = 1.4426950408889634
NT_DIMS = (((1,), (1,)), ((), ()))
TN_DIMS = (((0,), (0,)), ((), ()))


def _tile(n, pref, mult=SUBLANE):
    if n <= pref:
        return n
    t = (pref // mult) * mult
    while t > mult and n % t:
        t -= mult
    assert n % t == 0, (n, pref, mult)
    return t


def _params(*sem):
    return pltpu.CompilerParams(dimension_semantics=sem, vmem_limit_bytes=VMEM_LIMIT)


def _bf(x):
    return x.astype(BF16)


def _ada_kernel(c_ref, w_ref, b_ref, o_ref):
    cv = c_ref[...]
    a = cv * jax.nn.sigmoid(cv)
    o_ref[0] = jnp.dot(a, w_ref[0], preferred_element_type=F32, precision=HIGHEST) + b_ref[0]


def _ada(cc, w_ada, b_ada):
    L, D, W = w_ada.shape
    R = cc.shape[0]
    tn = _tile(W, 512, LANE)
    return pl.pallas_call(
        _ada_kernel,
        grid=(L, W // tn),
        in_specs=[
            pl.BlockSpec((R, D), lambda l, j: (0, 0)),
            pl.BlockSpec((1, D, tn), lambda l, j: (l, 0, j)),
            pl.BlockSpec((1, 1, tn), lambda l, j: (l, 0, j)),
        ],
        out_specs=pl.BlockSpec((1, R, tn), lambda l, j: (l, 0, j)),
        out_shape=jax.ShapeDtypeStruct((L, R, W), F32),
        compiler_params=_params("arbitrary", "arbitrary"),
        name="ada_mod",
    )(cc, w_ada, b_ada.reshape(L, 1, W))


def _norm_kernel(*refs, has_delta, modulate, emit_x):
    it = iter(refs)
    x_ref = next(it)
    if has_delta:
        d_ref, gate_ref = next(it), next(it)
    g_ref = next(it)
    if modulate:
        shift_ref, scale_ref = next(it), next(it)
    if emit_x:
        xo_ref = next(it)
    h_ref = next(it)
    x = x_ref[0]
    if has_delta:
        x = x + gate_ref[0] * d_ref[0]
    if emit_x:
        xo_ref[0] = x
    y = x * lax.rsqrt(jnp.mean(x * x, axis=-1, keepdims=True) + EPS)
    y = y * g_ref[...]
    if modulate:
        y = y * (1.0 + scale_ref[0]) + shift_ref[0]
    h_ref[0] = y.astype(h_ref.dtype)


def _norm(x, g, *, delta=None, gate=None, shift=None, scale=None, out_dtype=None, emit_x=True):
    out_dtype = BF16 if out_dtype is None else out_dtype
    B, T, D = x.shape
    tt = _tile(T, 256)
    has_delta = delta is not None
    emit_x = emit_x and has_delta
    modulate = shift is not None
    tok = pl.BlockSpec((1, tt, D), lambda b, t: (b, t, 0))
    vec = pl.BlockSpec((1, 1, D), lambda b, t: (b, 0, 0))
    args, specs = [x], [tok]
    if has_delta:
        args += [delta, gate.reshape(B, 1, D)]
        specs += [tok, vec]
    args.append(g.reshape(1, D))
    specs.append(pl.BlockSpec((1, D), lambda b, t: (0, 0)))
    if modulate:
        args += [shift.reshape(B, 1, D), scale.reshape(B, 1, D)]
        specs += [vec, vec]
    out_shape, out_specs = [], []
    if emit_x:
        out_shape.append(jax.ShapeDtypeStruct((B, T, D), F32))
        out_specs.append(tok)
    out_shape.append(jax.ShapeDtypeStruct((B, T, D), out_dtype))
    out_specs.append(tok)
    outs = pl.pallas_call(
        functools.partial(_norm_kernel, has_delta=has_delta, modulate=modulate, emit_x=emit_x),
        grid=(B, T // tt),
        in_specs=specs,
        out_specs=out_specs,
        out_shape=out_shape,
        compiler_params=_params("arbitrary", "arbitrary"),
        name="res_norm_mod",
    )(*args)
    if emit_x:
        return outs[0], outs[1]
    return None, outs[0]


def _mm_kernel(*refs, act, has_res, b_rows):
    if has_res:
        a_ref, b_ref, r_ref, m_ref, o_ref = refs
    else:
        a_ref, b_ref, o_ref = refs
    if b_rows:
        b = b_ref[0] if len(b_ref.shape) == 3 else b_ref[...]
        acc = lax.dot_general(a_ref[...], b, NT_DIMS, preferred_element_type=F32)
    else:
        acc = jnp.dot(a_ref[...], b_ref[...], preferred_element_type=F32)
    if act == "sigmoid":
        acc = jax.nn.sigmoid(acc)
    if has_res:
        acc = r_ref[...] + m_ref[0] * acc
    o_ref[...] = acc.astype(o_ref.dtype)


def _matmul(a, b, *, out_dtype, act=None, res=None, mod=None, rows_per_batch=None, layer=None, b_rows=False,
            row0=None, n_out=None, tm=1024, tn=1024):
    M, K = a.shape
    N = n_out if row0 is not None else (b.shape[-2] if b_rows else b.shape[-1])
    tm = _tile(rows_per_batch if rows_per_batch else M, tm)
    tn = _tile(N, tn, LANE)
    has_res = res is not None
    args = [a, b]
    blk, at = ((tn, K), lambda j: (j, 0)) if b_rows else ((K, tn), lambda j: (0, j))
    if layer is None:
        b_spec = pl.BlockSpec(blk, lambda i, j: at(j))
    else:
        b_spec = pl.BlockSpec((None,) + blk, lambda i, j: (layer,) + at(j))
    if row0 is not None:
        assert b_rows and layer is not None and row0 % ROW_ALIGN == 0
        b_spec = pl.BlockSpec((pl.Element(1), pl.Element(tn), pl.Element(K)),
                              lambda i, j: (layer, pl.multiple_of(row0 + j * tn, ROW_ALIGN), 0))
    specs = [pl.BlockSpec((tm, K), lambda i, j: (i, 0)), b_spec]
    if has_res:
        args += [res, mod.reshape(mod.shape[0], 1, N)]
        specs += [
            pl.BlockSpec((tm, tn), lambda i, j: (i, j)),
            pl.BlockSpec((1, 1, tn), lambda i, j: ((i * tm) // rows_per_batch, 0, j)),
        ]
    return pl.pallas_call(
        functools.partial(_mm_kernel, act=act, has_res=has_res, b_rows=b_rows),
        grid=(M // tm, N // tn),
        in_specs=specs,
        out_specs=pl.BlockSpec((tm, tn), lambda i, j: (i, j)),
        out_shape=jax.ShapeDtypeStruct((M, N), out_dtype),
        compiler_params=_params("arbitrary", "arbitrary"),
        name="matmul",
    )(*args)


def _merge_kernel(y0, y1, y2, y3, g0, g1, g2, g3, w_ref, o_ref):
    acc = None
    for i, (y, g) in enumerate(((y0, g0), (y1, g1), (y2, g2), (y3, g3))):
        p = g[...].astype(F32) * jnp.dot(y[...], w_ref[i], preferred_element_type=F32)
        acc = p if acc is None else acc + p
    o_ref[...] = acc.astype(o_ref.dtype)


def _merge(ys, gates, w_branch, layer):
    M = ys[0].shape[0]
    D = w_branch.shape[3]
    tm = _tile(M, 1024)
    tn = _tile(D, 512, LANE)
    nj = D // tn
    y_spec = pl.BlockSpec((tm, BRANCH_W), lambda i, j: (i, 0))
    g_specs = [pl.BlockSpec((tm, tn), functools.partial(lambda i, j, br: (i, br * nj + j), br=br))
               for br in range(N_BRANCH)]
    return pl.pallas_call(
        _merge_kernel,
        grid=(M // tm, nj),
        in_specs=[y_spec] * N_BRANCH + g_specs + [
            pl.BlockSpec((None, N_BRANCH, BRANCH_W, tn), lambda i, j: (layer, 0, 0, j))],
        out_specs=pl.BlockSpec((tm, tn), lambda i, j: (i, j)),
        out_shape=jax.ShapeDtypeStruct((M, D), BF16),
        compiler_params=_params("arbitrary", "arbitrary"),
        name="merge",
    )(*ys, gates, gates, gates, gates, w_branch)


def _topk_rows(s, k):
    n = s.shape[0]
    iota = lax.broadcasted_iota(jnp.int32, s.shape, 0).astype(F32)
    vals, idxs = [], []
    for _ in range(k):
        m = jnp.max(s, axis=0, keepdims=True)
        am = jnp.min(jnp.where(s == m, iota, float(n)), axis=0, keepdims=True)
        vals.append(m)
        idxs.append(am)
        s = jnp.where(iota == am, -jnp.inf, s)
    return jnp.concatenate(vals, axis=0), jnp.concatenate(idxs, axis=0)


def _candidate_rows(x1, x2):
    K, m = x1.shape
    r1, r2, ok = [], [], []
    a = 0
    while K // (a + 1) > 1:
        nb = K // (a + 1)
        width = -(-nb // SUBLANE) * SUBLANE
        r1.append(jnp.broadcast_to(x1[a:a + 1], (width, m)))
        r2.append(x2[0:width])
        ok.append(lax.broadcasted_iota(jnp.int32, (width, m), 0) < nb)
        a += 1
    assert (K - a) % SUBLANE == 0
    r1.append(x1[a:K])
    r2.append(jnp.broadcast_to(x2[0:1], (K - a, m)))
    ok.append(jnp.full((K - a, m), True))
    return jnp.concatenate(r1, axis=0), jnp.concatenate(r2, axis=0), jnp.concatenate(ok, axis=0)


def _peer_route_kernel(q_ref, k_ref, g_ref, e1_ref, e2_ref):
    half = PEER_DQ // 2
    K = PEER_TOPK
    for c0 in range(0, q_ref.shape[0], LANE):
        cols = slice(c0, c0 + LANE)
        q = q_ref[cols, :]
        v1, i1 = _topk_rows(lax.dot_general(k_ref[0, 0], q[:, :half], NT_DIMS, preferred_element_type=F32,
                                            precision=HIGHEST), K)
        v2, i2 = _topk_rows(lax.dot_general(k_ref[0, 1], q[:, half:], NT_DIMS, preferred_element_type=F32,
                                            precision=HIGHEST), K)
        c1, c2, ok = _candidate_rows(v1, v2)
        top, pos = _topk_rows(jnp.where(ok, c1 + c2, -jnp.inf), K)
        p = jnp.exp(top - top[0:1])
        g_ref[:, cols] = p / jnp.sum(p, axis=0, keepdims=True)
        id1, id2, _ = _candidate_rows(i1, i2)
        row = lax.broadcasted_iota(jnp.int32, id1.shape, 0).astype(F32)
        e1, e2 = [], []
        for r in range(K):
            sel = row == pos[r:r + 1]
            e1.append(jnp.sum(jnp.where(sel, id1, 0.0), axis=0, keepdims=True))
            e2.append(jnp.sum(jnp.where(sel, id2, 0.0), axis=0, keepdims=True))
        e1_ref[:, cols] = jnp.concatenate(e1, axis=0)
        e2_ref[:, cols] = jnp.concatenate(e2, axis=0)


def _peer_route(q, keys):
    M = q.shape[0]
    H, _, nk, half = keys.shape
    tm = _tile(M, 256, LANE)
    out = jax.ShapeDtypeStruct((H * PEER_TOPK, M), F32)
    o_spec = pl.BlockSpec((PEER_TOPK, tm), lambda i, h: (h, i))
    return pl.pallas_call(
        _peer_route_kernel,
        grid=(M // tm, H),
        in_specs=[
            pl.BlockSpec((tm, PEER_DQ), lambda i, h: (i, h)),
            pl.BlockSpec((1, 2, nk, half), lambda i, h: (h, 0, 0, 0)),
        ],
        out_specs=[o_spec, o_spec, o_spec],
        out_shape=[out, out, out],
        compiler_params=_params("arbitrary", "arbitrary"),
        name="peer_route",
    )(q, keys)


def _peer_w_kernel(g_ref, e1_ref, e2_ref, o_ref, gt_ref, e1t_ref, e2t_ref, w_scr, *, nk):
    tm = o_ref.shape[0]
    gt_ref[...] = g_ref[...].T
    e1t_ref[...] = e1_ref[...].T
    e2t_ref[...] = e2_ref[...].T
    key_iota = lax.broadcasted_iota(jnp.int32, (nk, g_ref.shape[0]), 0).astype(F32)

    def body(t, carry):
        row = pl.ds(t, 1)
        a = jnp.where(e1t_ref[row, :] == key_iota, gt_ref[row, :], 0.0).astype(BF16)
        b = jnp.where(e2t_ref[row, :] == key_iota, 1.0, 0.0).astype(BF16)
        w_scr[t] = lax.dot_general(a, b, NT_DIMS, preferred_element_type=F32)
        return carry

    lax.fori_loop(0, tm, body, 0, unroll=32)
    tb = _tile(tm, 64)
    for t0 in range(0, tm, tb):
        planes = jnp.swapaxes(w_scr[t0:t0 + tb], 0, 1)
        for j in range(nk):
            o_ref[t0:t0 + tb, j * nk:(j + 1) * nk] = planes[j].astype(o_ref.dtype)


def _peer_w(g, e1, e2, nk):
    S, M = g.shape
    tm = _tile(M, 256, LANE)
    spec = pl.BlockSpec((S, tm), lambda i: (0, i))
    return pl.pallas_call(
        functools.partial(_peer_w_kernel, nk=nk),
        grid=(M // tm,),
        in_specs=[spec, spec, spec],
        out_specs=pl.BlockSpec((tm, nk * nk), lambda i: (i, 0)),
        out_shape=jax.ShapeDtypeStruct((M, nk * nk), BF16),
        scratch_shapes=[pltpu.VMEM((tm, S), F32)] * 3 + [pltpu.VMEM((tm, nk, nk), F32)],
        compiler_params=_params("arbitrary"),
        name="peer_route_weights",
    )(g, e1, e2)


def _gelu_tanh(x):
    return 0.5 * x * (1.0 + jnp.tanh(0.7978845608028654 * (x + 0.044715 * (x * x * x))))


def _peer_dense_kernel(h_ref, u_ref, v_ref, w_ref, o_ref):
    @pl.when(pl.program_id(1) == 0)
    def _():
        o_ref[...] = jnp.zeros_like(o_ref)

    s = lax.dot_general(h_ref[...], u_ref[...], NT_DIMS, preferred_element_type=F32)
    a = (_gelu_tanh(s) * w_ref[...].astype(F32)).astype(BF16)
    o_ref[...] += jnp.dot(a, v_ref[...], preferred_element_type=F32)


def _peer_dense(h2, u, v, w, layer):
    M, D = h2.shape
    E = u.shape[1]
    tm = _tile(M, 1024)
    te = _tile(E, 512, LANE)
    once = pl.Buffered(1)
    return pl.pallas_call(
        _peer_dense_kernel,
        grid=(M // tm, E // te),
        in_specs=[
            pl.BlockSpec((tm, D), lambda i, e: (i, 0), pipeline_mode=once),
            pl.BlockSpec((None, te, D), lambda i, e: (layer, e, 0)),
            pl.BlockSpec((None, te, D), lambda i, e: (layer, e, 0)),
            pl.BlockSpec((tm, te), lambda i, e: (i, e)),
        ],
        out_specs=pl.BlockSpec((tm, D), lambda i, e: (i, 0), pipeline_mode=once),
        out_shape=jax.ShapeDtypeStruct((M, D), F32),
        compiler_params=_params("arbitrary", "arbitrary"),
        name="peer_dense",
    )(h2, u, v, w)


def _peer(h2, wq, keys, u, v, layer):
    nk = keys.shape[2]
    q = _matmul(h2, wq, out_dtype=F32, layer=layer)
    g, e1, e2 = _peer_route(q, keys)
    return _peer_dense(h2, u, v, _peer_w(g, e1, e2, nk), layer)


def _attn_kernel(*refs, scale, two):
    if two:
        q_ref, k1_ref, v1_ref, k2_ref, v2_ref, o_ref = refs
    else:
        q_ref, k1_ref, v1_ref, o_ref = refs
    tq = q_ref.shape[1]
    tg = _tile(tq, 256)
    for r0 in range(0, tq, tg):
        rows = slice(r0, r0 + tg)
        q = _bf(q_ref[0, rows, :].astype(F32) * (scale * LOG2_E))
        s1 = lax.dot_general(q, _bf(k1_ref[0]), NT_DIMS, preferred_element_type=F32)
        m = jnp.max(s1, axis=-1, keepdims=True)
        if two:
            s2 = lax.dot_general(q, _bf(k2_ref[0]), NT_DIMS, preferred_element_type=F32)
            m = jnp.maximum(m, jnp.max(s2, axis=-1, keepdims=True))
        p1 = jnp.exp2(s1 - m)
        l = jnp.sum(p1, axis=-1, keepdims=True)
        o = jnp.dot(_bf(p1), _bf(v1_ref[0]), preferred_element_type=F32)
        if two:
            p2 = jnp.exp2(s2 - m)
            l = l + jnp.sum(p2, axis=-1, keepdims=True)
            o = o + jnp.dot(_bf(p2), _bf(v2_ref[0]), preferred_element_type=F32)
        o_ref[0, rows, :] = (o / l).astype(o_ref.dtype)


def _attention(q, qcol, k1, k1col, v1, v1col, k2=None, k2col=0, v2=None, v2col=0, *, heads, dq, dv, scale):
    B, Tq, _ = q.shape
    tq = _tile(Tq, 512)
    two = k2 is not None

    def spec(arr, col, w, tiled):
        n = arr.shape[1]
        if tiled:
            return pl.BlockSpec((1, tq, w), lambda b, h, t: (b, t, col // w + h))
        return pl.BlockSpec((1, n, w), lambda b, h, t: (b, 0, col // w + h))

    args = [q, k1, v1]
    specs = [spec(q, qcol, dq, True), spec(k1, k1col, dq, False), spec(v1, v1col, dv, False)]
    if two:
        args += [k2, v2]
        specs += [spec(k2, k2col, dq, False), spec(v2, v2col, dv, False)]
    return pl.pallas_call(
        functools.partial(_attn_kernel, scale=scale, two=two),
        grid=(B, heads, Tq // tq),
        in_specs=specs,
        out_specs=pl.BlockSpec((1, tq, dv), lambda b, h, t: (b, t, h)),
        out_shape=jax.ShapeDtypeStruct((B, Tq, heads * dv), BF16),
        compiler_params=_params("arbitrary", "arbitrary", "arbitrary"),
        name="attention",
    )(*args)


def _mla_prep_kernel(z_ref, zr_ref, c_ref, s_ref, gq_ref, gkv_ref, wq_ref, wkv_ref, q_out, k_out, v_out):
    z = z_ref[0]

    def rms(x, g):
        return x * lax.rsqrt(jnp.mean(x * x, axis=-1, keepdims=True) + EPS) * g

    q = jnp.dot(_bf(rms(z[:, :MLA_Q_RANK], gq_ref[...])), wq_ref[...], preferred_element_type=F32)
    kv = jnp.dot(_bf(rms(z[:, MLA_Q_RANK:], gkv_ref[...])), wkv_ref[...], preferred_element_type=F32)
    cos, sin = c_ref[...], s_ref[...]

    def rope(x):
        return x * cos + pltpu.roll(x, MLA_ROPE, 1) * sin

    kr = rope(zr_ref[0]).astype(k_out.dtype)
    hv = MLA_HEADS * MLA_NOPE
    for h in range(MLA_HEADS):
        lo = h * MLA_DQP
        q_out[0, :, lo:lo + MLA_NOPE] = q[:, lo:lo + MLA_NOPE].astype(q_out.dtype)
        q_out[0, :, lo + MLA_NOPE:lo + MLA_DQP] = rope(q[:, lo + MLA_NOPE:lo + MLA_DQP]).astype(q_out.dtype)
        k_out[0, :, lo:lo + MLA_NOPE] = kv[:, h * MLA_NOPE:(h + 1) * MLA_NOPE].astype(k_out.dtype)
        k_out[0, :, lo + MLA_NOPE:lo + MLA_DQP] = kr
    v_out[0] = kv[:, hv:].astype(v_out.dtype)


def _mla_prep(z, cos_t, sin_t, gq, gkv, wq_p, wkv_p):
    B, T, _ = z.shape
    tt = _tile(T, 256)
    wq_w = MLA_HEADS * MLA_DQP
    wkv_w = MLA_HEADS * (MLA_NOPE + MLA_DV)
    cw = MLA_Q_RANK + MLA_KV_RANK
    tab = pl.BlockSpec((tt, LANE), lambda b, t: (t, 0))
    return pl.pallas_call(
        _mla_prep_kernel,
        grid=(B, T // tt),
        in_specs=[
            pl.BlockSpec((1, tt, cw), lambda b, t: (b, t, COL_MQKV // cw)),
            pl.BlockSpec((1, tt, LANE), lambda b, t: (b, t, COL_KR // LANE)),
            tab, tab,
            pl.BlockSpec((1, MLA_Q_RANK), lambda b, t: (0, 0)),
            pl.BlockSpec((1, MLA_KV_RANK), lambda b, t: (0, 0)),
            pl.BlockSpec((MLA_Q_RANK, wq_w), lambda b, t: (0, 0)),
            pl.BlockSpec((MLA_KV_RANK, wkv_w), lambda b, t: (0, 0)),
        ],
        out_specs=[
            pl.BlockSpec((1, tt, wq_w), lambda b, t: (b, t, 0)),
            pl.BlockSpec((1, tt, wq_w), lambda b, t: (b, t, 0)),
            pl.BlockSpec((1, tt, MLA_HEADS * MLA_DV), lambda b, t: (b, t, 0)),
        ],
        out_shape=[
            jax.ShapeDtypeStruct((B, T, wq_w), BF16),
            jax.ShapeDtypeStruct((B, T, wq_w), BF16),
            jax.ShapeDtypeStruct((B, T, MLA_HEADS * MLA_DV), BF16),
        ],
        compiler_params=_params("arbitrary", "arbitrary"),
        name="mla_prep",
    )(z, z, cos_t, sin_t, gq.reshape(1, -1), gkv.reshape(1, -1), wq_p, wkv_p)


def _mla_weights(w_uq, w_ukv):
    half = MLA_ROPE // 2
    wq = w_uq.reshape(MLA_Q_RANK, MLA_HEADS, MLA_NOPE + MLA_ROPE)
    pe = wq[..., MLA_NOPE:]
    pe_sw = jnp.concatenate([pe[..., half:], pe[..., :half]], axis=-1)
    wq_p = jnp.concatenate([wq[..., :MLA_NOPE], pe, pe_sw], axis=-1).reshape(MLA_Q_RANK, MLA_HEADS * MLA_DQP)
    wkv = w_ukv.reshape(MLA_KV_RANK, MLA_HEADS, MLA_NOPE + MLA_DV)
    wkv_p = jnp.concatenate([wkv[..., :MLA_NOPE].reshape(MLA_KV_RANK, -1),
                             wkv[..., MLA_NOPE:].reshape(MLA_KV_RANK, -1)], axis=-1)
    return _bf(wq_p), _bf(wkv_p)


def _rope_tables(n_tok, with_pos):
    n_freq = MLA_ROPE // 4
    zeros = jnp.zeros((n_tok, LANE - MLA_ROPE), F32)
    if not with_pos:
        return (jnp.concatenate([jnp.ones((n_tok, MLA_ROPE), F32), zeros], axis=1),
                jnp.zeros((n_tok, LANE), F32))
    t = jnp.arange(n_tok, dtype=jnp.int32)
    row = (t // GRID_W).astype(F32)
    col = (t % GRID_W).astype(F32)
    inv = ROPE_THETA ** (-jnp.arange(n_freq, dtype=F32) / n_freq)
    ang = jnp.concatenate([row[:, None] * inv, col[:, None] * inv], axis=-1)
    cos, sin = jnp.cos(ang), jnp.sin(ang)
    return (jnp.concatenate([cos, cos, zeros], axis=1), jnp.concatenate([-sin, sin, zeros], axis=1))


def _na_kernel(q_ref, k_ref, v_ref, kc_ref, vc_ref, bm_ref, o_ref, *, scale, rows):
    nblk = rows // NA_QROWS
    nq = NA_QROWS * GRID_W
    nkw = NA_KROWS * GRID_W
    kc = _bf(kc_ref[0])
    vc = _bf(vc_ref[0])

    def body(blk, carry):
        q0 = pl.multiple_of(blk * nq, nq)
        kb = jnp.clip(NA_QROWS * blk - NA_WIN_R // 2, 0, rows - NA_KROWS)
        k0 = pl.multiple_of(kb * GRID_W, (NA_WIN_R // 2) * GRID_W)
        pat = jnp.where(blk == 0, 0, jnp.where(blk == nblk - 1, 2, 1))
        q = _bf(q_ref[0, pl.ds(q0, nq), :] * scale)
        kw = _bf(k_ref[0, pl.ds(k0, nkw), :])
        vw = _bf(v_ref[0, pl.ds(k0, nkw), :])
        bm = bm_ref[0, pat]
        s = lax.dot_general(q, kw, NT_DIMS, preferred_element_type=F32)
        s = jnp.where(bm > 0.5 * NEG_INF, s + bm, NEG_INF)
        sc = lax.dot_general(q, kc, NT_DIMS, preferred_element_type=F32)
        m = jnp.maximum(jnp.max(s, axis=-1, keepdims=True), jnp.max(sc, axis=-1, keepdims=True))
        p = jnp.exp(s - m)
        pc = jnp.exp(sc - m)
        l = jnp.sum(p, axis=-1, keepdims=True) + jnp.sum(pc, axis=-1, keepdims=True)
        o = jnp.dot(_bf(p), vw, preferred_element_type=F32) + jnp.dot(_bf(pc), vc, preferred_element_type=F32)
        o_ref[0, pl.ds(q0, nq), :] = (o / l).astype(o_ref.dtype)
        return carry

    lax.fori_loop(0, nblk, body, 0, unroll=2)


def _na_bias_table(rpb, rows):
    nblk = rows // NA_QROWS
    col = np.arange(GRID_W)
    c_start = np.clip(col - NA_WIN_C // 2, 0, GRID_W - NA_WIN_C)
    in_win = (col[None, :] >= c_start[:, None]) & (col[None, :] < c_start[:, None] + NA_WIN_C)
    dc = np.clip(col[None, :] - col[:, None] + NA_WIN_C - 1, 0, 2 * NA_WIN_C - 2)
    dc_onehot = (dc[:, :, None] == np.arange(2 * NA_WIN_C - 1)).astype(np.float32)
    toeplitz = jnp.einsum("hab,qkb->haqk", rpb.astype(F32), dc_onehot, precision=HIGHEST)
    n_dr = 2 * NA_WIN_R - 1
    sel = np.zeros((3, NA_QROWS, NA_KROWS, n_dr), np.float32)
    valid = np.zeros((3, NA_QROWS, NA_KROWS), bool)
    for p, blk in enumerate((0, min(1, nblk - 1), nblk - 1)):
        kb = int(np.clip(NA_QROWS * blk - NA_WIN_R // 2, 0, rows - NA_KROWS))
        for rq in range(NA_QROWS):
            r = NA_QROWS * blk + rq
            r_start = int(np.clip(r - NA_WIN_R // 2, 0, rows - NA_WIN_R))
            for rk in range(NA_KROWS):
                kr = kb + rk
                if r_start <= kr < r_start + NA_WIN_R:
                    sel[p, rq, rk, kr - r + NA_WIN_R - 1] = 1.0
                    valid[p, rq, rk] = True
    bias = jnp.einsum("prka,haqc->hprqkc", sel, toeplitz, precision=HIGHEST)
    mask = valid[None, :, :, None, :, None] & in_win[None, None, None, :, None, :]
    H = rpb.shape[0]
    return jnp.where(mask, bias, NEG_INF).reshape(H, 3, NA_QROWS * GRID_W, NA_KROWS * GRID_W)


def _na(z, zc, rpb):
    B, T, _ = z.shape
    Tc = zc.shape[1]
    rows = T // GRID_W
    assert rows % NA_QROWS == 0 and rows >= NA_KROWS
    bm = _na_bias_table(rpb, rows)
    dh = NA_DH

    def seq(n, col):
        return pl.BlockSpec((1, n, dh), lambda b, h: (b, 0, col // dh + h))

    return pl.pallas_call(
        functools.partial(_na_kernel, scale=dh ** -0.5, rows=rows),
        grid=(B, NA_HEADS),
        in_specs=[seq(T, COL_NQ), seq(T, COL_NK), seq(T, COL_NV), seq(Tc, COL_NK), seq(Tc, COL_NV),
                  pl.BlockSpec((1,) + bm.shape[1:], lambda b, h: (h, 0, 0, 0))],
        out_specs=pl.BlockSpec((1, T, dh), lambda b, h: (b, 0, h)),
        out_shape=jax.ShapeDtypeStruct((B, T, NA_HEADS * dh), BF16),
        compiler_params=_params("arbitrary", "arbitrary"),
        name="neighbourhood_attention",
    )(z, z, z, zc, zc, bm)


def _lru_kernel(g_ref, x_ref, gc_ref, xc_ref, cw_ref, cb_ref, wa_ref, ba_ref, wx_ref, bx_ref, lam_ref,
                y_ref, yc_ref, a_scr, u_scr, h_scr):
    row8 = lax.broadcasted_iota(jnp.int32, (SUBLANE, LANE), 0)

    def scan_pair(n_tok, h0f, h0b):
        nb = n_tok // SUBLANE

        def body(i, carry):
            hf, hb = carry
            rf = pl.multiple_of(i * SUBLANE, SUBLANE)
            rb = pl.multiple_of((nb - 1 - i) * SUBLANE, SUBLANE)
            A, U = a_scr[0, pl.ds(rf, SUBLANE), :], u_scr[0, pl.ds(rf, SUBLANE), :]
            Ab, Ub = a_scr[1, pl.ds(rb, SUBLANE), :], u_scr[1, pl.ds(rb, SUBLANE), :]
            for s in (1, 2, 4):
                m = row8 >= s
                U = jnp.where(m, A * pltpu.roll(U, s, 0) + U, U)
                A = jnp.where(m, A * pltpu.roll(A, s, 0), A)
                mb = row8 < SUBLANE - s
                Ub = jnp.where(mb, Ab * pltpu.roll(Ub, SUBLANE - s, 0) + Ub, Ub)
                Ab = jnp.where(mb, Ab * pltpu.roll(Ab, SUBLANE - s, 0), Ab)
            hbf = A * hf + U
            hbb = Ab * hb + Ub
            h_scr[0, pl.ds(rf, SUBLANE), :] = hbf
            h_scr[1, pl.ds(rb, SUBLANE), :] = hbb
            return hbf[SUBLANE - 1:SUBLANE, :], hbb[0:1, :]

        return lax.fori_loop(0, nb, body, (h0f, h0b), unroll=4)

    def run(gate_ref, zx_ref, out_ref, n_tok, h0f, h0b):
        x = zx_ref[0]
        t = lax.broadcasted_iota(jnp.int32, x.shape, 0)
        w = cw_ref[...]
        xc = (w[0:1] * jnp.where(t >= 2, pltpu.roll(x, 2, 0), 0.0)
              + w[1:2] * jnp.where(t >= 1, pltpu.roll(x, 1, 0), 0.0)
              + w[2:3] * x
              + w[3:4] * jnp.where(t < n_tok - 1, pltpu.roll(x, n_tok - 1, 0), 0.0)) + cb_ref[...]
        xcb = _bf(xc)
        for d in range(2):
            r = jax.nn.sigmoid(jnp.dot(xcb, wa_ref[d, 0], preferred_element_type=F32) + ba_ref[d:d + 1, :])
            i = jax.nn.sigmoid(jnp.dot(xcb, wx_ref[d, 0], preferred_element_type=F32) + bx_ref[d:d + 1, :])
            nl = -lam_ref[d:d + 1, :]
            softplus = jnp.maximum(nl, 0.0) + jnp.log1p(jnp.exp(-jnp.abs(nl)))
            log_a = -LRU_C * r * softplus
            a = jnp.exp(log_a)
            a_scr[d, 0:n_tok, :] = a
            u_scr[d, 0:n_tok, :] = jnp.sqrt(1.0 - jnp.exp(2.0 * log_a)) * i * xc
        hf, hb = scan_pair(n_tok, h0f, h0b)
        h = h_scr[0, 0:n_tok, :] + h_scr[1, 0:n_tok, :]
        out_ref[0] = (_gelu_tanh(gate_ref[0]) * h).astype(out_ref.dtype)
        return hf, hb

    zero = jnp.zeros((1, LANE), F32)
    hf, hb = run(gc_ref, xc_ref, yc_ref, xc_ref.shape[1], zero, zero)
    run(g_ref, x_ref, y_ref, x_ref.shape[1], hf, hb)


def _lru(z, zc, conv_w, conv_b, wa, ba, wx, bx, lam):
    B, T, _ = z.shape
    Tc = zc.shape[1]
    bw = LRU_BW

    def seq(n, col):
        return pl.BlockSpec((1, n, bw), lambda b, j: (b, 0, col // bw + j))

    vec2 = pl.BlockSpec((2, bw), lambda b, j: (0, j))
    mat = pl.BlockSpec((2, 1, bw, bw), lambda b, j: (0, j, 0, 0))
    out = lambda n: pl.BlockSpec((1, n, bw), lambda b, j: (b, 0, j))
    return pl.pallas_call(
        _lru_kernel,
        grid=(B, LRU_BLOCKS),
        in_specs=[seq(T, COL_LG), seq(T, COL_LX), seq(Tc, COL_LG), seq(Tc, COL_LX),
                  pl.BlockSpec((CONV_W, bw), lambda b, j: (0, j)), pl.BlockSpec((1, bw), lambda b, j: (0, j)),
                  mat, vec2, mat, vec2, vec2],
        out_specs=[out(T), out(Tc)],
        out_shape=[jax.ShapeDtypeStruct((B, T, LRU_WIDTH), BF16), jax.ShapeDtypeStruct((B, Tc, LRU_WIDTH), BF16)],
        scratch_shapes=[pltpu.VMEM((2, T, bw), F32)] * 3,
        compiler_params=_params("arbitrary", "arbitrary"),
        name="rg_lru",
    )(z, z, zc, zc, conv_w, conv_b.reshape(1, -1), _bf(wa), ba, _bf(wx), bx, lam)


def _gla_kernel(q_ref, k_ref, v_ref, g_ref, d_ref, qc_ref, kc_ref, vc_ref, gc_ref, dc_ref,
                wa_ref, ba_ref, ng_ref, y_ref, yc_ref, b_scr, s_scr):
    C = GLA_CHUNK
    ri = lax.broadcasted_iota(jnp.int32, (C, C), 0)
    ci = lax.broadcasted_iota(jnp.int32, (C, C), 1)
    keep = (ri >= ci, ri <= ci)
    tri = (keep[0].astype(F32), keep[1].astype(F32))
    scale = GLA_DK ** -0.5

    def chunk_rows(c):
        return pl.ds(pl.multiple_of(c * C, C), C)

    def run(refs, gate_ref, dec_ref, out_ref, n_tok, st_f, st_b):
        qr, kr, vr = refs
        n = n_tok // C
        dec = dec_ref[0]
        for d in range(2):
            x = jnp.dot(dec, wa_ref[d], preferred_element_type=F32, precision=HIGHEST) + ba_ref[d:d + 1, :]
            log_sig = jnp.minimum(x, 0.0) - jnp.log1p(jnp.exp(-jnp.abs(x)))
            b_scr[d, 0:n_tok, :] = log_sig * (1.0 / GLA_TAU)

        def cumulate(c, carry):
            rows = chunk_rows(c)
            for d in range(2):
                b_scr[d, rows, :] = jnp.dot(tri[d], b_scr[d, rows, :], preferred_element_type=F32,
                                            precision=HIGHEST)
            return carry

        lax.fori_loop(0, n, cumulate, 0, unroll=4)

        def advance(i, states):
            out = []
            for d, st in enumerate(states):
                c = i if d == 0 else n - 1 - i
                rows = chunk_rows(c)
                b = b_scr[d, rows, :]
                b_tot = b[C - 1:C, :] if d == 0 else b[0:1, :]
                ke = _bf(kr[0, rows, :] * jnp.exp(b_tot - b))
                s_scr[d, c] = st.astype(s_scr.dtype)
                out.append(st * jnp.exp(b_tot)
                           + lax.dot_general(_bf(vr[0, rows, :]), ke, TN_DIMS, preferred_element_type=F32))
            return tuple(out)

        st_f, st_b = lax.fori_loop(0, n, advance, (st_f, st_b), unroll=4)

        def emit(c, carry):
            rows = chunk_rows(c)
            q = qr[0, rows, :] * scale
            k = kr[0, rows, :]
            v = _bf(vr[0, rows, :])
            o = None
            for d in range(2):
                b = b_scr[d, rows, :]
                qd = _bf(q * jnp.exp(b))
                att = lax.dot_general(qd, _bf(k * jnp.exp(-b)), NT_DIMS, preferred_element_type=F32)
                att = jnp.where(keep[d], att, 0.0)
                od = (jnp.dot(_bf(att), v, preferred_element_type=F32)
                      + lax.dot_general(qd, s_scr[d, c], NT_DIMS, preferred_element_type=F32))
                o = od if o is None else o + od
            o = o * lax.rsqrt(jnp.mean(o * o, axis=-1, keepdims=True) + EPS) * ng_ref[...]
            gate = gate_ref[0, rows, :]
            out_ref[0, rows, :] = (o * (gate * jax.nn.sigmoid(gate))).astype(out_ref.dtype)
            return carry

        lax.fori_loop(0, n, emit, 0, unroll=2)
        return st_f, st_b

    zero = jnp.zeros((GLA_DV, GLA_DK), F32)
    st_f, st_b = run((qc_ref, kc_ref, vc_ref), gc_ref, dc_ref, yc_ref, qc_ref.shape[1], zero, zero)
    run((q_ref, k_ref, v_ref), g_ref, d_ref, y_ref, q_ref.shape[1], st_f, st_b)


def _gla(z, zc, wa2, ba, norm_g):
    B, T, _ = z.shape
    Tc = zc.shape[1]
    wa_p = jnp.zeros((2, LANE, GLA_HEADS * GLA_DK), F32)
    wa_p = wa_p.at[0, :GLA_RANK].set(wa2[0]).at[1, GLA_RANK:2 * GLA_RANK].set(wa2[1])

    def seq(n, col, w):
        return pl.BlockSpec((1, n, w), lambda b, h: (b, 0, col // w + h), pipeline_mode=pl.Buffered(1))

    def dec(n):
        return pl.BlockSpec((1, n, LANE), lambda b, h: (b, 0, COL_DEC // LANE), pipeline_mode=pl.Buffered(1))

    def ins(n):
        return [seq(n, COL_GQ, GLA_DK), seq(n, COL_GK, GLA_DK), seq(n, COL_GV, GLA_DV), seq(n, COL_GG, GLA_DV), dec(n)]

    out = lambda n: pl.BlockSpec((1, n, GLA_DV), lambda b, h: (b, 0, h))
    return pl.pallas_call(
        _gla_kernel,
        grid=(B, GLA_HEADS),
        in_specs=ins(T) + ins(Tc) + [
            pl.BlockSpec((2, LANE, GLA_DK), lambda b, h: (0, 0, h)),
            pl.BlockSpec((2, GLA_DK), lambda b, h: (0, h)),
            pl.BlockSpec((1, GLA_DV), lambda b, h: (0, 0)),
        ],
        out_specs=[out(T), out(Tc)],
        out_shape=[jax.ShapeDtypeStruct((B, T, BRANCH_W), BF16), jax.ShapeDtypeStruct((B, Tc, BRANCH_W), BF16)],
        scratch_shapes=[pltpu.VMEM((2, T, GLA_DK), F32), pltpu.VMEM((2, T // GLA_CHUNK, GLA_DV, GLA_DK), BF16)],
        compiler_params=_params("arbitrary", "arbitrary"),
        name="gla",
    )(z, z, z, z, z, zc, zc, zc, zc, zc, wa_p, ba, norm_g.reshape(1, -1))


def _permute_mix_weight(w_rows):
    o = np.cumsum((0,) + MIX_SIZES)
    gq, gk, gv, gg, af, ab, cq, ckv, kr, lg, lx, nq, nk, nv = [w_rows[o[i]:o[i + 1]] for i in range(len(MIX_SIZES))]
    half = MLA_ROPE // 2
    kr_sw = jnp.concatenate([kr[half:], kr[:half]], axis=0)
    pad = jnp.zeros((MIXP - (COL_DEC + 2 * GLA_RANK), w_rows.shape[1]), w_rows.dtype)
    return _bf(jnp.concatenate([gv, gg, lg, lx, nq, nk, nv, gq, gk, cq, ckv, kr, kr_sw, af, ab, pad], axis=0))


def kernel(x, c, ctx, c_ctx, w_ada, b_ada, norm1_g, norm2_g, w_in, gla_wa2, gla_ba, gla_norm_g, mla_q_norm_g, mla_w_uq, mla_kv_norm_g, mla_w_ukv, lru_conv_w, lru_conv_b, lru_wa, lru_ba, lru_wx, lru_bx, lru_lambda, na_rpb, w_branch, w_out, peer_wq, peer_keys, peer_u, peer_v, final_norm_g):
    B, T, D = x.shape
    Tc = ctx.shape[1]
    L = w_ada.shape[0]
    rope_l = _rope_tables(T, True)
    rope_c = _rope_tables(Tc, False)

    n_rows = -(-(B + 1) // SUBLANE) * SUBLANE
    cc = jnp.zeros((n_rows, D), F32).at[:B].set(c).at[B].set(c_ctx)
    mods = _ada(cc, w_ada, b_ada)

    xc = ctx
    pe_l = pe_c = gate_l = gate_c = None
    pu = _bf(peer_u)
    pv = _bf(peer_v)
    w_in_rows = _bf(jnp.swapaxes(w_in, 1, 2))
    wb = _bf(w_branch)
    wo = _bf(w_out)
    wq = _bf(peer_wq)
    for l in range(L):
        update_ctx = l < L - 1
        ml = jnp.split(mods[l, :B], 6, axis=-1)
        mc = jnp.split(jnp.broadcast_to(mods[l, B], (B, 6 * D)), 6, axis=-1)
        w_mix = _permute_mix_weight(w_in_rows[l, :MIX_COLS])
        mla_wq, mla_wkv = _mla_weights(mla_w_uq[l], mla_w_ukv[l])

        xn, h = _norm(x, norm1_g[l], delta=pe_l, gate=gate_l, shift=ml[0], scale=ml[1])
        x = x if xn is None else xn
        xcn, hc = _norm(xc, norm1_g[l], delta=pe_c, gate=gate_c, shift=mc[0], scale=mc[1])
        xc = xc if xcn is None else xcn

        h2d = h.reshape(B * T, D)
        hc2d = hc.reshape(B * Tc, D)
        z = _matmul(h2d, w_mix, out_dtype=F32, b_rows=True).reshape(B, T, MIXP)
        zc = _matmul(hc2d, w_mix, out_dtype=F32, b_rows=True).reshape(B, Tc, MIXP)

        y_gla, yc_gla = _gla(z, zc, gla_wa2[l], gla_ba[l], gla_norm_g[l])
        y_lru, yc_lru = _lru(z, zc, lru_conv_w[l], lru_conv_b[l], lru_wa[l], lru_ba[l], lru_wx[l], lru_bx[l],
                             lru_lambda[l])
        ql, kl, vl = _mla_prep(z, *rope_l, mla_q_norm_g[l], mla_kv_norm_g[l], mla_wq, mla_wkv)
        qc, kc, vc = _mla_prep(zc, *rope_c, mla_q_norm_g[l], mla_kv_norm_g[l], mla_wq, mla_wkv)
        mla_args = dict(heads=MLA_HEADS, dq=MLA_DQP, dv=MLA_DV, scale=(MLA_NOPE + MLA_ROPE) ** -0.5)
        y_mla = _attention(ql, 0, kc, 0, vc, 0, kl, 0, vl, 0, **mla_args)
        y_na = _na(z, zc, na_rpb[l])
        ys_l = [y_gla, y_mla, y_lru, y_na]
        if update_ctx:
            yc_mla = _attention(qc, 0, kc, 0, vc, 0, **mla_args)
            yc_na = _attention(zc, COL_NQ, zc, COL_NK, zc, COL_NV, heads=NA_HEADS, dq=NA_DH, dv=NA_DH,
                               scale=NA_DH ** -0.5)
            ys_c = [yc_gla, yc_mla, yc_lru, yc_na]

        def channel_mix(xs, hs, ys_s, m, n_tok):
            M = B * n_tok
            gates = _matmul(hs, w_in_rows, out_dtype=BF16, act="sigmoid", layer=l, b_rows=True,
                            row0=MIX_COLS, n_out=N_BRANCH * D)
            mrg = _merge([y.reshape(M, BRANCH_W) for y in ys_s], gates, wb, l)
            xs = _matmul(mrg, wo, out_dtype=F32, res=xs.reshape(M, D), mod=m[2],
                         rows_per_batch=n_tok, layer=l).reshape(B, n_tok, D)
            _, h2 = _norm(xs, norm2_g[l], shift=m[3], scale=m[4])
            pe = _peer(h2.reshape(M, D), wq, peer_keys[l], pu, pv, l).reshape(B, n_tok, D)
            return xs, pe

        x, pe_l = channel_mix(x, h2d, ys_l, ml, T)
        gate_l = ml[5]
        if update_ctx:
            xc, pe_c = channel_mix(xc, hc2d, ys_c, mc, Tc)
            gate_c = mc[5]
        else:
            pe_c = gate_c = None
    _, out = _norm(x, final_norm_g, delta=pe_l, gate=gate_l, out_dtype=F32, emit_x=False)
    return out
```

```python
import functools

import jax
import jax.numpy as jnp
import numpy as np
from jax import lax
from jax.experimental import pallas as pl
from jax.experimental.pallas import tpu as pltpu

GRID_W = 64
EPS = 1e-6
NEG_INF = -1e30
N_BRANCH = 4
BRANCH_W = 1024
ROPE_THETA = 10000.0

GLA_HEADS = 4
GLA_DK = 128
GLA_DV = BRANCH_W // GLA_HEADS
GLA_RANK = 16
GLA_TAU = 16.0
GLA_CHUNK = 64

MLA_HEADS = 8
MLA_Q_RANK = 768
MLA_KV_RANK = 256
MLA_NOPE = 128
MLA_ROPE = 64
MLA_DV = BRANCH_W // MLA_HEADS
MLA_DQP = 256

LRU_WIDTH = BRANCH_W
LRU_BLOCKS = 8
LRU_BW = LRU_WIDTH // LRU_BLOCKS
CONV_W = 4
LRU_C = 8.0

NA_HEADS = 8
NA_DH = BRANCH_W // NA_HEADS
NA_WIN_R = 8
NA_WIN_C = 16
NA_QROWS = 8
NA_KROWS = 16

PEER_HEADS = 8
PEER_DQ = 256
PEER_TOPK = 16

MIX_SIZES = (
    GLA_HEADS * GLA_DK, GLA_HEADS * GLA_DK, GLA_HEADS * GLA_DV, GLA_HEADS * GLA_DV, GLA_RANK, GLA_RANK,
    MLA_Q_RANK, MLA_KV_RANK, MLA_ROPE, LRU_WIDTH, LRU_WIDTH,
    NA_HEADS * NA_DH, NA_HEADS * NA_DH, NA_HEADS * NA_DH,
)
MIX_COLS = sum(MIX_SIZES)

COL_GV, COL_GG, COL_LG, COL_LX, COL_NQ, COL_NK, COL_NV = 0, 1024, 2048, 3072, 4096, 5120, 6144
COL_GQ, COL_GK = 7168, 7680
COL_MQKV = 8192
COL_KR = 9216
COL_DEC = 9344
MIXP = 9728

V7X_VMEM_BYTES = 64 * 1024 * 1024
VMEM_LIMIT = V7X_VMEM_BYTES - 8 * 1024 * 1024
LANE = 128
SUBLANE = 8
ROW_ALIGN = 32

F32 = jnp.float32
BF16 = jnp.bfloat16
HIGHEST = lax.Precision.HIGHEST
LOG2_E = 1.4426950408889634
NT_DIMS = (((1,), (1,)), ((), ()))
TN_DIMS = (((0,), (0,)), ((), ()))


def _tile(n, pref, mult=SUBLANE):
    if n <= pref:
        return n
    t = (pref // mult) * mult
    while t > mult and n % t:
        t -= mult
    assert n % t == 0, (n, pref, mult)
    return t


def _params(*sem):
    return pltpu.CompilerParams(dimension_semantics=sem, vmem_limit_bytes=VMEM_LIMIT)


def _bf(x):
    return x.astype(BF16)


def _ada_kernel(c_ref, w_ref, b_ref, o_ref):
    cv = c_ref[...]
    a = cv * jax.nn.sigmoid(cv)
    o_ref[0] = jnp.dot(a, w_ref[0], preferred_element_type=F32, precision=HIGHEST) + b_ref[0]


def _ada(cc, w_ada, b_ada):
    L, D, W = w_ada.shape
    R = cc.shape[0]
    tn = _tile(W, 512, LANE)
    return pl.pallas_call(
        _ada_kernel,
        grid=(L, W // tn),
        in_specs=[
            pl.BlockSpec((R, D), lambda l, j: (0, 0)),
            pl.BlockSpec((1, D, tn), lambda l, j: (l, 0, j)),
            pl.BlockSpec((1, 1, tn), lambda l, j: (l, 0, j)),
        ],
        out_specs=pl.BlockSpec((1, R, tn), lambda l, j: (l, 0, j)),
        out_shape=jax.ShapeDtypeStruct((L, R, W), F32),
        compiler_params=_params("arbitrary", "arbitrary"),
        name="ada_mod",
    )(cc, w_ada, b_ada.reshape(L, 1, W))


def _norm_kernel(*refs, has_delta, modulate, emit_x):
    it = iter(refs)
    x_ref = next(it)
    if has_delta:
        d_ref, gate_ref = next(it), next(it)
    g_ref = next(it)
    if modulate:
        shift_ref, scale_ref = next(it), next(it)
    if emit_x:
        xo_ref = next(it)
    h_ref = next(it)
    x = x_ref[0]
    if has_delta:
        x = x + gate_ref[0] * d_ref[0]
    if emit_x:
        xo_ref[0] = x
    y = x * lax.rsqrt(jnp.mean(x * x, axis=-1, keepdims=True) + EPS)
    y = y * g_ref[...]
    if modulate:
        y = y * (1.0 + scale_ref[0]) + shift_ref[0]
    h_ref[0] = y.astype(h_ref.dtype)


def _norm(x, g, *, delta=None, gate=None, shift=None, scale=None, out_dtype=None, emit_x=True):
    out_dtype = BF16 if out_dtype is None else out_dtype
    B, T, D = x.shape
    tt = _tile(T, 256)
    has_delta = delta is not None
    emit_x = emit_x and has_delta
    modulate = shift is not None
    tok = pl.BlockSpec((1, tt, D), lambda b, t: (b, t, 0))
    vec = pl.BlockSpec((1, 1, D), lambda b, t: (b, 0, 0))
    args, specs = [x], [tok]
    if has_delta:
        args += [delta, gate.reshape(B, 1, D)]
        specs += [tok, vec]
    args.append(g.reshape(1, D))
    specs.append(pl.BlockSpec((1, D), lambda b, t: (0, 0)))
    if modulate:
        args += [shift.reshape(B, 1, D), scale.reshape(B, 1, D)]
        specs += [vec, vec]
    out_shape, out_specs = [], []
    if emit_x:
        out_shape.append(jax.ShapeDtypeStruct((B, T, D), F32))
        out_specs.append(tok)
    out_shape.append(jax.ShapeDtypeStruct((B, T, D), out_dtype))
    out_specs.append(tok)
    outs = pl.pallas_call(
        functools.partial(_norm_kernel, has_delta=has_delta, modulate=modulate, emit_x=emit_x),
        grid=(B, T // tt),
        in_specs=specs,
        out_specs=out_specs,
        out_shape=out_shape,
        compiler_params=_params("arbitrary", "arbitrary"),
        name="res_norm_mod",
    )(*args)
    if emit_x:
        return outs[0], outs[1]
    return None, outs[0]


def _mm_kernel(*refs, act, has_res, b_rows):
    if has_res:
        a_ref, b_ref, r_ref, m_ref, o_ref = refs
    else:
        a_ref, b_ref, o_ref = refs
    if b_rows:
        b = b_ref[0] if len(b_ref.shape) == 3 else b_ref[...]
        acc = lax.dot_general(a_ref[...], b, NT_DIMS, preferred_element_type=F32)
    else:
        acc = jnp.dot(a_ref[...], b_ref[...], preferred_element_type=F32)
    if act == "sigmoid":
        acc = jax.nn.sigmoid(acc)
    if has_res:
        acc = r_ref[...] + m_ref[0] * acc
    o_ref[...] = acc.astype(o_ref.dtype)


def _matmul(a, b, *, out_dtype, act=None, res=None, mod=None, rows_per_batch=None, layer=None, b_rows=False,
            row0=None, n_out=None, tm=1024, tn=1024):
    M, K = a.shape
    N = n_out if row0 is not None else (b.shape[-2] if b_rows else b.shape[-1])
    tm = _tile(rows_per_batch if rows_per_batch else M, tm)
    tn = _tile(N, tn, LANE)
    has_res = res is not None
    args = [a, b]
    blk, at = ((tn, K), lambda j: (j, 0)) if b_rows else ((K, tn), lambda j: (0, j))
    if layer is None:
        b_spec = pl.BlockSpec(blk, lambda i, j: at(j))
    else:
        b_spec = pl.BlockSpec((None,) + blk, lambda i, j: (layer,) + at(j))
    if row0 is not None:
        assert b_rows and layer is not None and row0 % ROW_ALIGN == 0
        b_spec = pl.BlockSpec((pl.Element(1), pl.Element(tn), pl.Element(K)),
                              lambda i, j: (layer, pl.multiple_of(row0 + j * tn, ROW_ALIGN), 0))
    specs = [pl.BlockSpec((tm, K), lambda i, j: (i, 0)), b_spec]
    if has_res:
        args += [res, mod.reshape(mod.shape[0], 1, N)]
        specs += [
            pl.BlockSpec((tm, tn), lambda i, j: (i, j)),
            pl.BlockSpec((1, 1, tn), lambda i, j: ((i * tm) // rows_per_batch, 0, j)),
        ]
    return pl.pallas_call(
        functools.partial(_mm_kernel, act=act, has_res=has_res, b_rows=b_rows),
        grid=(M // tm, N // tn),
        in_specs=specs,
        out_specs=pl.BlockSpec((tm, tn), lambda i, j: (i, j)),
        out_shape=jax.ShapeDtypeStruct((M, N), out_dtype),
        compiler_params=_params("arbitrary", "arbitrary"),
        name="matmul",
    )(*args)


def _merge_kernel(y0, y1, y2, y3, g0, g1, g2, g3, w_ref, o_ref):
    acc = None
    for i, (y, g) in enumerate(((y0, g0), (y1, g1), (y2, g2), (y3, g3))):
        p = g[...].astype(F32) * jnp.dot(y[...], w_ref[i], preferred_element_type=F32)
        acc = p if acc is None else acc + p
    o_ref[...] = acc.astype(o_ref.dtype)


def _merge(ys, gates, w_branch, layer):
    M = ys[0].shape[0]
    D = w_branch.shape[3]
    tm = _tile(M, 1024)
    tn = _tile(D, 512, LANE)
    nj = D // tn
    y_spec = pl.BlockSpec((tm, BRANCH_W), lambda i, j: (i, 0))
    g_specs = [pl.BlockSpec((tm, tn), functools.partial(lambda i, j, br: (i, br * nj + j), br=br))
               for br in range(N_BRANCH)]
    return pl.pallas_call(
        _merge_kernel,
        grid=(M // tm, nj),
        in_specs=[y_spec] * N_BRANCH + g_specs + [
            pl.BlockSpec((None, N_BRANCH, BRANCH_W, tn), lambda i, j: (layer, 0, 0, j))],
        out_specs=pl.BlockSpec((tm, tn), lambda i, j: (i, j)),
        out_shape=jax.ShapeDtypeStruct((M, D), BF16),
        compiler_params=_params("arbitrary", "arbitrary"),
        name="merge",
    )(*ys, gates, gates, gates, gates, w_branch)


def _topk_rows(s, k):
    n = s.shape[0]
    iota = lax.broadcasted_iota(jnp.int32, s.shape, 0).astype(F32)
    vals, idxs = [], []
    for _ in range(k):
        m = jnp.max(s, axis=0, keepdims=True)
        am = jnp.min(jnp.where(s == m, iota, float(n)), axis=0, keepdims=True)
        vals.append(m)
        idxs.append(am)
        s = jnp.where(iota == am, -jnp.inf, s)
    return jnp.concatenate(vals, axis=0), jnp.concatenate(idxs, axis=0)


def _candidate_rows(x1, x2):
    K, m = x1.shape
    r1, r2, ok = [], [], []
    a = 0
    while K // (a + 1) > 1:
        nb = K // (a + 1)
        width = -(-nb // SUBLANE) * SUBLANE
        r1.append(jnp.broadcast_to(x1[a:a + 1], (width, m)))
        r2.append(x2[0:width])
        ok.append(lax.broadcasted_iota(jnp.int32, (width, m), 0) < nb)
        a += 1
    assert (K - a) % SUBLANE == 0
    r1.append(x1[a:K])
    r2.append(jnp.broadcast_to(x2[0:1], (K - a, m)))
    ok.append(jnp.full((K - a, m), True))
    return jnp.concatenate(r1, axis=0), jnp.concatenate(r2, axis=0), jnp.concatenate(ok, axis=0)


def _peer_route_kernel(h_ref, wq_ref, k_ref, g_ref, e1_ref, e2_ref, qa_scr, qb_scr):
    H = wq_ref.shape[0]
    assert H % 2 == 0
    half = PEER_DQ // 2
    K = PEER_TOPK

    def project(h, dst):
        dst[...] = jnp.dot(h_ref[...], wq_ref[h], preferred_element_type=F32)

    def route(h, q_scr):
        rows = pl.ds(pl.multiple_of(h * K, K), K)
        for c0 in range(0, h_ref.shape[0], LANE):
            cols = slice(c0, c0 + LANE)
            q = q_scr[cols, :]
            v1, i1 = _topk_rows(lax.dot_general(k_ref[h, 0], q[:, :half], NT_DIMS, preferred_element_type=F32,
                                                precision=HIGHEST), K)
            v2, i2 = _topk_rows(lax.dot_general(k_ref[h, 1], q[:, half:], NT_DIMS, preferred_element_type=F32,
                                                precision=HIGHEST), K)
            c1, c2, ok = _candidate_rows(v1, v2)
            top, pos = _topk_rows(jnp.where(ok, c1 + c2, -jnp.inf), K)
            p = jnp.exp(top - top[0:1])
            g_ref[rows, cols] = p / jnp.sum(p, axis=0, keepdims=True)
            id1, id2, _ = _candidate_rows(i1, i2)
            row = lax.broadcasted_iota(jnp.int32, id1.shape, 0).astype(F32)
            e1, e2 = [], []
            for r in range(K):
                sel = row == pos[r:r + 1]
                e1.append(jnp.sum(jnp.where(sel, id1, 0.0), axis=0, keepdims=True))
                e2.append(jnp.sum(jnp.where(sel, id2, 0.0), axis=0, keepdims=True))
            e1_ref[rows, cols] = jnp.concatenate(e1, axis=0)
            e2_ref[rows, cols] = jnp.concatenate(e2, axis=0)

    project(0, qa_scr)

    def head_pair(i, carry):
        h = 2 * i
        project(h + 1, qb_scr)
        route(h, qa_scr)
        project(jnp.minimum(h + 2, H - 1), qa_scr)
        route(h + 1, qb_scr)
        return carry

    lax.fori_loop(0, H // 2, head_pair, 0)


def _peer_route(h2, wq_heads, keys, layer):
    M, D = h2.shape
    H, _, nk, half = keys.shape
    tm = _tile(M, 256, LANE)
    out = jax.ShapeDtypeStruct((H * PEER_TOPK, M), F32)
    o_spec = pl.BlockSpec((H * PEER_TOPK, tm), lambda i: (0, i))
    return pl.pallas_call(
        _peer_route_kernel,
        grid=(M // tm,),
        in_specs=[
            pl.BlockSpec((tm, D), lambda i: (i, 0)),
            pl.BlockSpec((None, H, D, PEER_DQ), lambda i: (layer, 0, 0, 0), pipeline_mode=pl.Buffered(1)),
            pl.BlockSpec((H, 2, nk, half), lambda i: (0, 0, 0, 0)),
        ],
        out_specs=[o_spec, o_spec, o_spec],
        out_shape=[out, out, out],
        scratch_shapes=[pltpu.VMEM((tm, PEER_DQ), F32)] * 2,
        compiler_params=_params("arbitrary"),
        name="peer_route",
    )(h2, wq_heads, keys)


def _peer_w_kernel(g_ref, e1_ref, e2_ref, o_ref, gt_ref, e1t_ref, e2t_ref, w_scr, *, nk):
    tm = o_ref.shape[0]
    gt_ref[...] = g_ref[...].T
    e1t_ref[...] = e1_ref[...].T
    e2t_ref[...] = e2_ref[...].T
    key_iota = lax.broadcasted_iota(jnp.int32, (nk, g_ref.shape[0]), 0).astype(F32)

    def body(t, carry):
        row = pl.ds(t, 1)
        a = jnp.where(e1t_ref[row, :] == key_iota, gt_ref[row, :], 0.0).astype(BF16)
        b = jnp.where(e2t_ref[row, :] == key_iota, 1.0, 0.0).astype(BF16)
        w_scr[t] = lax.dot_general(a, b, NT_DIMS, preferred_element_type=F32)
        return carry

    lax.fori_loop(0, tm, body, 0, unroll=32)
    tb = _tile(tm, 64)
    for t0 in range(0, tm, tb):
        planes = jnp.swapaxes(w_scr[t0:t0 + tb], 0, 1)
        for j in range(nk):
            o_ref[t0:t0 + tb, j * nk:(j + 1) * nk] = planes[j].astype(o_ref.dtype)


def _peer_w(g, e1, e2, nk):
    S, M = g.shape
    tm = _tile(M, 256, LANE)
    spec = pl.BlockSpec((S, tm), lambda i: (0, i))
    return pl.pallas_call(
        functools.partial(_peer_w_kernel, nk=nk),
        grid=(M // tm,),
        in_specs=[spec, spec, spec],
        out_specs=pl.BlockSpec((tm, nk * nk), lambda i: (i, 0)),
        out_shape=jax.ShapeDtypeStruct((M, nk * nk), BF16),
        scratch_shapes=[pltpu.VMEM((tm, S), F32)] * 3 + [pltpu.VMEM((tm, nk, nk), F32)],
        compiler_params=_params("arbitrary"),
        name="peer_route_weights",
    )(g, e1, e2)


def _gelu_tanh(x):
    return 0.5 * x * (1.0 + jnp.tanh(0.7978845608028654 * (x + 0.044715 * (x * x * x))))


def _peer_dense_kernel(h_ref, u_ref, v_ref, w_ref, o_ref):
    @pl.when(pl.program_id(1) == 0)
    def _():
        o_ref[...] = jnp.zeros_like(o_ref)

    s = lax.dot_general(h_ref[...], u_ref[...], NT_DIMS, preferred_element_type=F32)
    a = (_gelu_tanh(s) * w_ref[...].astype(F32)).astype(BF16)
    o_ref[...] += jnp.dot(a, v_ref[...], preferred_element_type=F32)


def _peer_dense(h2, u, v, w, layer):
    M, D = h2.shape
    E = u.shape[1]
    tm = _tile(M, 1024)
    te = _tile(E, 512, LANE)
    once = pl.Buffered(1)
    return pl.pallas_call(
        _peer_dense_kernel,
        grid=(M // tm, E // te),
        in_specs=[
            pl.BlockSpec((tm, D), lambda i, e: (i, 0), pipeline_mode=once),
            pl.BlockSpec((None, te, D), lambda i, e: (layer, e, 0)),
            pl.BlockSpec((None, te, D), lambda i, e: (layer, e, 0)),
            pl.BlockSpec((tm, te), lambda i, e: (i, e)),
        ],
        out_specs=pl.BlockSpec((tm, D), lambda i, e: (i, 0), pipeline_mode=once),
        out_shape=jax.ShapeDtypeStruct((M, D), F32),
        compiler_params=_params("arbitrary", "arbitrary"),
        name="peer_dense",
    )(h2, u, v, w)


def _peer(h2, wq, keys, u, v, layer):
    nk = keys.shape[2]
    g, e1, e2 = _peer_route(h2, wq, keys, layer)
    return _peer_dense(h2, u, v, _peer_w(g, e1, e2, nk), layer)


def _attn_kernel(*refs, scale, two):
    if two:
        q_ref, k1_ref, v1_ref, k2_ref, v2_ref, o_ref = refs
    else:
        q_ref, k1_ref, v1_ref, o_ref = refs
    tq = q_ref.shape[1]
    tg = _tile(tq, 256)
    for r0 in range(0, tq, tg):
        rows = slice(r0, r0 + tg)
        q = _bf(q_ref[0, rows, :].astype(F32) * (scale * LOG2_E))
        s1 = lax.dot_general(q, _bf(k1_ref[0]), NT_DIMS, preferred_element_type=F32)
        m = jnp.max(s1, axis=-1, keepdims=True)
        if two:
            s2 = lax.dot_general(q, _bf(k2_ref[0]), NT_DIMS, preferred_element_type=F32)
            m = jnp.maximum(m, jnp.max(s2, axis=-1, keepdims=True))
        p1 = jnp.exp2(s1 - m)
        l = jnp.sum(p1, axis=-1, keepdims=True)
        o = jnp.dot(_bf(p1), _bf(v1_ref[0]), preferred_element_type=F32)
        if two:
            p2 = jnp.exp2(s2 - m)
            l = l + jnp.sum(p2, axis=-1, keepdims=True)
            o = o + jnp.dot(_bf(p2), _bf(v2_ref[0]), preferred_element_type=F32)
        o_ref[0, rows, :] = (o / l).astype(o_ref.dtype)


def _attention(q, qcol, k1, k1col, v1, v1col, k2=None, k2col=0, v2=None, v2col=0, *, heads, dq, dv, scale):
    B, Tq, _ = q.shape
    tq = _tile(Tq, 512)
    two = k2 is not None

    def spec(arr, col, w, tiled):
        n = arr.shape[1]
        if tiled:
            return pl.BlockSpec((1, tq, w), lambda b, h, t: (b, t, col // w + h))
        return pl.BlockSpec((1, n, w), lambda b, h, t: (b, 0, col // w + h))

    args = [q, k1, v1]
    specs = [spec(q, qcol, dq, True), spec(k1, k1col, dq, False), spec(v1, v1col, dv, False)]
    if two:
        args += [k2, v2]
        specs += [spec(k2, k2col, dq, False), spec(v2, v2col, dv, False)]
    return pl.pallas_call(
        functools.partial(_attn_kernel, scale=scale, two=two),
        grid=(B, heads, Tq // tq),
        in_specs=specs,
        out_specs=pl.BlockSpec((1, tq, dv), lambda b, h, t: (b, t, h)),
        out_shape=jax.ShapeDtypeStruct((B, Tq, heads * dv), BF16),
        compiler_params=_params("arbitrary", "arbitrary", "arbitrary"),
        name="attention",
    )(*args)


def _mla_prep_kernel(z_ref, zr_ref, c_ref, s_ref, gq_ref, gkv_ref, wq_ref, wkv_ref, q_out, k_out, v_out):
    z = z_ref[0]

    def rms(x, g):
        return x * lax.rsqrt(jnp.mean(x * x, axis=-1, keepdims=True) + EPS) * g

    q = jnp.dot(_bf(rms(z[:, :MLA_Q_RANK], gq_ref[...])), wq_ref[...], preferred_element_type=F32)
    kv = jnp.dot(_bf(rms(z[:, MLA_Q_RANK:], gkv_ref[...])), wkv_ref[...], preferred_element_type=F32)
    cos, sin = c_ref[...], s_ref[...]

    def rope(x):
        return x * cos + pltpu.roll(x, MLA_ROPE, 1) * sin

    kr = rope(zr_ref[0]).astype(k_out.dtype)
    hv = MLA_HEADS * MLA_NOPE
    for h in range(MLA_HEADS):
        lo = h * MLA_DQP
        q_out[0, :, lo:lo + MLA_NOPE] = q[:, lo:lo + MLA_NOPE].astype(q_out.dtype)
        q_out[0, :, lo + MLA_NOPE:lo + MLA_DQP] = rope(q[:, lo + MLA_NOPE:lo + MLA_DQP]).astype(q_out.dtype)
        k_out[0, :, lo:lo + MLA_NOPE] = kv[:, h * MLA_NOPE:(h + 1) * MLA_NOPE].astype(k_out.dtype)
        k_out[0, :, lo + MLA_NOPE:lo + MLA_DQP] = kr
    v_out[0] = kv[:, hv:].astype(v_out.dtype)


def _mla_prep(z, cos_t, sin_t, gq, gkv, wq_p, wkv_p):
    B, T, _ = z.shape
    tt = _tile(T, 256)
    wq_w = MLA_HEADS * MLA_DQP
    wkv_w = MLA_HEADS * (MLA_NOPE + MLA_DV)
    cw = MLA_Q_RANK + MLA_KV_RANK
    tab = pl.BlockSpec((tt, LANE), lambda b, t: (t, 0))
    return pl.pallas_call(
        _mla_prep_kernel,
        grid=(B, T // tt),
        in_specs=[
            pl.BlockSpec((1, tt, cw), lambda b, t: (b, t, COL_MQKV // cw)),
            pl.BlockSpec((1, tt, LANE), lambda b, t: (b, t, COL_KR // LANE)),
            tab, tab,
            pl.BlockSpec((1, MLA_Q_RANK), lambda b, t: (0, 0)),
            pl.BlockSpec((1, MLA_KV_RANK), lambda b, t: (0, 0)),
            pl.BlockSpec((MLA_Q_RANK, wq_w), lambda b, t: (0, 0)),
            pl.BlockSpec((MLA_KV_RANK, wkv_w), lambda b, t: (0, 0)),
        ],
        out_specs=[
            pl.BlockSpec((1, tt, wq_w), lambda b, t: (b, t, 0)),
            pl.BlockSpec((1, tt, wq_w), lambda b, t: (b, t, 0)),
            pl.BlockSpec((1, tt, MLA_HEADS * MLA_DV), lambda b, t: (b, t, 0)),
        ],
        out_shape=[
            jax.ShapeDtypeStruct((B, T, wq_w), BF16),
            jax.ShapeDtypeStruct((B, T, wq_w), BF16),
            jax.ShapeDtypeStruct((B, T, MLA_HEADS * MLA_DV), BF16),
        ],
        compiler_params=_params("arbitrary", "arbitrary"),
        name="mla_prep",
    )(z, z, cos_t, sin_t, gq.reshape(1, -1), gkv.reshape(1, -1), wq_p, wkv_p)


def _mla_weights(w_uq, w_ukv):
    half = MLA_ROPE // 2
    wq = w_uq.reshape(MLA_Q_RANK, MLA_HEADS, MLA_NOPE + MLA_ROPE)
    pe = wq[..., MLA_NOPE:]
    pe_sw = jnp.concatenate([pe[..., half:], pe[..., :half]], axis=-1)
    wq_p = jnp.concatenate([wq[..., :MLA_NOPE], pe, pe_sw], axis=-1).reshape(MLA_Q_RANK, MLA_HEADS * MLA_DQP)
    wkv = w_ukv.reshape(MLA_KV_RANK, MLA_HEADS, MLA_NOPE + MLA_DV)
    wkv_p = jnp.concatenate([wkv[..., :MLA_NOPE].reshape(MLA_KV_RANK, -1),
                             wkv[..., MLA_NOPE:].reshape(MLA_KV_RANK, -1)], axis=-1)
    return _bf(wq_p), _bf(wkv_p)


def _rope_tables(n_tok, with_pos):
    n_freq = MLA_ROPE // 4
    zeros = jnp.zeros((n_tok, LANE - MLA_ROPE), F32)
    if not with_pos:
        return (jnp.concatenate([jnp.ones((n_tok, MLA_ROPE), F32), zeros], axis=1),
                jnp.zeros((n_tok, LANE), F32))
    t = jnp.arange(n_tok, dtype=jnp.int32)
    row = (t // GRID_W).astype(F32)
    col = (t % GRID_W).astype(F32)
    inv = ROPE_THETA ** (-jnp.arange(n_freq, dtype=F32) / n_freq)
    ang = jnp.concatenate([row[:, None] * inv, col[:, None] * inv], axis=-1)
    cos, sin = jnp.cos(ang), jnp.sin(ang)
    return (jnp.concatenate([cos, cos, zeros], axis=1), jnp.concatenate([-sin, sin, zeros], axis=1))


def _na_kernel(q_ref, k_ref, v_ref, kc_ref, vc_ref, bm_ref, o_ref, *, scale, rows):
    nblk = rows // NA_QROWS
    nq = NA_QROWS * GRID_W
    nkw = NA_KROWS * GRID_W
    kc = _bf(kc_ref[0])
    vc = _bf(vc_ref[0])

    def body(blk, carry):
        q0 = pl.multiple_of(blk * nq, nq)
        kb = jnp.clip(NA_QROWS * blk - NA_WIN_R // 2, 0, rows - NA_KROWS)
        k0 = pl.multiple_of(kb * GRID_W, (NA_WIN_R // 2) * GRID_W)
        pat = jnp.where(blk == 0, 0, jnp.where(blk == nblk - 1, 2, 1))
        q = _bf(q_ref[0, pl.ds(q0, nq), :] * scale)
        kw = _bf(k_ref[0, pl.ds(k0, nkw), :])
        vw = _bf(v_ref[0, pl.ds(k0, nkw), :])
        bm = bm_ref[0, pat]
        s = lax.dot_general(q, kw, NT_DIMS, preferred_element_type=F32)
        s = jnp.where(bm > 0.5 * NEG_INF, s + bm, NEG_INF)
        sc = lax.dot_general(q, kc, NT_DIMS, preferred_element_type=F32)
        m = jnp.maximum(jnp.max(s, axis=-1, keepdims=True), jnp.max(sc, axis=-1, keepdims=True))
        p = jnp.exp(s - m)
        pc = jnp.exp(sc - m)
        l = jnp.sum(p, axis=-1, keepdims=True) + jnp.sum(pc, axis=-1, keepdims=True)
        o = jnp.dot(_bf(p), vw, preferred_element_type=F32) + jnp.dot(_bf(pc), vc, preferred_element_type=F32)
        o_ref[0, pl.ds(q0, nq), :] = (o / l).astype(o_ref.dtype)
        return carry

    lax.fori_loop(0, nblk, body, 0, unroll=2)


def _na_bias_table(rpb, rows):
    nblk = rows // NA_QROWS
    col = np.arange(GRID_W)
    c_start = np.clip(col - NA_WIN_C // 2, 0, GRID_W - NA_WIN_C)
    in_win = (col[None, :] >= c_start[:, None]) & (col[None, :] < c_start[:, None] + NA_WIN_C)
    dc = np.clip(col[None, :] - col[:, None] + NA_WIN_C - 1, 0, 2 * NA_WIN_C - 2)
    dc_onehot = (dc[:, :, None] == np.arange(2 * NA_WIN_C - 1)).astype(np.float32)
    toeplitz = jnp.einsum("hab,qkb->haqk", rpb.astype(F32), dc_onehot, precision=HIGHEST)
    n_dr = 2 * NA_WIN_R - 1
    sel = np.zeros((3, NA_QROWS, NA_KROWS, n_dr), np.float32)
    valid = np.zeros((3, NA_QROWS, NA_KROWS), bool)
    for p, blk in enumerate((0, min(1, nblk - 1), nblk - 1)):
        kb = int(np.clip(NA_QROWS * blk - NA_WIN_R // 2, 0, rows - NA_KROWS))
        for rq in range(NA_QROWS):
            r = NA_QROWS * blk + rq
            r_start = int(np.clip(r - NA_WIN_R // 2, 0, rows - NA_WIN_R))
            for rk in range(NA_KROWS):
                kr = kb + rk
                if r_start <= kr < r_start + NA_WIN_R:
                    sel[p, rq, rk, kr - r + NA_WIN_R - 1] = 1.0
                    valid[p, rq, rk] = True
    bias = jnp.einsum("prka,haqc->hprqkc", sel, toeplitz, precision=HIGHEST)
    mask = valid[None, :, :, None, :, None] & in_win[None, None, None, :, None, :]
    H = rpb.shape[0]
    return jnp.where(mask, bias, NEG_INF).reshape(H, 3, NA_QROWS * GRID_W, NA_KROWS * GRID_W)


def _na(z, zc, rpb):
    B, T, _ = z.shape
    Tc = zc.shape[1]
    rows = T // GRID_W
    assert rows % NA_QROWS == 0 and rows >= NA_KROWS
    bm = _na_bias_table(rpb, rows)
    dh = NA_DH

    def seq(n, col):
        return pl.BlockSpec((1, n, dh), lambda b, h: (b, 0, col // dh + h))

    return pl.pallas_call(
        functools.partial(_na_kernel, scale=dh ** -0.5, rows=rows),
        grid=(B, NA_HEADS),
        in_specs=[seq(T, COL_NQ), seq(T, COL_NK), seq(T, COL_NV), seq(Tc, COL_NK), seq(Tc, COL_NV),
                  pl.BlockSpec((1,) + bm.shape[1:], lambda b, h: (h, 0, 0, 0))],
        out_specs=pl.BlockSpec((1, T, dh), lambda b, h: (b, 0, h)),
        out_shape=jax.ShapeDtypeStruct((B, T, NA_HEADS * dh), BF16),
        compiler_params=_params("arbitrary", "arbitrary"),
        name="neighbourhood_attention",
    )(z, z, z, zc, zc, bm)


def _lru_kernel(g_ref, x_ref, gc_ref, xc_ref, cw_ref, cb_ref, wa_ref, ba_ref, wx_ref, bx_ref, lam_ref,
                y_ref, yc_ref, a_scr, u_scr, h_scr):
    row8 = lax.broadcasted_iota(jnp.int32, (SUBLANE, LANE), 0)

    def scan_pair(n_tok, h0f, h0b):
        nb = n_tok // SUBLANE

        def body(i, carry):
            hf, hb = carry
            rf = pl.multiple_of(i * SUBLANE, SUBLANE)
            rb = pl.multiple_of((nb - 1 - i) * SUBLANE, SUBLANE)
            A, U = a_scr[0, pl.ds(rf, SUBLANE), :], u_scr[0, pl.ds(rf, SUBLANE), :]
            Ab, Ub = a_scr[1, pl.ds(rb, SUBLANE), :], u_scr[1, pl.ds(rb, SUBLANE), :]
            for s in (1, 2, 4):
                m = row8 >= s
                U = jnp.where(m, A * pltpu.roll(U, s, 0) + U, U)
                A = jnp.where(m, A * pltpu.roll(A, s, 0), A)
                mb = row8 < SUBLANE - s
                Ub = jnp.where(mb, Ab * pltpu.roll(Ub, SUBLANE - s, 0) + Ub, Ub)
                Ab = jnp.where(mb, Ab * pltpu.roll(Ab, SUBLANE - s, 0), Ab)
            hbf = A * hf + U
            hbb = Ab * hb + Ub
            h_scr[0, pl.ds(rf, SUBLANE), :] = hbf
            h_scr[1, pl.ds(rb, SUBLANE), :] = hbb
            return hbf[SUBLANE - 1:SUBLANE, :], hbb[0:1, :]

        return lax.fori_loop(0, nb, body, (h0f, h0b), unroll=4)

    def run(gate_ref, zx_ref, out_ref, n_tok, h0f, h0b):
        x = zx_ref[0]
        t = lax.broadcasted_iota(jnp.int32, x.shape, 0)
        w = cw_ref[...]
        xc = (w[0:1] * jnp.where(t >= 2, pltpu.roll(x, 2, 0), 0.0)
              + w[1:2] * jnp.where(t >= 1, pltpu.roll(x, 1, 0), 0.0)
              + w[2:3] * x
              + w[3:4] * jnp.where(t < n_tok - 1, pltpu.roll(x, n_tok - 1, 0), 0.0)) + cb_ref[...]
        xcb = _bf(xc)
        for d in range(2):
            r = jax.nn.sigmoid(jnp.dot(xcb, wa_ref[d, 0], preferred_element_type=F32) + ba_ref[d:d + 1, :])
            i = jax.nn.sigmoid(jnp.dot(xcb, wx_ref[d, 0], preferred_element_type=F32) + bx_ref[d:d + 1, :])
            nl = -lam_ref[d:d + 1, :]
            softplus = jnp.maximum(nl, 0.0) + jnp.log1p(jnp.exp(-jnp.abs(nl)))
            log_a = -LRU_C * r * softplus
            a = jnp.exp(log_a)
            a_scr[d, 0:n_tok, :] = a
            u_scr[d, 0:n_tok, :] = jnp.sqrt(1.0 - jnp.exp(2.0 * log_a)) * i * xc
        hf, hb = scan_pair(n_tok, h0f, h0b)
        h = h_scr[0, 0:n_tok, :] + h_scr[1, 0:n_tok, :]
        out_ref[0] = (_gelu_tanh(gate_ref[0]) * h).astype(out_ref.dtype)
        return hf, hb

    zero = jnp.zeros((1, LANE), F32)
    hf, hb = run(gc_ref, xc_ref, yc_ref, xc_ref.shape[1], zero, zero)
    run(g_ref, x_ref, y_ref, x_ref.shape[1], hf, hb)


def _lru(z, zc, conv_w, conv_b, wa, ba, wx, bx, lam):
    B, T, _ = z.shape
    Tc = zc.shape[1]
    bw = LRU_BW

    def seq(n, col):
        return pl.BlockSpec((1, n, bw), lambda b, j: (b, 0, col // bw + j))

    vec2 = pl.BlockSpec((2, bw), lambda b, j: (0, j))
    mat = pl.BlockSpec((2, 1, bw, bw), lambda b, j: (0, j, 0, 0))
    out = lambda n: pl.BlockSpec((1, n, bw), lambda b, j: (b, 0, j))
    return pl.pallas_call(
        _lru_kernel,
        grid=(B, LRU_BLOCKS),
        in_specs=[seq(T, COL_LG), seq(T, COL_LX), seq(Tc, COL_LG), seq(Tc, COL_LX),
                  pl.BlockSpec((CONV_W, bw), lambda b, j: (0, j)), pl.BlockSpec((1, bw), lambda b, j: (0, j)),
                  mat, vec2, mat, vec2, vec2],
        out_specs=[out(T), out(Tc)],
        out_shape=[jax.ShapeDtypeStruct((B, T, LRU_WIDTH), BF16), jax.ShapeDtypeStruct((B, Tc, LRU_WIDTH), BF16)],
        scratch_shapes=[pltpu.VMEM((2, T, bw), F32)] * 3,
        compiler_params=_params("arbitrary", "arbitrary"),
        name="rg_lru",
    )(z, z, zc, zc, conv_w, conv_b.reshape(1, -1), _bf(wa), ba, _bf(wx), bx, lam)


def _gla_kernel(q_ref, k_ref, v_ref, g_ref, d_ref, qc_ref, kc_ref, vc_ref, gc_ref, dc_ref,
                wa_ref, ba_ref, ng_ref, y_ref, yc_ref, b_scr, s_scr):
    C = GLA_CHUNK
    ri = lax.broadcasted_iota(jnp.int32, (C, C), 0)
    ci = lax.broadcasted_iota(jnp.int32, (C, C), 1)
    keep = (ri >= ci, ri <= ci)
    tri = (keep[0].astype(F32), keep[1].astype(F32))
    scale = GLA_DK ** -0.5

    def chunk_rows(c):
        return pl.ds(pl.multiple_of(c * C, C), C)

    def run(refs, gate_ref, dec_ref, out_ref, n_tok, st_f, st_b):
        qr, kr, vr = refs
        n = n_tok // C
        dec = dec_ref[0]
        for d in range(2):
            x = jnp.dot(dec, wa_ref[d], preferred_element_type=F32, precision=HIGHEST) + ba_ref[d:d + 1, :]
            log_sig = jnp.minimum(x, 0.0) - jnp.log1p(jnp.exp(-jnp.abs(x)))
            b_scr[d, 0:n_tok, :] = log_sig * (1.0 / GLA_TAU)

        def cumulate(c, carry):
            rows = chunk_rows(c)
            for d in range(2):
                b_scr[d, rows, :] = jnp.dot(tri[d], b_scr[d, rows, :], preferred_element_type=F32,
                                            precision=HIGHEST)
            return carry

        lax.fori_loop(0, n, cumulate, 0, unroll=4)

        def advance(i, states):
            out = []
            for d, st in enumerate(states):
                c = i if d == 0 else n - 1 - i
                rows = chunk_rows(c)
                b = b_scr[d, rows, :]
                b_tot = b[C - 1:C, :] if d == 0 else b[0:1, :]
                ke = _bf(kr[0, rows, :] * jnp.exp(b_tot - b))
                s_scr[d, c] = st.astype(s_scr.dtype)
                out.append(st * jnp.exp(b_tot)
                           + lax.dot_general(_bf(vr[0, rows, :]), ke, TN_DIMS, preferred_element_type=F32))
            return tuple(out)

        st_f, st_b = lax.fori_loop(0, n, advance, (st_f, st_b), unroll=4)

        def emit(c, carry):
            rows = chunk_rows(c)
            q = qr[0, rows, :] * scale
            k = kr[0, rows, :]
            v = _bf(vr[0, rows, :])
            o = None
            for d in range(2):
                b = b_scr[d, rows, :]
                qd = _bf(q * jnp.exp(b))
                att = lax.dot_general(qd, _bf(k * jnp.exp(-b)), NT_DIMS, preferred_element_type=F32)
                att = jnp.where(keep[d], att, 0.0)
                od = (jnp.dot(_bf(att), v, preferred_element_type=F32)
                      + lax.dot_general(qd, s_scr[d, c], NT_DIMS, preferred_element_type=F32))
                o = od if o is None else o + od
            o = o * lax.rsqrt(jnp.mean(o * o, axis=-1, keepdims=True) + EPS) * ng_ref[...]
            gate = gate_ref[0, rows, :]
            out_ref[0, rows, :] = (o * (gate * jax.nn.sigmoid(gate))).astype(out_ref.dtype)
            return carry

        lax.fori_loop(0, n, emit, 0, unroll=2)
        return st_f, st_b

    zero = jnp.zeros((GLA_DV, GLA_DK), F32)
    st_f, st_b = run((qc_ref, kc_ref, vc_ref), gc_ref, dc_ref, yc_ref, qc_ref.shape[1], zero, zero)
    run((q_ref, k_ref, v_ref), g_ref, d_ref, y_ref, q_ref.shape[1], st_f, st_b)


def _gla(z, zc, wa2, ba, norm_g):
    B, T, _ = z.shape
    Tc = zc.shape[1]
    wa_p = jnp.zeros((2, LANE, GLA_HEADS * GLA_DK), F32)
    wa_p = wa_p.at[0, :GLA_RANK].set(wa2[0]).at[1, GLA_RANK:2 * GLA_RANK].set(wa2[1])

    def seq(n, col, w):
        return pl.BlockSpec((1, n, w), lambda b, h: (b, 0, col // w + h), pipeline_mode=pl.Buffered(1))

    def dec(n):
        return pl.BlockSpec((1, n, LANE), lambda b, h: (b, 0, COL_DEC // LANE), pipeline_mode=pl.Buffered(1))

    def ins(n):
        return [seq(n, COL_GQ, GLA_DK), seq(n, COL_GK, GLA_DK), seq(n, COL_GV, GLA_DV), seq(n, COL_GG, GLA_DV), dec(n)]

    out = lambda n: pl.BlockSpec((1, n, GLA_DV), lambda b, h: (b, 0, h))
    return pl.pallas_call(
        _gla_kernel,
        grid=(B, GLA_HEADS),
        in_specs=ins(T) + ins(Tc) + [
            pl.BlockSpec((2, LANE, GLA_DK), lambda b, h: (0, 0, h)),
            pl.BlockSpec((2, GLA_DK), lambda b, h: (0, h)),
            pl.BlockSpec((1, GLA_DV), lambda b, h: (0, 0)),
        ],
        out_specs=[out(T), out(Tc)],
        out_shape=[jax.ShapeDtypeStruct((B, T, BRANCH_W), BF16), jax.ShapeDtypeStruct((B, Tc, BRANCH_W), BF16)],
        scratch_shapes=[pltpu.VMEM((2, T, GLA_DK), F32), pltpu.VMEM((2, T // GLA_CHUNK, GLA_DV, GLA_DK), BF16)],
        compiler_params=_params("arbitrary", "arbitrary"),
        name="gla",
    )(z, z, z, z, z, zc, zc, zc, zc, zc, wa_p, ba, norm_g.reshape(1, -1))


def _permute_mix_weight(w_rows):
    o = np.cumsum((0,) + MIX_SIZES)
    gq, gk, gv, gg, af, ab, cq, ckv, kr, lg, lx, nq, nk, nv = [w_rows[o[i]:o[i + 1]] for i in range(len(MIX_SIZES))]
    half = MLA_ROPE // 2
    kr_sw = jnp.concatenate([kr[half:], kr[:half]], axis=0)
    pad = jnp.zeros((MIXP - (COL_DEC + 2 * GLA_RANK), w_rows.shape[1]), w_rows.dtype)
    return _bf(jnp.concatenate([gv, gg, lg, lx, nq, nk, nv, gq, gk, cq, ckv, kr, kr_sw, af, ab, pad], axis=0))


def kernel(x, c, ctx, c_ctx, w_ada, b_ada, norm1_g, norm2_g, w_in, gla_wa2, gla_ba, gla_norm_g, mla_q_norm_g, mla_w_uq, mla_kv_norm_g, mla_w_ukv, lru_conv_w, lru_conv_b, lru_wa, lru_ba, lru_wx, lru_bx, lru_lambda, na_rpb, w_branch, w_out, peer_wq, peer_keys, peer_u, peer_v, final_norm_g):
    B, T, D = x.shape
    Tc = ctx.shape[1]
    L = w_ada.shape[0]
    rope_l = _rope_tables(T, True)
    rope_c = _rope_tables(Tc, False)

    n_rows = -(-(B + 1) // SUBLANE) * SUBLANE
    cc = jnp.zeros((n_rows, D), F32).at[:B].set(c).at[B].set(c_ctx)
    mods = _ada(cc, w_ada, b_ada)

    xc = ctx
    pe_l = pe_c = gate_l = gate_c = None
    pu = _bf(peer_u)
    pv = _bf(peer_v)
    w_in_rows = _bf(jnp.swapaxes(w_in, 1, 2))
    wb = _bf(w_branch)
    wo = _bf(w_out)
    wq = jnp.swapaxes(_bf(peer_wq).reshape(L, D, PEER_HEADS, PEER_DQ), 1, 2)
    for l in range(L):
        update_ctx = l < L - 1
        ml = jnp.split(mods[l, :B], 6, axis=-1)
        mc = jnp.split(jnp.broadcast_to(mods[l, B], (B, 6 * D)), 6, axis=-1)
        w_mix = _permute_mix_weight(w_in_rows[l, :MIX_COLS])
        mla_wq, mla_wkv = _mla_weights(mla_w_uq[l], mla_w_ukv[l])

        xn, h = _norm(x, norm1_g[l], delta=pe_l, gate=gate_l, shift=ml[0], scale=ml[1])
        x = x if xn is None else xn
        xcn, hc = _norm(xc, norm1_g[l], delta=pe_c, gate=gate_c, shift=mc[0], scale=mc[1])
        xc = xc if xcn is None else xcn

        h2d = h.reshape(B * T, D)
        hc2d = hc.reshape(B * Tc, D)
        z = _matmul(h2d, w_mix, out_dtype=F32, b_rows=True).reshape(B, T, MIXP)
        zc = _matmul(hc2d, w_mix, out_dtype=F32, b_rows=True).reshape(B, Tc, MIXP)

        y_gla, yc_gla = _gla(z, zc, gla_wa2[l], gla_ba[l], gla_norm_g[l])
        y_lru, yc_lru = _lru(z, zc, lru_conv_w[l], lru_conv_b[l], lru_wa[l], lru_ba[l], lru_wx[l], lru_bx[l],
                             lru_lambda[l])
        ql, kl, vl = _mla_prep(z, *rope_l, mla_q_norm_g[l], mla_kv_norm_g[l], mla_wq, mla_wkv)
        qc, kc, vc = _mla_prep(zc, *rope_c, mla_q_norm_g[l], mla_kv_norm_g[l], mla_wq, mla_wkv)
        mla_args = dict(heads=MLA_HEADS, dq=MLA_DQP, dv=MLA_DV, scale=(MLA_NOPE + MLA_ROPE) ** -0.5)
        y_mla = _attention(ql, 0, kc, 0, vc, 0, kl, 0, vl, 0, **mla_args)
        y_na = _na(z, zc, na_rpb[l])
        ys_l = [y_gla, y_mla, y_lru, y_na]
        if update_ctx:
            yc_mla = _attention(qc, 0, kc, 0, vc, 0, **mla_args)
            yc_na = _attention(zc, COL_NQ, zc, COL_NK, zc, COL_NV, heads=NA_HEADS, dq=NA_DH, dv=NA_DH,
                               scale=NA_DH ** -0.5)
            ys_c = [yc_gla, yc_mla, yc_lru, yc_na]

        def channel_mix(xs, hs, ys_s, m, n_tok):
            M = B * n_tok
            gates = _matmul(hs, w_in_rows, out_dtype=BF16, act="sigmoid", layer=l, b_rows=True,
                            row0=MIX_COLS, n_out=N_BRANCH * D)
            mrg = _merge([y.reshape(M, BRANCH_W) for y in ys_s], gates, wb, l)
            xs = _matmul(mrg, wo, out_dtype=F32, res=xs.reshape(M, D), mod=m[2],
                         rows_per_batch=n_tok, layer=l).reshape(B, n_tok, D)
            _, h2 = _norm(xs, norm2_g[l], shift=m[3], scale=m[4])
            pe = _peer(h2.reshape(M, D), wq, peer_keys[l], pu, pv, l).reshape(B, n_tok, D)
            return xs, pe

        x, pe_l = channel_mix(x, h2d, ys_l, ml, T)
        gate_l = ml[5]
        if update_ctx:
            xc, pe_c = channel_mix(xc, hc2d, ys_c, mc, Tc)
            gate_c = mc[5]
        else:
            pe_c = gate_c = None
    _, out = _norm(x, final_norm_g, delta=pe_l, gate=gate_l, out_dtype=F32, emit_x=False)
    return out
```

```python
import functools

import jax
import jax.numpy as jnp
import numpy as np
from jax import lax
from jax.experimental import pallas as pl
from jax.experimental.pallas import tpu as pltpu

GRID_W = 64
EPS = 1e-6
NEG_INF = -1e30
N_BRANCH = 4
BRANCH_W = 1024
ROPE_THETA = 10000.0

GLA_HEADS = 4
GLA_DK = 128
GLA_DV = BRANCH_W // GLA_HEADS
GLA_RANK = 16
GLA_TAU = 16.0
GLA_CHUNK = 64

MLA_HEADS = 8
MLA_Q_RANK = 768
MLA_KV_RANK = 256
MLA_NOPE = 128
MLA_ROPE = 64
MLA_DV = BRANCH_W // MLA_HEADS
MLA_DQP = 256

LRU_WIDTH = BRANCH_W
LRU_BLOCKS = 8
LRU_BW = LRU_WIDTH // LRU_BLOCKS
CONV_W = 4
LRU_C = 8.0

NA_HEADS = 8
NA_DH = BRANCH_W // NA_HEADS
NA_WIN_R = 8
NA_WIN_C = 16
NA_QROWS = 8
NA_KROWS = 16

PEER_HEADS = 8
PEER_DQ = 256
PEER_TOPK = 16

MIX_SIZES = (
    GLA_HEADS * GLA_DK, GLA_HEADS * GLA_DK, GLA_HEADS * GLA_DV, GLA_HEADS * GLA_DV, GLA_RANK, GLA_RANK,
    MLA_Q_RANK, MLA_KV_RANK, MLA_ROPE, LRU_WIDTH, LRU_WIDTH,
    NA_HEADS * NA_DH, NA_HEADS * NA_DH, NA_HEADS * NA_DH,
)
MIX_COLS = sum(MIX_SIZES)

COL_GV, COL_GG, COL_LG, COL_LX, COL_NQ, COL_NK, COL_NV = 0, 1024, 2048, 3072, 4096, 5120, 6144
COL_GQ, COL_GK = 7168, 7680
COL_MQKV = 8192
COL_KR = 9216
COL_DEC = 9344
MIXP = 9728

V7X_VMEM_BYTES = 64 * 1024 * 1024
VMEM_LIMIT = V7X_VMEM_BYTES - 8 * 1024 * 1024
LANE = 128
SUBLANE = 8
ROW_ALIGN = 32

F32 = jnp.float32
BF16 = jnp.bfloat16
HIGHEST = lax.Precision.HIGHEST
LOG2_E = 1.4426950408889634
NT_DIMS = (((1,), (1,)), ((), ()))
TN_DIMS = (((0,), (0,)), ((), ()))


def _tile(n, pref, mult=SUBLANE):
    if n <= pref:
        return n
    t = (pref // mult) * mult
    while t > mult and n % t:
        t -= mult
    assert n % t == 0, (n, pref, mult)
    return t


def _params(*sem):
    return pltpu.CompilerParams(dimension_semantics=sem, vmem_limit_bytes=VMEM_LIMIT)


def _bf(x):
    return x.astype(BF16)


def _ada_kernel(c_ref, w_ref, b_ref, o_ref):
    cv = c_ref[...]
    a = cv * jax.nn.sigmoid(cv)
    o_ref[0] = jnp.dot(a, w_ref[0], preferred_element_type=F32, precision=HIGHEST) + b_ref[0]


def _ada(cc, w_ada, b_ada):
    L, D, W = w_ada.shape
    R = cc.shape[0]
    tn = _tile(W, 512, LANE)
    return pl.pallas_call(
        _ada_kernel,
        grid=(L, W // tn),
        in_specs=[
            pl.BlockSpec((R, D), lambda l, j: (0, 0)),
            pl.BlockSpec((1, D, tn), lambda l, j: (l, 0, j)),
            pl.BlockSpec((1, 1, tn), lambda l, j: (l, 0, j)),
        ],
        out_specs=pl.BlockSpec((1, R, tn), lambda l, j: (l, 0, j)),
        out_shape=jax.ShapeDtypeStruct((L, R, W), F32),
        compiler_params=_params("arbitrary", "arbitrary"),
        name="ada_mod",
    )(cc, w_ada, b_ada.reshape(L, 1, W))


def _norm_kernel(*refs, has_delta, modulate, emit_x):
    it = iter(refs)
    x_ref = next(it)
    if has_delta:
        d_ref, gate_ref = next(it), next(it)
    g_ref = next(it)
    if modulate:
        shift_ref, scale_ref = next(it), next(it)
    if emit_x:
        xo_ref = next(it)
    h_ref = next(it)
    x = x_ref[0]
    if has_delta:
        x = x + gate_ref[0] * d_ref[0]
    if emit_x:
        xo_ref[0] = x
    y = x * lax.rsqrt(jnp.mean(x * x, axis=-1, keepdims=True) + EPS)
    y = y * g_ref[...]
    if modulate:
        y = y * (1.0 + scale_ref[0]) + shift_ref[0]
    h_ref[0] = y.astype(h_ref.dtype)


def _norm(x, g, *, delta=None, gate=None, shift=None, scale=None, out_dtype=None, emit_x=True):
    out_dtype = BF16 if out_dtype is None else out_dtype
    B, T, D = x.shape
    tt = _tile(T, 256)
    has_delta = delta is not None
    emit_x = emit_x and has_delta
    modulate = shift is not None
    tok = pl.BlockSpec((1, tt, D), lambda b, t: (b, t, 0))
    vec = pl.BlockSpec((1, 1, D), lambda b, t: (b, 0, 0))
    args, specs = [x], [tok]
    if has_delta:
        args += [delta, gate.reshape(B, 1, D)]
        specs += [tok, vec]
    args.append(g.reshape(1, D))
    specs.append(pl.BlockSpec((1, D), lambda b, t: (0, 0)))
    if modulate:
        args += [shift.reshape(B, 1, D), scale.reshape(B, 1, D)]
        specs += [vec, vec]
    out_shape, out_specs = [], []
    if emit_x:
        out_shape.append(jax.ShapeDtypeStruct((B, T, D), F32))
        out_specs.append(tok)
    out_shape.append(jax.ShapeDtypeStruct((B, T, D), out_dtype))
    out_specs.append(tok)
    outs = pl.pallas_call(
        functools.partial(_norm_kernel, has_delta=has_delta, modulate=modulate, emit_x=emit_x),
        grid=(B, T // tt),
        in_specs=specs,
        out_specs=out_specs,
        out_shape=out_shape,
        compiler_params=_params("arbitrary", "arbitrary"),
        name="res_norm_mod",
    )(*args)
    if emit_x:
        return outs[0], outs[1]
    return None, outs[0]


def _mm_kernel(*refs, act, has_res, b_rows):
    if has_res:
        a_ref, b_ref, r_ref, m_ref, o_ref = refs
    else:
        a_ref, b_ref, o_ref = refs
    if b_rows:
        b = b_ref[0] if len(b_ref.shape) == 3 else b_ref[...]
        acc = lax.dot_general(a_ref[...], b, NT_DIMS, preferred_element_type=F32)
    else:
        acc = jnp.dot(a_ref[...], b_ref[...], preferred_element_type=F32)
    if act == "sigmoid":
        acc = jax.nn.sigmoid(acc)
    if has_res:
        acc = r_ref[...] + m_ref[0] * acc
    o_ref[...] = acc.astype(o_ref.dtype)


def _matmul(a, b, *, out_dtype, act=None, res=None, mod=None, rows_per_batch=None, layer=None, b_rows=False,
            row0=None, n_out=None, tm=1024, tn=1024):
    M, K = a.shape
    N = n_out if row0 is not None else (b.shape[-2] if b_rows else b.shape[-1])
    tm = _tile(rows_per_batch if rows_per_batch else M, tm)
    tn = _tile(N, tn, LANE)
    has_res = res is not None
    args = [a, b]
    blk, at = ((tn, K), lambda j: (j, 0)) if b_rows else ((K, tn), lambda j: (0, j))
    if layer is None:
        b_spec = pl.BlockSpec(blk, lambda i, j: at(j))
    else:
        b_spec = pl.BlockSpec((None,) + blk, lambda i, j: (layer,) + at(j))
    if row0 is not None:
        assert b_rows and layer is not None and row0 % ROW_ALIGN == 0
        b_spec = pl.BlockSpec((pl.Element(1), pl.Element(tn), pl.Element(K)),
                              lambda i, j: (layer, pl.multiple_of(row0 + j * tn, ROW_ALIGN), 0))
    specs = [pl.BlockSpec((tm, K), lambda i, j: (i, 0)), b_spec]
    if has_res:
        args += [res, mod.reshape(mod.shape[0], 1, N)]
        specs += [
            pl.BlockSpec((tm, tn), lambda i, j: (i, j)),
            pl.BlockSpec((1, 1, tn), lambda i, j: ((i * tm) // rows_per_batch, 0, j)),
        ]
    return pl.pallas_call(
        functools.partial(_mm_kernel, act=act, has_res=has_res, b_rows=b_rows),
        grid=(M // tm, N // tn),
        in_specs=specs,
        out_specs=pl.BlockSpec((tm, tn), lambda i, j: (i, j)),
        out_shape=jax.ShapeDtypeStruct((M, N), out_dtype),
        compiler_params=_params("arbitrary", "arbitrary"),
        name="matmul",
    )(*args)


def _merge_kernel(y0, y1, y2, y3, g0, g1, g2, g3, w_ref, o_ref):
    acc = None
    for i, (y, g) in enumerate(((y0, g0), (y1, g1), (y2, g2), (y3, g3))):
        p = g[...].astype(F32) * jnp.dot(y[...], w_ref[i], preferred_element_type=F32)
        acc = p if acc is None else acc + p
    o_ref[...] = acc.astype(o_ref.dtype)


def _merge(ys, gates, w_branch, layer):
    M = ys[0].shape[0]
    D = w_branch.shape[3]
    tm = _tile(M, 1024)
    tn = _tile(D, 512, LANE)
    nj = D // tn
    y_spec = pl.BlockSpec((tm, BRANCH_W), lambda i, j: (i, 0))
    g_specs = [pl.BlockSpec((tm, tn), functools.partial(lambda i, j, br: (i, br * nj + j), br=br))
               for br in range(N_BRANCH)]
    return pl.pallas_call(
        _merge_kernel,
        grid=(M // tm, nj),
        in_specs=[y_spec] * N_BRANCH + g_specs + [
            pl.BlockSpec((None, N_BRANCH, BRANCH_W, tn), lambda i, j: (layer, 0, 0, j))],
        out_specs=pl.BlockSpec((tm, tn), lambda i, j: (i, j)),
        out_shape=jax.ShapeDtypeStruct((M, D), BF16),
        compiler_params=_params("arbitrary", "arbitrary"),
        name="merge",
    )(*ys, gates, gates, gates, gates, w_branch)


def _topk_rows(s, k):
    n = s.shape[0]
    iota = lax.broadcasted_iota(jnp.int32, s.shape, 0).astype(F32)
    vals, idxs = [], []
    for _ in range(k):
        m = jnp.max(s, axis=0, keepdims=True)
        am = jnp.min(jnp.where(s == m, iota, float(n)), axis=0, keepdims=True)
        vals.append(m)
        idxs.append(am)
        s = jnp.where(iota == am, -jnp.inf, s)
    return jnp.concatenate(vals, axis=0), jnp.concatenate(idxs, axis=0)


def _candidate_rows(x1, x2):
    K, m = x1.shape
    r1, r2, ok = [], [], []
    a = 0
    while K // (a + 1) > 1:
        nb = K // (a + 1)
        width = -(-nb // SUBLANE) * SUBLANE
        r1.append(jnp.broadcast_to(x1[a:a + 1], (width, m)))
        r2.append(x2[0:width])
        ok.append(lax.broadcasted_iota(jnp.int32, (width, m), 0) < nb)
        a += 1
    assert (K - a) % SUBLANE == 0
    r1.append(x1[a:K])
    r2.append(jnp.broadcast_to(x2[0:1], (K - a, m)))
    ok.append(jnp.full((K - a, m), True))
    return jnp.concatenate(r1, axis=0), jnp.concatenate(r2, axis=0), jnp.concatenate(ok, axis=0)


def _peer_route_kernel(h_ref, wq_ref, k_ref, g_ref, e1_ref, e2_ref, qa_scr, qb_scr):
    H = wq_ref.shape[0]
    assert H % 2 == 0
    half = PEER_DQ // 2
    K = PEER_TOPK

    def project(h, dst):
        dst[...] = jnp.dot(h_ref[...], wq_ref[h], preferred_element_type=F32)

    def route(h, q_scr):
        rows = pl.ds(pl.multiple_of(h * K, K), K)
        for c0 in range(0, h_ref.shape[0], LANE):
            cols = slice(c0, c0 + LANE)
            q = q_scr[cols, :]
            v1, i1 = _topk_rows(lax.dot_general(k_ref[h, 0], q[:, :half], NT_DIMS, preferred_element_type=F32,
                                                precision=HIGHEST), K)
            v2, i2 = _topk_rows(lax.dot_general(k_ref[h, 1], q[:, half:], NT_DIMS, preferred_element_type=F32,
                                                precision=HIGHEST), K)
            c1, c2, ok = _candidate_rows(v1, v2)
            top, pos = _topk_rows(jnp.where(ok, c1 + c2, -jnp.inf), K)
            p = jnp.exp(top - top[0:1])
            g_ref[rows, cols] = p / jnp.sum(p, axis=0, keepdims=True)
            id1, id2, _ = _candidate_rows(i1, i2)
            row = lax.broadcasted_iota(jnp.int32, id1.shape, 0).astype(F32)
            e1, e2 = [], []
            for r in range(K):
                sel = row == pos[r:r + 1]
                e1.append(jnp.sum(jnp.where(sel, id1, 0.0), axis=0, keepdims=True))
                e2.append(jnp.sum(jnp.where(sel, id2, 0.0), axis=0, keepdims=True))
            e1_ref[rows, cols] = jnp.concatenate(e1, axis=0)
            e2_ref[rows, cols] = jnp.concatenate(e2, axis=0)

    project(0, qa_scr)

    def head_pair(i, carry):
        h = 2 * i
        project(h + 1, qb_scr)
        route(h, qa_scr)
        project(jnp.minimum(h + 2, H - 1), qa_scr)
        route(h + 1, qb_scr)
        return carry

    lax.fori_loop(0, H // 2, head_pair, 0)


def _peer_route(h2, wq_heads, keys, layer):
    M, D = h2.shape
    H, _, nk, half = keys.shape
    tm = _tile(M, 512, LANE)
    out = jax.ShapeDtypeStruct((H * PEER_TOPK, M), F32)
    o_spec = pl.BlockSpec((H * PEER_TOPK, tm), lambda i: (0, i))
    return pl.pallas_call(
        _peer_route_kernel,
        grid=(M // tm,),
        in_specs=[
            pl.BlockSpec((tm, D), lambda i: (i, 0)),
            pl.BlockSpec((None, H, D, PEER_DQ), lambda i: (layer, 0, 0, 0), pipeline_mode=pl.Buffered(1)),
            pl.BlockSpec((H, 2, nk, half), lambda i: (0, 0, 0, 0)),
        ],
        out_specs=[o_spec, o_spec, o_spec],
        out_shape=[out, out, out],
        scratch_shapes=[pltpu.VMEM((tm, PEER_DQ), F32)] * 2,
        compiler_params=_params("arbitrary"),
        name="peer_route",
    )(h2, wq_heads, keys)


def _peer_w_kernel(g_ref, e1_ref, e2_ref, o_ref, gt_ref, e1t_ref, e2t_ref, w_scr, *, nk):
    tm = o_ref.shape[0]
    gt_ref[...] = g_ref[...].T
    e1t_ref[...] = e1_ref[...].T
    e2t_ref[...] = e2_ref[...].T
    key_iota = lax.broadcasted_iota(jnp.int32, (nk, g_ref.shape[0]), 0).astype(F32)

    def body(t, carry):
        row = pl.ds(t, 1)
        a = jnp.where(e1t_ref[row, :] == key_iota, gt_ref[row, :], 0.0).astype(BF16)
        b = jnp.where(e2t_ref[row, :] == key_iota, 1.0, 0.0).astype(BF16)
        w_scr[t] = lax.dot_general(a, b, NT_DIMS, preferred_element_type=F32)
        return carry

    lax.fori_loop(0, tm, body, 0, unroll=32)
    tb = _tile(tm, 64)
    for t0 in range(0, tm, tb):
        planes = jnp.swapaxes(w_scr[t0:t0 + tb], 0, 1)
        for j in range(nk):
            o_ref[t0:t0 + tb, j * nk:(j + 1) * nk] = planes[j].astype(o_ref.dtype)


def _peer_w(g, e1, e2, nk):
    S, M = g.shape
    tm = _tile(M, 256, LANE)
    spec = pl.BlockSpec((S, tm), lambda i: (0, i))
    return pl.pallas_call(
        functools.partial(_peer_w_kernel, nk=nk),
        grid=(M // tm,),
        in_specs=[spec, spec, spec],
        out_specs=pl.BlockSpec((tm, nk * nk), lambda i: (i, 0)),
        out_shape=jax.ShapeDtypeStruct((M, nk * nk), BF16),
        scratch_shapes=[pltpu.VMEM((tm, S), F32)] * 3 + [pltpu.VMEM((tm, nk, nk), F32)],
        compiler_params=_params("arbitrary"),
        name="peer_route_weights",
    )(g, e1, e2)


def _gelu_tanh(x):
    return 0.5 * x * (1.0 + jnp.tanh(0.7978845608028654 * (x + 0.044715 * (x * x * x))))


def _peer_dense_kernel(h_ref, u_ref, v_ref, w_ref, o_ref):
    @pl.when(pl.program_id(1) == 0)
    def _():
        o_ref[...] = jnp.zeros_like(o_ref)

    s = lax.dot_general(h_ref[...], u_ref[...], NT_DIMS, preferred_element_type=F32)
    a = (_gelu_tanh(s) * w_ref[...].astype(F32)).astype(BF16)
    o_ref[...] += jnp.dot(a, v_ref[...], preferred_element_type=F32)


def _peer_dense(h2, u, v, w, layer):
    M, D = h2.shape
    E = u.shape[1]
    tm = _tile(M, 1024)
    te = _tile(E, 512, LANE)
    once = pl.Buffered(1)
    return pl.pallas_call(
        _peer_dense_kernel,
        grid=(M // tm, E // te),
        in_specs=[
            pl.BlockSpec((tm, D), lambda i, e: (i, 0), pipeline_mode=once),
            pl.BlockSpec((None, te, D), lambda i, e: (layer, e, 0)),
            pl.BlockSpec((None, te, D), lambda i, e: (layer, e, 0)),
            pl.BlockSpec((tm, te), lambda i, e: (i, e)),
        ],
        out_specs=pl.BlockSpec((tm, D), lambda i, e: (i, 0), pipeline_mode=once),
        out_shape=jax.ShapeDtypeStruct((M, D), F32),
        compiler_params=_params("arbitrary", "arbitrary"),
        name="peer_dense",
    )(h2, u, v, w)


def _peer(h2, wq, keys, u, v, layer):
    nk = keys.shape[2]
    g, e1, e2 = _peer_route(h2, wq, keys, layer)
    return _peer_dense(h2, u, v, _peer_w(g, e1, e2, nk), layer)


def _attn_kernel(*refs, scale, two):
    if two:
        q_ref, k1_ref, v1_ref, k2_ref, v2_ref, o_ref = refs
    else:
        q_ref, k1_ref, v1_ref, o_ref = refs
    tq = q_ref.shape[1]
    tg = _tile(tq, 256)
    for r0 in range(0, tq, tg):
        rows = slice(r0, r0 + tg)
        q = _bf(q_ref[0, rows, :].astype(F32) * (scale * LOG2_E))
        s1 = lax.dot_general(q, _bf(k1_ref[0]), NT_DIMS, preferred_element_type=F32)
        m = jnp.max(s1, axis=-1, keepdims=True)
        if two:
            s2 = lax.dot_general(q, _bf(k2_ref[0]), NT_DIMS, preferred_element_type=F32)
            m = jnp.maximum(m, jnp.max(s2, axis=-1, keepdims=True))
        p1 = jnp.exp2(s1 - m)
        l = jnp.sum(p1, axis=-1, keepdims=True)
        o = jnp.dot(_bf(p1), _bf(v1_ref[0]), preferred_element_type=F32)
        if two:
            p2 = jnp.exp2(s2 - m)
            l = l + jnp.sum(p2, axis=-1, keepdims=True)
            o = o + jnp.dot(_bf(p2), _bf(v2_ref[0]), preferred_element_type=F32)
        o_ref[0, rows, :] = (o / l).astype(o_ref.dtype)


def _attention(q, qcol, k1, k1col, v1, v1col, k2=None, k2col=0, v2=None, v2col=0, *, heads, dq, dv, scale):
    B, Tq, _ = q.shape
    tq = _tile(Tq, 512)
    two = k2 is not None

    def spec(arr, col, w, tiled):
        n = arr.shape[1]
        if tiled:
            return pl.BlockSpec((1, tq, w), lambda b, h, t: (b, t, col // w + h))
        return pl.BlockSpec((1, n, w), lambda b, h, t: (b, 0, col // w + h))

    args = [q, k1, v1]
    specs = [spec(q, qcol, dq, True), spec(k1, k1col, dq, False), spec(v1, v1col, dv, False)]
    if two:
        args += [k2, v2]
        specs += [spec(k2, k2col, dq, False), spec(v2, v2col, dv, False)]
    return pl.pallas_call(
        functools.partial(_attn_kernel, scale=scale, two=two),
        grid=(B, heads, Tq // tq),
        in_specs=specs,
        out_specs=pl.BlockSpec((1, tq, dv), lambda b, h, t: (b, t, h)),
        out_shape=jax.ShapeDtypeStruct((B, Tq, heads * dv), BF16),
        compiler_params=_params("arbitrary", "arbitrary", "arbitrary"),
        name="attention",
    )(*args)


def _mla_prep_kernel(z_ref, zr_ref, c_ref, s_ref, gq_ref, gkv_ref, wq_ref, wkv_ref, q_out, k_out, v_out):
    z = z_ref[0]

    def rms(x, g):
        return x * lax.rsqrt(jnp.mean(x * x, axis=-1, keepdims=True) + EPS) * g

    q = jnp.dot(_bf(rms(z[:, :MLA_Q_RANK], gq_ref[...])), wq_ref[...], preferred_element_type=F32)
    kv = jnp.dot(_bf(rms(z[:, MLA_Q_RANK:], gkv_ref[...])), wkv_ref[...], preferred_element_type=F32)
    cos, sin = c_ref[...], s_ref[...]

    def rope(x):
        return x * cos + pltpu.roll(x, MLA_ROPE, 1) * sin

    kr = rope(zr_ref[0]).astype(k_out.dtype)
    hv = MLA_HEADS * MLA_NOPE
    for h in range(MLA_HEADS):
        lo = h * MLA_DQP
        q_out[0, :, lo:lo + MLA_NOPE] = q[:, lo:lo + MLA_NOPE].astype(q_out.dtype)
        q_out[0, :, lo + MLA_NOPE:lo + MLA_DQP] = rope(q[:, lo + MLA_NOPE:lo + MLA_DQP]).astype(q_out.dtype)
        k_out[0, :, lo:lo + MLA_NOPE] = kv[:, h * MLA_NOPE:(h + 1) * MLA_NOPE].astype(k_out.dtype)
        k_out[0, :, lo + MLA_NOPE:lo + MLA_DQP] = kr
    v_out[0] = kv[:, hv:].astype(v_out.dtype)


def _mla_prep(z, cos_t, sin_t, gq, gkv, wq_p, wkv_p):
    B, T, _ = z.shape
    tt = _tile(T, 256)
    wq_w = MLA_HEADS * MLA_DQP
    wkv_w = MLA_HEADS * (MLA_NOPE + MLA_DV)
    cw = MLA_Q_RANK + MLA_KV_RANK
    tab = pl.BlockSpec((tt, LANE), lambda b, t: (t, 0))
    return pl.pallas_call(
        _mla_prep_kernel,
        grid=(B, T // tt),
        in_specs=[
            pl.BlockSpec((1, tt, cw), lambda b, t: (b, t, COL_MQKV // cw)),
            pl.BlockSpec((1, tt, LANE), lambda b, t: (b, t, COL_KR // LANE)),
            tab, tab,
            pl.BlockSpec((1, MLA_Q_RANK), lambda b, t: (0, 0)),
            pl.BlockSpec((1, MLA_KV_RANK), lambda b, t: (0, 0)),
            pl.BlockSpec((MLA_Q_RANK, wq_w), lambda b, t: (0, 0)),
            pl.BlockSpec((MLA_KV_RANK, wkv_w), lambda b, t: (0, 0)),
        ],
        out_specs=[
            pl.BlockSpec((1, tt, wq_w), lambda b, t: (b, t, 0)),
            pl.BlockSpec((1, tt, wq_w), lambda b, t: (b, t, 0)),
            pl.BlockSpec((1, tt, MLA_HEADS * MLA_DV), lambda b, t: (b, t, 0)),
        ],
        out_shape=[
            jax.ShapeDtypeStruct((B, T, wq_w), BF16),
            jax.ShapeDtypeStruct((B, T, wq_w), BF16),
            jax.ShapeDtypeStruct((B, T, MLA_HEADS * MLA_DV), BF16),
        ],
        compiler_params=_params("arbitrary", "arbitrary"),
        name="mla_prep",
    )(z, z, cos_t, sin_t, gq.reshape(1, -1), gkv.reshape(1, -1), wq_p, wkv_p)


def _mla_weights(w_uq, w_ukv):
    half = MLA_ROPE // 2
    wq = w_uq.reshape(MLA_Q_RANK, MLA_HEADS, MLA_NOPE + MLA_ROPE)
    pe = wq[..., MLA_NOPE:]
    pe_sw = jnp.concatenate([pe[..., half:], pe[..., :half]], axis=-1)
    wq_p = jnp.concatenate([wq[..., :MLA_NOPE], pe, pe_sw], axis=-1).reshape(MLA_Q_RANK, MLA_HEADS * MLA_DQP)
    wkv = w_ukv.reshape(MLA_KV_RANK, MLA_HEADS, MLA_NOPE + MLA_DV)
    wkv_p = jnp.concatenate([wkv[..., :MLA_NOPE].reshape(MLA_KV_RANK, -1),
                             wkv[..., MLA_NOPE:].reshape(MLA_KV_RANK, -1)], axis=-1)
    return _bf(wq_p), _bf(wkv_p)


def _rope_tables(n_tok, with_pos):
    n_freq = MLA_ROPE // 4
    zeros = jnp.zeros((n_tok, LANE - MLA_ROPE), F32)
    if not with_pos:
        return (jnp.concatenate([jnp.ones((n_tok, MLA_ROPE), F32), zeros], axis=1),
                jnp.zeros((n_tok, LANE), F32))
    t = jnp.arange(n_tok, dtype=jnp.int32)
    row = (t // GRID_W).astype(F32)
    col = (t % GRID_W).astype(F32)
    inv = ROPE_THETA ** (-jnp.arange(n_freq, dtype=F32) / n_freq)
    ang = jnp.concatenate([row[:, None] * inv, col[:, None] * inv], axis=-1)
    cos, sin = jnp.cos(ang), jnp.sin(ang)
    return (jnp.concatenate([cos, cos, zeros], axis=1), jnp.concatenate([-sin, sin, zeros], axis=1))


def _na_kernel(q_ref, k_ref, v_ref, kc_ref, vc_ref, bm_ref, o_ref, *, scale, rows):
    nblk = rows // NA_QROWS
    nq = NA_QROWS * GRID_W
    nkw = NA_KROWS * GRID_W
    kc = _bf(kc_ref[0])
    vc = _bf(vc_ref[0])

    def body(blk, carry):
        q0 = pl.multiple_of(blk * nq, nq)
        kb = jnp.clip(NA_QROWS * blk - NA_WIN_R // 2, 0, rows - NA_KROWS)
        k0 = pl.multiple_of(kb * GRID_W, (NA_WIN_R // 2) * GRID_W)
        pat = jnp.where(blk == 0, 0, jnp.where(blk == nblk - 1, 2, 1))
        q = _bf(q_ref[0, pl.ds(q0, nq), :] * scale)
        kw = _bf(k_ref[0, pl.ds(k0, nkw), :])
        vw = _bf(v_ref[0, pl.ds(k0, nkw), :])
        bm = bm_ref[0, pat]
        s = lax.dot_general(q, kw, NT_DIMS, preferred_element_type=F32)
        s = jnp.where(bm > 0.5 * NEG_INF, s + bm, NEG_INF)
        sc = lax.dot_general(q, kc, NT_DIMS, preferred_element_type=F32)
        m = jnp.maximum(jnp.max(s, axis=-1, keepdims=True), jnp.max(sc, axis=-1, keepdims=True))
        p = jnp.exp(s - m)
        pc = jnp.exp(sc - m)
        l = jnp.sum(p, axis=-1, keepdims=True) + jnp.sum(pc, axis=-1, keepdims=True)
        o = jnp.dot(_bf(p), vw, preferred_element_type=F32) + jnp.dot(_bf(pc), vc, preferred_element_type=F32)
        o_ref[0, pl.ds(q0, nq), :] = (o / l).astype(o_ref.dtype)
        return carry

    lax.fori_loop(0, nblk, body, 0, unroll=2)


def _na_bias_table(rpb, rows):
    nblk = rows // NA_QROWS
    col = np.arange(GRID_W)
    c_start = np.clip(col - NA_WIN_C // 2, 0, GRID_W - NA_WIN_C)
    in_win = (col[None, :] >= c_start[:, None]) & (col[None, :] < c_start[:, None] + NA_WIN_C)
    dc = np.clip(col[None, :] - col[:, None] + NA_WIN_C - 1, 0, 2 * NA_WIN_C - 2)
    dc_onehot = (dc[:, :, None] == np.arange(2 * NA_WIN_C - 1)).astype(np.float32)
    toeplitz = jnp.einsum("hab,qkb->haqk", rpb.astype(F32), dc_onehot, precision=HIGHEST)
    n_dr = 2 * NA_WIN_R - 1
    sel = np.zeros((3, NA_QROWS, NA_KROWS, n_dr), np.float32)
    valid = np.zeros((3, NA_QROWS, NA_KROWS), bool)
    for p, blk in enumerate((0, min(1, nblk - 1), nblk - 1)):
        kb = int(np.clip(NA_QROWS * blk - NA_WIN_R // 2, 0, rows - NA_KROWS))
        for rq in range(NA_QROWS):
            r = NA_QROWS * blk + rq
            r_start = int(np.clip(r - NA_WIN_R // 2, 0, rows - NA_WIN_R))
            for rk in range(NA_KROWS):
                kr = kb + rk
                if r_start <= kr < r_start + NA_WIN_R:
                    sel[p, rq, rk, kr - r + NA_WIN_R - 1] = 1.0
                    valid[p, rq, rk] = True
    bias = jnp.einsum("prka,haqc->hprqkc", sel, toeplitz, precision=HIGHEST)
    mask = valid[None, :, :, None, :, None] & in_win[None, None, None, :, None, :]
    H = rpb.shape[0]
    return jnp.where(mask, bias, NEG_INF).reshape(H, 3, NA_QROWS * GRID_W, NA_KROWS * GRID_W)


def _na(z, zc, rpb):
    B, T, _ = z.shape
    Tc = zc.shape[1]
    rows = T // GRID_W
    assert rows % NA_QROWS == 0 and rows >= NA_KROWS
    bm = _na_bias_table(rpb, rows)
    dh = NA_DH

    def seq(n, col):
        return pl.BlockSpec((1, n, dh), lambda b, h: (b, 0, col // dh + h))

    return pl.pallas_call(
        functools.partial(_na_kernel, scale=dh ** -0.5, rows=rows),
        grid=(B, NA_HEADS),
        in_specs=[seq(T, COL_NQ), seq(T, COL_NK), seq(T, COL_NV), seq(Tc, COL_NK), seq(Tc, COL_NV),
                  pl.BlockSpec((1,) + bm.shape[1:], lambda b, h: (h, 0, 0, 0))],
        out_specs=pl.BlockSpec((1, T, dh), lambda b, h: (b, 0, h)),
        out_shape=jax.ShapeDtypeStruct((B, T, NA_HEADS * dh), BF16),
        compiler_params=_params("arbitrary", "arbitrary"),
        name="neighbourhood_attention",
    )(z, z, z, zc, zc, bm)


def _lru_kernel(g_ref, x_ref, gc_ref, xc_ref, cw_ref, cb_ref, wa_ref, ba_ref, wx_ref, bx_ref, lam_ref,
                y_ref, yc_ref, a_scr, u_scr, h_scr):
    row8 = lax.broadcasted_iota(jnp.int32, (SUBLANE, LANE), 0)

    def scan_pair(n_tok, h0f, h0b):
        nb = n_tok // SUBLANE

        def body(i, carry):
            hf, hb = carry
            rf = pl.multiple_of(i * SUBLANE, SUBLANE)
            rb = pl.multiple_of((nb - 1 - i) * SUBLANE, SUBLANE)
            A, U = a_scr[0, pl.ds(rf, SUBLANE), :], u_scr[0, pl.ds(rf, SUBLANE), :]
            Ab, Ub = a_scr[1, pl.ds(rb, SUBLANE), :], u_scr[1, pl.ds(rb, SUBLANE), :]
            for s in (1, 2, 4):
                m = row8 >= s
                U = jnp.where(m, A * pltpu.roll(U, s, 0) + U, U)
                A = jnp.where(m, A * pltpu.roll(A, s, 0), A)
                mb = row8 < SUBLANE - s
                Ub = jnp.where(mb, Ab * pltpu.roll(Ub, SUBLANE - s, 0) + Ub, Ub)
                Ab = jnp.where(mb, Ab * pltpu.roll(Ab, SUBLANE - s, 0), Ab)
            hbf = A * hf + U
            hbb = Ab * hb + Ub
            h_scr[0, pl.ds(rf, SUBLANE), :] = hbf
            h_scr[1, pl.ds(rb, SUBLANE), :] = hbb
            return hbf[SUBLANE - 1:SUBLANE, :], hbb[0:1, :]

        return lax.fori_loop(0, nb, body, (h0f, h0b), unroll=4)

    def run(gate_ref, zx_ref, out_ref, n_tok, h0f, h0b):
        x = zx_ref[0]
        t = lax.broadcasted_iota(jnp.int32, x.shape, 0)
        w = cw_ref[...]
        xc = (w[0:1] * jnp.where(t >= 2, pltpu.roll(x, 2, 0), 0.0)
              + w[1:2] * jnp.where(t >= 1, pltpu.roll(x, 1, 0), 0.0)
              + w[2:3] * x
              + w[3:4] * jnp.where(t < n_tok - 1, pltpu.roll(x, n_tok - 1, 0), 0.0)) + cb_ref[...]
        xcb = _bf(xc)
        for d in range(2):
            r = jax.nn.sigmoid(jnp.dot(xcb, wa_ref[d, 0], preferred_element_type=F32) + ba_ref[d:d + 1, :])
            i = jax.nn.sigmoid(jnp.dot(xcb, wx_ref[d, 0], preferred_element_type=F32) + bx_ref[d:d + 1, :])
            nl = -lam_ref[d:d + 1, :]
            softplus = jnp.maximum(nl, 0.0) + jnp.log1p(jnp.exp(-jnp.abs(nl)))
            log_a = -LRU_C * r * softplus
            a = jnp.exp(log_a)
            a_scr[d, 0:n_tok, :] = a
            u_scr[d, 0:n_tok, :] = jnp.sqrt(1.0 - jnp.exp(2.0 * log_a)) * i * xc
        hf, hb = scan_pair(n_tok, h0f, h0b)
        h = h_scr[0, 0:n_tok, :] + h_scr[1, 0:n_tok, :]
        out_ref[0] = (_gelu_tanh(gate_ref[0]) * h).astype(out_ref.dtype)
        return hf, hb

    zero = jnp.zeros((1, LANE), F32)
    hf, hb = run(gc_ref, xc_ref, yc_ref, xc_ref.shape[1], zero, zero)
    run(g_ref, x_ref, y_ref, x_ref.shape[1], hf, hb)


def _lru(z, zc, conv_w, conv_b, wa, ba, wx, bx, lam):
    B, T, _ = z.shape
    Tc = zc.shape[1]
    bw = LRU_BW

    def seq(n, col):
        return pl.BlockSpec((1, n, bw), lambda b, j: (b, 0, col // bw + j))

    vec2 = pl.BlockSpec((2, bw), lambda b, j: (0, j))
    mat = pl.BlockSpec((2, 1, bw, bw), lambda b, j: (0, j, 0, 0))
    out = lambda n: pl.BlockSpec((1, n, bw), lambda b, j: (b, 0, j))
    return pl.pallas_call(
        _lru_kernel,
        grid=(B, LRU_BLOCKS),
        in_specs=[seq(T, COL_LG), seq(T, COL_LX), seq(Tc, COL_LG), seq(Tc, COL_LX),
                  pl.BlockSpec((CONV_W, bw), lambda b, j: (0, j)), pl.BlockSpec((1, bw), lambda b, j: (0, j)),
                  mat, vec2, mat, vec2, vec2],
        out_specs=[out(T), out(Tc)],
        out_shape=[jax.ShapeDtypeStruct((B, T, LRU_WIDTH), BF16), jax.ShapeDtypeStruct((B, Tc, LRU_WIDTH), BF16)],
        scratch_shapes=[pltpu.VMEM((2, T, bw), F32)] * 3,
        compiler_params=_params("arbitrary", "arbitrary"),
        name="rg_lru",
    )(z, z, zc, zc, conv_w, conv_b.reshape(1, -1), _bf(wa), ba, _bf(wx), bx, lam)


def _gla_kernel(q_ref, k_ref, v_ref, g_ref, d_ref, qc_ref, kc_ref, vc_ref, gc_ref, dc_ref,
                wa_ref, ba_ref, ng_ref, y_ref, yc_ref, b_scr, s_scr):
    C = GLA_CHUNK
    ri = lax.broadcasted_iota(jnp.int32, (C, C), 0)
    ci = lax.broadcasted_iota(jnp.int32, (C, C), 1)
    keep = (ri >= ci, ri <= ci)
    tri = (keep[0].astype(F32), keep[1].astype(F32))
    scale = GLA_DK ** -0.5

    def chunk_rows(c):
        return pl.ds(pl.multiple_of(c * C, C), C)

    def run(refs, gate_ref, dec_ref, out_ref, n_tok, st_f, st_b):
        qr, kr, vr = refs
        n = n_tok // C
        dec = dec_ref[0]
        for d in range(2):
            x = jnp.dot(dec, wa_ref[d], preferred_element_type=F32, precision=HIGHEST) + ba_ref[d:d + 1, :]
            log_sig = jnp.minimum(x, 0.0) - jnp.log1p(jnp.exp(-jnp.abs(x)))
            b_scr[d, 0:n_tok, :] = log_sig * (1.0 / GLA_TAU)

        def cumulate(c, carry):
            rows = chunk_rows(c)
            for d in range(2):
                b_scr[d, rows, :] = jnp.dot(tri[d], b_scr[d, rows, :], preferred_element_type=F32,
                                            precision=HIGHEST)
            return carry

        lax.fori_loop(0, n, cumulate, 0, unroll=4)

        def advance(i, states):
            out = []
            for d, st in enumerate(states):
                c = i if d == 0 else n - 1 - i
                rows = chunk_rows(c)
                b = b_scr[d, rows, :]
                b_tot = b[C - 1:C, :] if d == 0 else b[0:1, :]
                ke = _bf(kr[0, rows, :] * jnp.exp(b_tot - b))
                s_scr[d, c] = st.astype(s_scr.dtype)
                out.append(st * jnp.exp(b_tot)
                           + lax.dot_general(_bf(vr[0, rows, :]), ke, TN_DIMS, preferred_element_type=F32))
            return tuple(out)

        st_f, st_b = lax.fori_loop(0, n, advance, (st_f, st_b), unroll=4)

        def emit(c, carry):
            rows = chunk_rows(c)
            q = qr[0, rows, :] * scale
            k = kr[0, rows, :]
            v = _bf(vr[0, rows, :])
            o = None
            for d in range(2):
                b = b_scr[d, rows, :]
                qd = _bf(q * jnp.exp(b))
                att = lax.dot_general(qd, _bf(k * jnp.exp(-b)), NT_DIMS, preferred_element_type=F32)
                att = jnp.where(keep[d], att, 0.0)
                od = (jnp.dot(_bf(att), v, preferred_element_type=F32)
                      + lax.dot_general(qd, s_scr[d, c], NT_DIMS, preferred_element_type=F32))
                o = od if o is None else o + od
            o = o * lax.rsqrt(jnp.mean(o * o, axis=-1, keepdims=True) + EPS) * ng_ref[...]
            gate = gate_ref[0, rows, :]
            out_ref[0, rows, :] = (o * (gate * jax.nn.sigmoid(gate))).astype(out_ref.dtype)
            return carry

        lax.fori_loop(0, n, emit, 0, unroll=4)
        return st_f, st_b

    zero = jnp.zeros((GLA_DV, GLA_DK), F32)
    st_f, st_b = run((qc_ref, kc_ref, vc_ref), gc_ref, dc_ref, yc_ref, qc_ref.shape[1], zero, zero)
    run((q_ref, k_ref, v_ref), g_ref, d_ref, y_ref, q_ref.shape[1], st_f, st_b)


def _gla(z, zc, wa2, ba, norm_g):
    B, T, _ = z.shape
    Tc = zc.shape[1]
    wa_p = jnp.zeros((2, LANE, GLA_HEADS * GLA_DK), F32)
    wa_p = wa_p.at[0, :GLA_RANK].set(wa2[0]).at[1, GLA_RANK:2 * GLA_RANK].set(wa2[1])

    def seq(n, col, w):
        return pl.BlockSpec((1, n, w), lambda b, h: (b, 0, col // w + h), pipeline_mode=pl.Buffered(1))

    def dec(n):
        return pl.BlockSpec((1, n, LANE), lambda b, h: (b, 0, COL_DEC // LANE), pipeline_mode=pl.Buffered(1))

    def ins(n):
        return [seq(n, COL_GQ, GLA_DK), seq(n, COL_GK, GLA_DK), seq(n, COL_GV, GLA_DV), seq(n, COL_GG, GLA_DV), dec(n)]

    out = lambda n: pl.BlockSpec((1, n, GLA_DV), lambda b, h: (b, 0, h))
    return pl.pallas_call(
        _gla_kernel,
        grid=(B, GLA_HEADS),
        in_specs=ins(T) + ins(Tc) + [
            pl.BlockSpec((2, LANE, GLA_DK), lambda b, h: (0, 0, h)),
            pl.BlockSpec((2, GLA_DK), lambda b, h: (0, h)),
            pl.BlockSpec((1, GLA_DV), lambda b, h: (0, 0)),
        ],
        out_specs=[out(T), out(Tc)],
        out_shape=[jax.ShapeDtypeStruct((B, T, BRANCH_W), BF16), jax.ShapeDtypeStruct((B, Tc, BRANCH_W), BF16)],
        scratch_shapes=[pltpu.VMEM((2, T, GLA_DK), F32), pltpu.VMEM((2, T // GLA_CHUNK, GLA_DV, GLA_DK), BF16)],
        compiler_params=_params("arbitrary", "arbitrary"),
        name="gla",
    )(z, z, z, z, z, zc, zc, zc, zc, zc, wa_p, ba, norm_g.reshape(1, -1))


def _permute_mix_weight(w_rows):
    o = np.cumsum((0,) + MIX_SIZES)
    gq, gk, gv, gg, af, ab, cq, ckv, kr, lg, lx, nq, nk, nv = [w_rows[o[i]:o[i + 1]] for i in range(len(MIX_SIZES))]
    half = MLA_ROPE // 2
    kr_sw = jnp.concatenate([kr[half:], kr[:half]], axis=0)
    pad = jnp.zeros((MIXP - (COL_DEC + 2 * GLA_RANK), w_rows.shape[1]), w_rows.dtype)
    return _bf(jnp.concatenate([gv, gg, lg, lx, nq, nk, nv, gq, gk, cq, ckv, kr, kr_sw, af, ab, pad], axis=0))


def kernel(x, c, ctx, c_ctx, w_ada, b_ada, norm1_g, norm2_g, w_in, gla_wa2, gla_ba, gla_norm_g, mla_q_norm_g, mla_w_uq, mla_kv_norm_g, mla_w_ukv, lru_conv_w, lru_conv_b, lru_wa, lru_ba, lru_wx, lru_bx, lru_lambda, na_rpb, w_branch, w_out, peer_wq, peer_keys, peer_u, peer_v, final_norm_g):
    B, T, D = x.shape
    Tc = ctx.shape[1]
    L = w_ada.shape[0]
    rope_l = _rope_tables(T, True)
    rope_c = _rope_tables(Tc, False)

    n_rows = -(-(B + 1) // SUBLANE) * SUBLANE
    cc = jnp.zeros((n_rows, D), F32).at[:B].set(c).at[B].set(c_ctx)
    mods = _ada(cc, w_ada, b_ada)

    xc = ctx
    pe_l = pe_c = gate_l = gate_c = None
    pu = _bf(peer_u)
    pv = _bf(peer_v)
    w_in_rows = _bf(jnp.swapaxes(w_in, 1, 2))
    wb = _bf(w_branch)
    wo = _bf(w_out)
    wq = jnp.swapaxes(_bf(peer_wq).reshape(L, D, PEER_HEADS, PEER_DQ), 1, 2)
    for l in range(L):
        update_ctx = l < L - 1
        ml = jnp.split(mods[l, :B], 6, axis=-1)
        mc = jnp.split(jnp.broadcast_to(mods[l, B], (B, 6 * D)), 6, axis=-1)
        w_mix = _permute_mix_weight(w_in_rows[l, :MIX_COLS])
        mla_wq, mla_wkv = _mla_weights(mla_w_uq[l], mla_w_ukv[l])

        xn, h = _norm(x, norm1_g[l], delta=pe_l, gate=gate_l, shift=ml[0], scale=ml[1])
        x = x if xn is None else xn
        xcn, hc = _norm(xc, norm1_g[l], delta=pe_c, gate=gate_c, shift=mc[0], scale=mc[1])
        xc = xc if xcn is None else xcn

        h2d = h.reshape(B * T, D)
        hc2d = hc.reshape(B * Tc, D)
        z = _matmul(h2d, w_mix, out_dtype=F32, b_rows=True).reshape(B, T, MIXP)
        zc = _matmul(hc2d, w_mix, out_dtype=F32, b_rows=True).reshape(B, Tc, MIXP)

        y_gla, yc_gla = _gla(z, zc, gla_wa2[l], gla_ba[l], gla_norm_g[l])
        y_lru, yc_lru = _lru(z, zc, lru_conv_w[l], lru_conv_b[l], lru_wa[l], lru_ba[l], lru_wx[l], lru_bx[l],
                             lru_lambda[l])
        ql, kl, vl = _mla_prep(z, *rope_l, mla_q_norm_g[l], mla_kv_norm_g[l], mla_wq, mla_wkv)
        qc, kc, vc = _mla_prep(zc, *rope_c, mla_q_norm_g[l], mla_kv_norm_g[l], mla_wq, mla_wkv)
        mla_args = dict(heads=MLA_HEADS, dq=MLA_DQP, dv=MLA_DV, scale=(MLA_NOPE + MLA_ROPE) ** -0.5)
        y_mla = _attention(ql, 0, kc, 0, vc, 0, kl, 0, vl, 0, **mla_args)
        y_na = _na(z, zc, na_rpb[l])
        ys_l = [y_gla, y_mla, y_lru, y_na]
        if update_ctx:
            yc_mla = _attention(qc, 0, kc, 0, vc, 0, **mla_args)
            yc_na = _attention(zc, COL_NQ, zc, COL_NK, zc, COL_NV, heads=NA_HEADS, dq=NA_DH, dv=NA_DH,
                               scale=NA_DH ** -0.5)
            ys_c = [yc_gla, yc_mla, yc_lru, yc_na]

        def channel_mix(xs, hs, ys_s, m, n_tok):
            M = B * n_tok
            gates = _matmul(hs, w_in_rows, out_dtype=BF16, act="sigmoid", layer=l, b_rows=True,
                            row0=MIX_COLS, n_out=N_BRANCH * D)
            mrg = _merge([y.reshape(M, BRANCH_W) for y in ys_s], gates, wb, l)
            xs = _matmul(mrg, wo, out_dtype=F32, res=xs.reshape(M, D), mod=m[2],
                         rows_per_batch=n_tok, layer=l).reshape(B, n_tok, D)
            _, h2 = _norm(xs, norm2_g[l], shift=m[3], scale=m[4])
            pe = _peer(h2.reshape(M, D), wq, peer_keys[l], pu, pv, l).reshape(B, n_tok, D)
            return xs, pe

        x, pe_l = channel_mix(x, h2d, ys_l, ml, T)
        gate_l = ml[5]
        if update_ctx:
            xc, pe_c = channel_mix(xc, hc2d, ys_c, mc, Tc)
            gate_c = mc[5]
        else:
            pe_c = gate_c = None
    _, out = _norm(x, final_norm_g, delta=pe_l, gate=gate_l, out_dtype=F32, emit_x=False)
    return out
```

```python
import functools

import jax
import jax.numpy as jnp
import numpy as np
from jax import lax
from jax.experimental import pallas as pl
from jax.experimental.pallas import tpu as pltpu

GRID_W = 64
EPS = 1e-6
NEG_INF = -1e30
N_BRANCH = 4
BRANCH_W = 1024
ROPE_THETA = 10000.0

GLA_HEADS = 4
GLA_DK = 128
GLA_DV = BRANCH_W // GLA_HEADS
GLA_RANK = 16
GLA_TAU = 16.0
GLA_CHUNK = 64

MLA_HEADS = 8
MLA_Q_RANK = 768
MLA_KV_RANK = 256
MLA_NOPE = 128
MLA_ROPE = 64
MLA_DV = BRANCH_W // MLA_HEADS
MLA_DQP = 256

LRU_WIDTH = BRANCH_W
LRU_BLOCKS = 8
LRU_BW = LRU_WIDTH // LRU_BLOCKS
CONV_W = 4
LRU_C = 8.0

NA_HEADS = 8
NA_DH = BRANCH_W // NA_HEADS
NA_WIN_R = 8
NA_WIN_C = 16
NA_QROWS = 8
NA_KROWS = 16

PEER_HEADS = 8
PEER_DQ = 256
PEER_TOPK = 16

MIX_SIZES = (
    GLA_HEADS * GLA_DK, GLA_HEADS * GLA_DK, GLA_HEADS * GLA_DV, GLA_HEADS * GLA_DV, GLA_RANK, GLA_RANK,
    MLA_Q_RANK, MLA_KV_RANK, MLA_ROPE, LRU_WIDTH, LRU_WIDTH,
    NA_HEADS * NA_DH, NA_HEADS * NA_DH, NA_HEADS * NA_DH,
)
MIX_COLS = sum(MIX_SIZES)

COL_GV, COL_GG, COL_LG, COL_LX, COL_NQ, COL_NK, COL_NV = 0, 1024, 2048, 3072, 4096, 5120, 6144
COL_GQ, COL_GK = 7168, 7680
COL_MQKV = 8192
COL_KR = 9216
COL_DEC = 9344
MIXP = 9728

V7X_VMEM_BYTES = 64 * 1024 * 1024
VMEM_LIMIT = V7X_VMEM_BYTES - 8 * 1024 * 1024
LANE = 128
SUBLANE = 8
ROW_ALIGN = 32

F32 = jnp.float32
BF16 = jnp.bfloat16
HIGHEST = lax.Precision.HIGHEST
LOG2_E = 1.4426950408889634
NT_DIMS = (((1,), (1,)), ((), ()))
TN_DIMS = (((0,), (0,)), ((), ()))


def _tile(n, pref, mult=SUBLANE):
    if n <= pref:
        return n
    t = (pref // mult) * mult
    while t > mult and n % t:
        t -= mult
    assert n % t == 0, (n, pref, mult)
    return t


def _params(*sem):
    return pltpu.CompilerParams(dimension_semantics=sem, vmem_limit_bytes=VMEM_LIMIT)


def _bf(x):
    return x.astype(BF16)


def _ada_kernel(c_ref, w_ref, b_ref, o_ref):
    cv = c_ref[...]
    a = cv * jax.nn.sigmoid(cv)
    o_ref[0] = jnp.dot(a, w_ref[0], preferred_element_type=F32, precision=HIGHEST) + b_ref[0]


def _ada(cc, w_ada, b_ada):
    L, D, W = w_ada.shape
    R = cc.shape[0]
    tn = _tile(W, 512, LANE)
    return pl.pallas_call(
        _ada_kernel,
        grid=(L, W // tn),
        in_specs=[
            pl.BlockSpec((R, D), lambda l, j: (0, 0)),
            pl.BlockSpec((1, D, tn), lambda l, j: (l, 0, j)),
            pl.BlockSpec((1, 1, tn), lambda l, j: (l, 0, j)),
        ],
        out_specs=pl.BlockSpec((1, R, tn), lambda l, j: (l, 0, j)),
        out_shape=jax.ShapeDtypeStruct((L, R, W), F32),
        compiler_params=_params("arbitrary", "arbitrary"),
        name="ada_mod",
    )(cc, w_ada, b_ada.reshape(L, 1, W))


def _norm_kernel(*refs, has_delta, modulate, emit_x):
    it = iter(refs)
    x_ref = next(it)
    if has_delta:
        d_ref, gate_ref = next(it), next(it)
    g_ref = next(it)
    if modulate:
        shift_ref, scale_ref = next(it), next(it)
    if emit_x:
        xo_ref = next(it)
    h_ref = next(it)
    x = x_ref[0]
    if has_delta:
        x = x + gate_ref[0] * d_ref[0]
    if emit_x:
        xo_ref[0] = x
    y = x * lax.rsqrt(jnp.mean(x * x, axis=-1, keepdims=True) + EPS)
    y = y * g_ref[...]
    if modulate:
        y = y * (1.0 + scale_ref[0]) + shift_ref[0]
    h_ref[0] = y.astype(h_ref.dtype)


def _norm(x, g, *, delta=None, gate=None, shift=None, scale=None, out_dtype=None, emit_x=True):
    out_dtype = BF16 if out_dtype is None else out_dtype
    B, T, D = x.shape
    tt = _tile(T, 256)
    has_delta = delta is not None
    emit_x = emit_x and has_delta
    modulate = shift is not None
    tok = pl.BlockSpec((1, tt, D), lambda b, t: (b, t, 0))
    vec = pl.BlockSpec((1, 1, D), lambda b, t: (b, 0, 0))
    args, specs = [x], [tok]
    if has_delta:
        args += [delta, gate.reshape(B, 1, D)]
        specs += [tok, vec]
    args.append(g.reshape(1, D))
    specs.append(pl.BlockSpec((1, D), lambda b, t: (0, 0)))
    if modulate:
        args += [shift.reshape(B, 1, D), scale.reshape(B, 1, D)]
        specs += [vec, vec]
    out_shape, out_specs = [], []
    if emit_x:
        out_shape.append(jax.ShapeDtypeStruct((B, T, D), F32))
        out_specs.append(tok)
    out_shape.append(jax.ShapeDtypeStruct((B, T, D), out_dtype))
    out_specs.append(tok)
    outs = pl.pallas_call(
        functools.partial(_norm_kernel, has_delta=has_delta, modulate=modulate, emit_x=emit_x),
        grid=(B, T // tt),
        in_specs=specs,
        out_specs=out_specs,
        out_shape=out_shape,
        compiler_params=_params("arbitrary", "arbitrary"),
        name="res_norm_mod",
    )(*args)
    if emit_x:
        return outs[0], outs[1]
    return None, outs[0]


def _mm_kernel(*refs, act, has_res, b_rows):
    if has_res:
        a_ref, b_ref, r_ref, m_ref, o_ref = refs
    else:
        a_ref, b_ref, o_ref = refs
    if b_rows:
        b = b_ref[0] if len(b_ref.shape) == 3 else b_ref[...]
        acc = lax.dot_general(a_ref[...], b, NT_DIMS, preferred_element_type=F32)
    else:
        acc = jnp.dot(a_ref[...], b_ref[...], preferred_element_type=F32)
    if act == "sigmoid":
        acc = jax.nn.sigmoid(acc)
    if has_res:
        acc = r_ref[...] + m_ref[0] * acc
    o_ref[...] = acc.astype(o_ref.dtype)


def _matmul(a, b, *, out_dtype, act=None, res=None, mod=None, rows_per_batch=None, layer=None, b_rows=False,
            row0=None, n_out=None, tm=1024, tn=1024):
    M, K = a.shape
    N = n_out if row0 is not None else (b.shape[-2] if b_rows else b.shape[-1])
    tm = _tile(rows_per_batch if rows_per_batch else M, tm)
    tn = _tile(N, tn, LANE)
    has_res = res is not None
    args = [a, b]
    blk, at = ((tn, K), lambda j: (j, 0)) if b_rows else ((K, tn), lambda j: (0, j))
    if layer is None:
        b_spec = pl.BlockSpec(blk, lambda i, j: at(j))
    else:
        b_spec = pl.BlockSpec((None,) + blk, lambda i, j: (layer,) + at(j))
    if row0 is not None:
        assert b_rows and layer is not None and row0 % ROW_ALIGN == 0
        b_spec = pl.BlockSpec((pl.Element(1), pl.Element(tn), pl.Element(K)),
                              lambda i, j: (layer, pl.multiple_of(row0 + j * tn, ROW_ALIGN), 0))
    specs = [pl.BlockSpec((tm, K), lambda i, j: (i, 0)), b_spec]
    if has_res:
        args += [res, mod.reshape(mod.shape[0], 1, N)]
        specs += [
            pl.BlockSpec((tm, tn), lambda i, j: (i, j)),
            pl.BlockSpec((1, 1, tn), lambda i, j: ((i * tm) // rows_per_batch, 0, j)),
        ]
    return pl.pallas_call(
        functools.partial(_mm_kernel, act=act, has_res=has_res, b_rows=b_rows),
        grid=(M // tm, N // tn),
        in_specs=specs,
        out_specs=pl.BlockSpec((tm, tn), lambda i, j: (i, j)),
        out_shape=jax.ShapeDtypeStruct((M, N), out_dtype),
        compiler_params=_params("arbitrary", "arbitrary"),
        name="matmul",
    )(*args)


def _merge_kernel(y0, y1, y2, y3, g0, g1, g2, g3, w_ref, o_ref):
    acc = None
    for i, (y, g) in enumerate(((y0, g0), (y1, g1), (y2, g2), (y3, g3))):
        p = g[...].astype(F32) * jnp.dot(y[...], w_ref[i], preferred_element_type=F32)
        acc = p if acc is None else acc + p
    o_ref[...] = acc.astype(o_ref.dtype)


def _merge(ys, gates, w_branch, layer):
    M = ys[0].shape[0]
    D = w_branch.shape[3]
    tm = _tile(M, 1024)
    tn = _tile(D, 512, LANE)
    nj = D // tn
    y_spec = pl.BlockSpec((tm, BRANCH_W), lambda i, j: (i, 0))
    g_specs = [pl.BlockSpec((tm, tn), functools.partial(lambda i, j, br: (i, br * nj + j), br=br))
               for br in range(N_BRANCH)]
    return pl.pallas_call(
        _merge_kernel,
        grid=(M // tm, nj),
        in_specs=[y_spec] * N_BRANCH + g_specs + [
            pl.BlockSpec((None, N_BRANCH, BRANCH_W, tn), lambda i, j: (layer, 0, 0, j))],
        out_specs=pl.BlockSpec((tm, tn), lambda i, j: (i, j)),
        out_shape=jax.ShapeDtypeStruct((M, D), BF16),
        compiler_params=_params("arbitrary", "arbitrary"),
        name="merge",
    )(*ys, gates, gates, gates, gates, w_branch)


def _topk_rows(s, k):
    n = s.shape[0]
    iota = lax.broadcasted_iota(jnp.int32, s.shape, 0).astype(F32)
    vals, idxs = [], []
    for _ in range(k):
        m = jnp.max(s, axis=0, keepdims=True)
        am = jnp.min(jnp.where(s == m, iota, float(n)), axis=0, keepdims=True)
        vals.append(m)
        idxs.append(am)
        s = jnp.where(iota == am, -jnp.inf, s)
    return jnp.concatenate(vals, axis=0), jnp.concatenate(idxs, axis=0)


def _candidate_rows(x1, x2):
    K, m = x1.shape
    r1, r2, ok = [], [], []
    a = 0
    while K // (a + 1) > 1:
        nb = K // (a + 1)
        width = -(-nb // SUBLANE) * SUBLANE
        r1.append(jnp.broadcast_to(x1[a:a + 1], (width, m)))
        r2.append(x2[0:width])
        ok.append(lax.broadcasted_iota(jnp.int32, (width, m), 0) < nb)
        a += 1
    assert (K - a) % SUBLANE == 0
    r1.append(x1[a:K])
    r2.append(jnp.broadcast_to(x2[0:1], (K - a, m)))
    ok.append(jnp.full((K - a, m), True))
    return jnp.concatenate(r1, axis=0), jnp.concatenate(r2, axis=0), jnp.concatenate(ok, axis=0)


def _peer_route_kernel(h_ref, wq_ref, k_ref, g_ref, e1_ref, e2_ref, qa_scr, qb_scr):
    H = wq_ref.shape[0]
    assert H % 2 == 0
    half = PEER_DQ // 2
    K = PEER_TOPK

    def project(h, dst):
        dst[...] = jnp.dot(h_ref[...], wq_ref[h], preferred_element_type=F32)

    def route(h, q_scr):
        rows = pl.ds(pl.multiple_of(h * K, K), K)
        for c0 in range(0, h_ref.shape[0], LANE):
            cols = slice(c0, c0 + LANE)
            q = q_scr[cols, :]
            v1, i1 = _topk_rows(lax.dot_general(k_ref[h, 0], q[:, :half], NT_DIMS, preferred_element_type=F32,
                                                precision=HIGHEST), K)
            v2, i2 = _topk_rows(lax.dot_general(k_ref[h, 1], q[:, half:], NT_DIMS, preferred_element_type=F32,
                                                precision=HIGHEST), K)
            c1, c2, ok = _candidate_rows(v1, v2)
            top, pos = _topk_rows(jnp.where(ok, c1 + c2, -jnp.inf), K)
            p = jnp.exp(top - top[0:1])
            g_ref[rows, cols] = p / jnp.sum(p, axis=0, keepdims=True)
            id1, id2, _ = _candidate_rows(i1, i2)
            row = lax.broadcasted_iota(jnp.int32, id1.shape, 0).astype(F32)
            e1, e2 = [], []
            for r in range(K):
                sel = row == pos[r:r + 1]
                e1.append(jnp.sum(jnp.where(sel, id1, 0.0), axis=0, keepdims=True))
                e2.append(jnp.sum(jnp.where(sel, id2, 0.0), axis=0, keepdims=True))
            e1_ref[rows, cols] = jnp.concatenate(e1, axis=0)
            e2_ref[rows, cols] = jnp.concatenate(e2, axis=0)

    project(0, qa_scr)

    def head_pair(i, carry):
        h = 2 * i
        project(h + 1, qb_scr)
        route(h, qa_scr)
        project(jnp.minimum(h + 2, H - 1), qa_scr)
        route(h + 1, qb_scr)
        return carry

    lax.fori_loop(0, H // 2, head_pair, 0)


def _peer_route(h2, wq_heads, keys, layer):
    M, D = h2.shape
    H, _, nk, half = keys.shape
    tm = _tile(M, 512, LANE)
    out = jax.ShapeDtypeStruct((H * PEER_TOPK, M), F32)
    o_spec = pl.BlockSpec((H * PEER_TOPK, tm), lambda i: (0, i))
    return pl.pallas_call(
        _peer_route_kernel,
        grid=(M // tm,),
        in_specs=[
            pl.BlockSpec((tm, D), lambda i: (i, 0)),
            pl.BlockSpec((None, H, D, PEER_DQ), lambda i: (layer, 0, 0, 0), pipeline_mode=pl.Buffered(1)),
            pl.BlockSpec((H, 2, nk, half), lambda i: (0, 0, 0, 0)),
        ],
        out_specs=[o_spec, o_spec, o_spec],
        out_shape=[out, out, out],
        scratch_shapes=[pltpu.VMEM((tm, PEER_DQ), F32)] * 2,
        compiler_params=_params("arbitrary"),
        name="peer_route",
    )(h2, wq_heads, keys)


def _peer_w_kernel(g_ref, e1_ref, e2_ref, o_ref, gt_ref, e1t_ref, e2t_ref, w_scr, *, nk):
    tm = o_ref.shape[0]
    gt_ref[...] = g_ref[...].T
    e1t_ref[...] = e1_ref[...].T
    e2t_ref[...] = e2_ref[...].T
    key_iota = lax.broadcasted_iota(jnp.int32, (nk, g_ref.shape[0]), 0).astype(F32)

    def body(t, carry):
        row = pl.ds(t, 1)
        a = jnp.where(e1t_ref[row, :] == key_iota, gt_ref[row, :], 0.0).astype(BF16)
        b = jnp.where(e2t_ref[row, :] == key_iota, 1.0, 0.0).astype(BF16)
        w_scr[t] = lax.dot_general(a, b, NT_DIMS, preferred_element_type=F32)
        return carry

    lax.fori_loop(0, tm, body, 0, unroll=64)
    tb = _tile(tm, 64)
    for t0 in range(0, tm, tb):
        planes = jnp.swapaxes(w_scr[t0:t0 + tb], 0, 1)
        for j in range(nk):
            o_ref[t0:t0 + tb, j * nk:(j + 1) * nk] = planes[j].astype(o_ref.dtype)


def _peer_w(g, e1, e2, nk):
    S, M = g.shape
    tm = _tile(M, 256, LANE)
    spec = pl.BlockSpec((S, tm), lambda i: (0, i))
    return pl.pallas_call(
        functools.partial(_peer_w_kernel, nk=nk),
        grid=(M // tm,),
        in_specs=[spec, spec, spec],
        out_specs=pl.BlockSpec((tm, nk * nk), lambda i: (i, 0)),
        out_shape=jax.ShapeDtypeStruct((M, nk * nk), BF16),
        scratch_shapes=[pltpu.VMEM((tm, S), F32)] * 3 + [pltpu.VMEM((tm, nk, nk), F32)],
        compiler_params=_params("arbitrary"),
        name="peer_route_weights",
    )(g, e1, e2)


def _gelu_tanh(x):
    return 0.5 * x * (1.0 + jnp.tanh(0.7978845608028654 * (x + 0.044715 * (x * x * x))))


def _peer_dense_kernel(h_ref, u_ref, v_ref, w_ref, o_ref):
    @pl.when(pl.program_id(1) == 0)
    def _():
        o_ref[...] = jnp.zeros_like(o_ref)

    s = lax.dot_general(h_ref[...], u_ref[...], NT_DIMS, preferred_element_type=F32)
    a = (_gelu_tanh(s) * w_ref[...].astype(F32)).astype(BF16)
    o_ref[...] += jnp.dot(a, v_ref[...], preferred_element_type=F32)


def _peer_dense(h2, u, v, w, layer):
    M, D = h2.shape
    E = u.shape[1]
    tm = _tile(M, 1024)
    te = _tile(E, 512, LANE)
    once = pl.Buffered(1)
    return pl.pallas_call(
        _peer_dense_kernel,
        grid=(M // tm, E // te),
        in_specs=[
            pl.BlockSpec((tm, D), lambda i, e: (i, 0), pipeline_mode=once),
            pl.BlockSpec((None, te, D), lambda i, e: (layer, e, 0)),
            pl.BlockSpec((None, te, D), lambda i, e: (layer, e, 0)),
            pl.BlockSpec((tm, te), lambda i, e: (i, e)),
        ],
        out_specs=pl.BlockSpec((tm, D), lambda i, e: (i, 0), pipeline_mode=once),
        out_shape=jax.ShapeDtypeStruct((M, D), F32),
        compiler_params=_params("arbitrary", "arbitrary"),
        name="peer_dense",
    )(h2, u, v, w)


def _peer(h2, wq, keys, u, v, layer):
    nk = keys.shape[2]
    g, e1, e2 = _peer_route(h2, wq, keys, layer)
    return _peer_dense(h2, u, v, _peer_w(g, e1, e2, nk), layer)


def _attn_kernel(*refs, scale, two):
    if two:
        q_ref, k1_ref, v1_ref, k2_ref, v2_ref, o_ref = refs
    else:
        q_ref, k1_ref, v1_ref, o_ref = refs
    tq = q_ref.shape[1]
    tg = _tile(tq, 256)
    for r0 in range(0, tq, tg):
        rows = slice(r0, r0 + tg)
        q = _bf(q_ref[0, rows, :].astype(F32) * (scale * LOG2_E))
        s1 = lax.dot_general(q, _bf(k1_ref[0]), NT_DIMS, preferred_element_type=F32)
        m = jnp.max(s1, axis=-1, keepdims=True)
        if two:
            s2 = lax.dot_general(q, _bf(k2_ref[0]), NT_DIMS, preferred_element_type=F32)
            m = jnp.maximum(m, jnp.max(s2, axis=-1, keepdims=True))
        p1 = jnp.exp2(s1 - m)
        l = jnp.sum(p1, axis=-1, keepdims=True)
        o = jnp.dot(_bf(p1), _bf(v1_ref[0]), preferred_element_type=F32)
        if two:
            p2 = jnp.exp2(s2 - m)
            l = l + jnp.sum(p2, axis=-1, keepdims=True)
            o = o + jnp.dot(_bf(p2), _bf(v2_ref[0]), preferred_element_type=F32)
        o_ref[0, rows, :] = (o / l).astype(o_ref.dtype)


def _attention(q, qcol, k1, k1col, v1, v1col, k2=None, k2col=0, v2=None, v2col=0, *, heads, dq, dv, scale):
    B, Tq, _ = q.shape
    tq = _tile(Tq, 512)
    two = k2 is not None

    def spec(arr, col, w, tiled):
        n = arr.shape[1]
        if tiled:
            return pl.BlockSpec((1, tq, w), lambda b, h, t: (b, t, col // w + h))
        return pl.BlockSpec((1, n, w), lambda b, h, t: (b, 0, col // w + h))

    args = [q, k1, v1]
    specs = [spec(q, qcol, dq, True), spec(k1, k1col, dq, False), spec(v1, v1col, dv, False)]
    if two:
        args += [k2, v2]
        specs += [spec(k2, k2col, dq, False), spec(v2, v2col, dv, False)]
    return pl.pallas_call(
        functools.partial(_attn_kernel, scale=scale, two=two),
        grid=(B, heads, Tq // tq),
        in_specs=specs,
        out_specs=pl.BlockSpec((1, tq, dv), lambda b, h, t: (b, t, h)),
        out_shape=jax.ShapeDtypeStruct((B, Tq, heads * dv), BF16),
        compiler_params=_params("arbitrary", "arbitrary", "arbitrary"),
        name="attention",
    )(*args)


def _mla_prep_kernel(z_ref, zr_ref, c_ref, s_ref, gq_ref, gkv_ref, wq_ref, wkv_ref, q_out, k_out, v_out):
    z = z_ref[0]

    def rms(x, g):
        return x * lax.rsqrt(jnp.mean(x * x, axis=-1, keepdims=True) + EPS) * g

    q = jnp.dot(_bf(rms(z[:, :MLA_Q_RANK], gq_ref[...])), wq_ref[...], preferred_element_type=F32)
    kv = jnp.dot(_bf(rms(z[:, MLA_Q_RANK:], gkv_ref[...])), wkv_ref[...], preferred_element_type=F32)
    cos, sin = c_ref[...], s_ref[...]

    def rope(x):
        return x * cos + pltpu.roll(x, MLA_ROPE, 1) * sin

    kr = rope(zr_ref[0]).astype(k_out.dtype)
    hv = MLA_HEADS * MLA_NOPE
    for h in range(MLA_HEADS):
        lo = h * MLA_DQP
        q_out[0, :, lo:lo + MLA_NOPE] = q[:, lo:lo + MLA_NOPE].astype(q_out.dtype)
        q_out[0, :, lo + MLA_NOPE:lo + MLA_DQP] = rope(q[:, lo + MLA_NOPE:lo + MLA_DQP]).astype(q_out.dtype)
        k_out[0, :, lo:lo + MLA_NOPE] = kv[:, h * MLA_NOPE:(h + 1) * MLA_NOPE].astype(k_out.dtype)
        k_out[0, :, lo + MLA_NOPE:lo + MLA_DQP] = kr
    v_out[0] = kv[:, hv:].astype(v_out.dtype)


def _mla_prep(z, cos_t, sin_t, gq, gkv, wq_p, wkv_p):
    B, T, _ = z.shape
    tt = _tile(T, 256)
    wq_w = MLA_HEADS * MLA_DQP
    wkv_w = MLA_HEADS * (MLA_NOPE + MLA_DV)
    cw = MLA_Q_RANK + MLA_KV_RANK
    tab = pl.BlockSpec((tt, LANE), lambda b, t: (t, 0))
    return pl.pallas_call(
        _mla_prep_kernel,
        grid=(B, T // tt),
        in_specs=[
            pl.BlockSpec((1, tt, cw), lambda b, t: (b, t, COL_MQKV // cw)),
            pl.BlockSpec((1, tt, LANE), lambda b, t: (b, t, COL_KR // LANE)),
            tab, tab,
            pl.BlockSpec((1, MLA_Q_RANK), lambda b, t: (0, 0)),
            pl.BlockSpec((1, MLA_KV_RANK), lambda b, t: (0, 0)),
            pl.BlockSpec((MLA_Q_RANK, wq_w), lambda b, t: (0, 0)),
            pl.BlockSpec((MLA_KV_RANK, wkv_w), lambda b, t: (0, 0)),
        ],
        out_specs=[
            pl.BlockSpec((1, tt, wq_w), lambda b, t: (b, t, 0)),
            pl.BlockSpec((1, tt, wq_w), lambda b, t: (b, t, 0)),
            pl.BlockSpec((1, tt, MLA_HEADS * MLA_DV), lambda b, t: (b, t, 0)),
        ],
        out_shape=[
            jax.ShapeDtypeStruct((B, T, wq_w), BF16),
            jax.ShapeDtypeStruct((B, T, wq_w), BF16),
            jax.ShapeDtypeStruct((B, T, MLA_HEADS * MLA_DV), BF16),
        ],
        compiler_params=_params("arbitrary", "arbitrary"),
        name="mla_prep",
    )(z, z, cos_t, sin_t, gq.reshape(1, -1), gkv.reshape(1, -1), wq_p, wkv_p)


def _mla_weights(w_uq, w_ukv):
    half = MLA_ROPE // 2
    wq = w_uq.reshape(MLA_Q_RANK, MLA_HEADS, MLA_NOPE + MLA_ROPE)
    pe = wq[..., MLA_NOPE:]
    pe_sw = jnp.concatenate([pe[..., half:], pe[..., :half]], axis=-1)
    wq_p = jnp.concatenate([wq[..., :MLA_NOPE], pe, pe_sw], axis=-1).reshape(MLA_Q_RANK, MLA_HEADS * MLA_DQP)
    wkv = w_ukv.reshape(MLA_KV_RANK, MLA_HEADS, MLA_NOPE + MLA_DV)
    wkv_p = jnp.concatenate([wkv[..., :MLA_NOPE].reshape(MLA_KV_RANK, -1),
                             wkv[..., MLA_NOPE:].reshape(MLA_KV_RANK, -1)], axis=-1)
    return _bf(wq_p), _bf(wkv_p)


def _rope_tables(n_tok, with_pos):
    n_freq = MLA_ROPE // 4
    zeros = jnp.zeros((n_tok, LANE - MLA_ROPE), F32)
    if not with_pos:
        return (jnp.concatenate([jnp.ones((n_tok, MLA_ROPE), F32), zeros], axis=1),
                jnp.zeros((n_tok, LANE), F32))
    t = jnp.arange(n_tok, dtype=jnp.int32)
    row = (t // GRID_W).astype(F32)
    col = (t % GRID_W).astype(F32)
    inv = ROPE_THETA ** (-jnp.arange(n_freq, dtype=F32) / n_freq)
    ang = jnp.concatenate([row[:, None] * inv, col[:, None] * inv], axis=-1)
    cos, sin = jnp.cos(ang), jnp.sin(ang)
    return (jnp.concatenate([cos, cos, zeros], axis=1), jnp.concatenate([-sin, sin, zeros], axis=1))


def _na_kernel(q_ref, k_ref, v_ref, kc_ref, vc_ref, bm_ref, o_ref, *, scale, rows):
    nblk = rows // NA_QROWS
    nq = NA_QROWS * GRID_W
    nkw = NA_KROWS * GRID_W
    kc = _bf(kc_ref[0])
    vc = _bf(vc_ref[0])

    def body(blk, carry):
        q0 = pl.multiple_of(blk * nq, nq)
        kb = jnp.clip(NA_QROWS * blk - NA_WIN_R // 2, 0, rows - NA_KROWS)
        k0 = pl.multiple_of(kb * GRID_W, (NA_WIN_R // 2) * GRID_W)
        pat = jnp.where(blk == 0, 0, jnp.where(blk == nblk - 1, 2, 1))
        q = _bf(q_ref[0, pl.ds(q0, nq), :] * scale)
        kw = _bf(k_ref[0, pl.ds(k0, nkw), :])
        vw = _bf(v_ref[0, pl.ds(k0, nkw), :])
        bm = bm_ref[0, pat]
        s = lax.dot_general(q, kw, NT_DIMS, preferred_element_type=F32)
        s = jnp.where(bm > 0.5 * NEG_INF, s + bm, NEG_INF)
        sc = lax.dot_general(q, kc, NT_DIMS, preferred_element_type=F32)
        m = jnp.maximum(jnp.max(s, axis=-1, keepdims=True), jnp.max(sc, axis=-1, keepdims=True))
        p = jnp.exp(s - m)
        pc = jnp.exp(sc - m)
        l = jnp.sum(p, axis=-1, keepdims=True) + jnp.sum(pc, axis=-1, keepdims=True)
        o = jnp.dot(_bf(p), vw, preferred_element_type=F32) + jnp.dot(_bf(pc), vc, preferred_element_type=F32)
        o_ref[0, pl.ds(q0, nq), :] = (o / l).astype(o_ref.dtype)
        return carry

    lax.fori_loop(0, nblk, body, 0, unroll=4)


def _na_bias_table(rpb, rows):
    nblk = rows // NA_QROWS
    col = np.arange(GRID_W)
    c_start = np.clip(col - NA_WIN_C // 2, 0, GRID_W - NA_WIN_C)
    in_win = (col[None, :] >= c_start[:, None]) & (col[None, :] < c_start[:, None] + NA_WIN_C)
    dc = np.clip(col[None, :] - col[:, None] + NA_WIN_C - 1, 0, 2 * NA_WIN_C - 2)
    dc_onehot = (dc[:, :, None] == np.arange(2 * NA_WIN_C - 1)).astype(np.float32)
    toeplitz = jnp.einsum("hab,qkb->haqk", rpb.astype(F32), dc_onehot, precision=HIGHEST)
    n_dr = 2 * NA_WIN_R - 1
    sel = np.zeros((3, NA_QROWS, NA_KROWS, n_dr), np.float32)
    valid = np.zeros((3, NA_QROWS, NA_KROWS), bool)
    for p, blk in enumerate((0, min(1, nblk - 1), nblk - 1)):
        kb = int(np.clip(NA_QROWS * blk - NA_WIN_R // 2, 0, rows - NA_KROWS))
        for rq in range(NA_QROWS):
            r = NA_QROWS * blk + rq
            r_start = int(np.clip(r - NA_WIN_R // 2, 0, rows - NA_WIN_R))
            for rk in range(NA_KROWS):
                kr = kb + rk
                if r_start <= kr < r_start + NA_WIN_R:
                    sel[p, rq, rk, kr - r + NA_WIN_R - 1] = 1.0
                    valid[p, rq, rk] = True
    bias = jnp.einsum("prka,haqc->hprqkc", sel, toeplitz, precision=HIGHEST)
    mask = valid[None, :, :, None, :, None] & in_win[None, None, None, :, None, :]
    H = rpb.shape[0]
    return jnp.where(mask, bias, NEG_INF).reshape(H, 3, NA_QROWS * GRID_W, NA_KROWS * GRID_W)


def _na(z, zc, rpb):
    B, T, _ = z.shape
    Tc = zc.shape[1]
    rows = T // GRID_W
    assert rows % NA_QROWS == 0 and rows >= NA_KROWS
    bm = _na_bias_table(rpb, rows)
    dh = NA_DH

    def seq(n, col):
        return pl.BlockSpec((1, n, dh), lambda b, h: (b, 0, col // dh + h))

    return pl.pallas_call(
        functools.partial(_na_kernel, scale=dh ** -0.5, rows=rows),
        grid=(B, NA_HEADS),
        in_specs=[seq(T, COL_NQ), seq(T, COL_NK), seq(T, COL_NV), seq(Tc, COL_NK), seq(Tc, COL_NV),
                  pl.BlockSpec((1,) + bm.shape[1:], lambda b, h: (h, 0, 0, 0))],
        out_specs=pl.BlockSpec((1, T, dh), lambda b, h: (b, 0, h)),
        out_shape=jax.ShapeDtypeStruct((B, T, NA_HEADS * dh), BF16),
        compiler_params=_params("arbitrary", "arbitrary"),
        name="neighbourhood_attention",
    )(z, z, z, zc, zc, bm)


def _lru_kernel(g_ref, x_ref, gc_ref, xc_ref, cw_ref, cb_ref, wa_ref, ba_ref, wx_ref, bx_ref, lam_ref,
                y_ref, yc_ref, a_scr, u_scr, h_scr):
    row8 = lax.broadcasted_iota(jnp.int32, (SUBLANE, LANE), 0)

    def scan_pair(n_tok, h0f, h0b):
        nb = n_tok // SUBLANE

        def body(i, carry):
            hf, hb = carry
            rf = pl.multiple_of(i * SUBLANE, SUBLANE)
            rb = pl.multiple_of((nb - 1 - i) * SUBLANE, SUBLANE)
            A, U = a_scr[0, pl.ds(rf, SUBLANE), :], u_scr[0, pl.ds(rf, SUBLANE), :]
            Ab, Ub = a_scr[1, pl.ds(rb, SUBLANE), :], u_scr[1, pl.ds(rb, SUBLANE), :]
            for s in (1, 2, 4):
                m = row8 >= s
                U = jnp.where(m, A * pltpu.roll(U, s, 0) + U, U)
                A = jnp.where(m, A * pltpu.roll(A, s, 0), A)
                mb = row8 < SUBLANE - s
                Ub = jnp.where(mb, Ab * pltpu.roll(Ub, SUBLANE - s, 0) + Ub, Ub)
                Ab = jnp.where(mb, Ab * pltpu.roll(Ab, SUBLANE - s, 0), Ab)
            hbf = A * hf + U
            hbb = Ab * hb + Ub
            h_scr[0, pl.ds(rf, SUBLANE), :] = hbf
            h_scr[1, pl.ds(rb, SUBLANE), :] = hbb
            return hbf[SUBLANE - 1:SUBLANE, :], hbb[0:1, :]

        return lax.fori_loop(0, nb, body, (h0f, h0b), unroll=4)

    def run(gate_ref, zx_ref, out_ref, n_tok, h0f, h0b):
        x = zx_ref[0]
        t = lax.broadcasted_iota(jnp.int32, x.shape, 0)
        w = cw_ref[...]
        xc = (w[0:1] * jnp.where(t >= 2, pltpu.roll(x, 2, 0), 0.0)
              + w[1:2] * jnp.where(t >= 1, pltpu.roll(x, 1, 0), 0.0)
              + w[2:3] * x
              + w[3:4] * jnp.where(t < n_tok - 1, pltpu.roll(x, n_tok - 1, 0), 0.0)) + cb_ref[...]
        xcb = _bf(xc)
        for d in range(2):
            r = jax.nn.sigmoid(jnp.dot(xcb, wa_ref[d, 0], preferred_element_type=F32) + ba_ref[d:d + 1, :])
            i = jax.nn.sigmoid(jnp.dot(xcb, wx_ref[d, 0], preferred_element_type=F32) + bx_ref[d:d + 1, :])
            nl = -lam_ref[d:d + 1, :]
            softplus = jnp.maximum(nl, 0.0) + jnp.log1p(jnp.exp(-jnp.abs(nl)))
            log_a = -LRU_C * r * softplus
            a = jnp.exp(log_a)
            a_scr[d, 0:n_tok, :] = a
            u_scr[d, 0:n_tok, :] = jnp.sqrt(1.0 - jnp.exp(2.0 * log_a)) * i * xc
        hf, hb = scan_pair(n_tok, h0f, h0b)
        h = h_scr[0, 0:n_tok, :] + h_scr[1, 0:n_tok, :]
        out_ref[0] = (_gelu_tanh(gate_ref[0]) * h).astype(out_ref.dtype)
        return hf, hb

    zero = jnp.zeros((1, LANE), F32)
    hf, hb = run(gc_ref, xc_ref, yc_ref, xc_ref.shape[1], zero, zero)
    run(g_ref, x_ref, y_ref, x_ref.shape[1], hf, hb)


def _lru(z, zc, conv_w, conv_b, wa, ba, wx, bx, lam):
    B, T, _ = z.shape
    Tc = zc.shape[1]
    bw = LRU_BW

    def seq(n, col):
        return pl.BlockSpec((1, n, bw), lambda b, j: (b, 0, col // bw + j))

    vec2 = pl.BlockSpec((2, bw), lambda b, j: (0, j))
    mat = pl.BlockSpec((2, 1, bw, bw), lambda b, j: (0, j, 0, 0))
    out = lambda n: pl.BlockSpec((1, n, bw), lambda b, j: (b, 0, j))
    return pl.pallas_call(
        _lru_kernel,
        grid=(B, LRU_BLOCKS),
        in_specs=[seq(T, COL_LG), seq(T, COL_LX), seq(Tc, COL_LG), seq(Tc, COL_LX),
                  pl.BlockSpec((CONV_W, bw), lambda b, j: (0, j)), pl.BlockSpec((1, bw), lambda b, j: (0, j)),
                  mat, vec2, mat, vec2, vec2],
        out_specs=[out(T), out(Tc)],
        out_shape=[jax.ShapeDtypeStruct((B, T, LRU_WIDTH), BF16), jax.ShapeDtypeStruct((B, Tc, LRU_WIDTH), BF16)],
        scratch_shapes=[pltpu.VMEM((2, T, bw), F32)] * 3,
        compiler_params=_params("arbitrary", "arbitrary"),
        name="rg_lru",
    )(z, z, zc, zc, conv_w, conv_b.reshape(1, -1), _bf(wa), ba, _bf(wx), bx, lam)


def _gla_kernel(q_ref, k_ref, v_ref, g_ref, d_ref, qc_ref, kc_ref, vc_ref, gc_ref, dc_ref,
                wa_ref, ba_ref, ng_ref, y_ref, yc_ref, b_scr, s_scr):
    C = GLA_CHUNK
    ri = lax.broadcasted_iota(jnp.int32, (C, C), 0)
    ci = lax.broadcasted_iota(jnp.int32, (C, C), 1)
    keep = (ri >= ci, ri <= ci)
    tri = (keep[0].astype(F32), keep[1].astype(F32))
    scale = GLA_DK ** -0.5

    def chunk_rows(c):
        return pl.ds(pl.multiple_of(c * C, C), C)

    def run(refs, gate_ref, dec_ref, out_ref, n_tok, st_f, st_b):
        qr, kr, vr = refs
        n = n_tok // C
        dec = dec_ref[0]
        for d in range(2):
            x = jnp.dot(dec, wa_ref[d], preferred_element_type=F32, precision=HIGHEST) + ba_ref[d:d + 1, :]
            log_sig = jnp.minimum(x, 0.0) - jnp.log1p(jnp.exp(-jnp.abs(x)))
            b_scr[d, 0:n_tok, :] = log_sig * (1.0 / GLA_TAU)

        def cumulate(c, carry):
            rows = chunk_rows(c)
            for d in range(2):
                b_scr[d, rows, :] = jnp.dot(tri[d], b_scr[d, rows, :], preferred_element_type=F32,
                                            precision=HIGHEST)
            return carry

        lax.fori_loop(0, n, cumulate, 0, unroll=8)

        def advance(i, states):
            out = []
            for d, st in enumerate(states):
                c = i if d == 0 else n - 1 - i
                rows = chunk_rows(c)
                b = b_scr[d, rows, :]
                b_tot = b[C - 1:C, :] if d == 0 else b[0:1, :]
                ke = _bf(kr[0, rows, :] * jnp.exp(b_tot - b))
                s_scr[d, c] = st.astype(s_scr.dtype)
                out.append(st * jnp.exp(b_tot)
                           + lax.dot_general(_bf(vr[0, rows, :]), ke, TN_DIMS, preferred_element_type=F32))
            return tuple(out)

        st_f, st_b = lax.fori_loop(0, n, advance, (st_f, st_b), unroll=8)

        def emit(c, carry):
            rows = chunk_rows(c)
            q = qr[0, rows, :] * scale
            k = kr[0, rows, :]
            v = _bf(vr[0, rows, :])
            o = None
            for d in range(2):
                b = b_scr[d, rows, :]
                qd = _bf(q * jnp.exp(b))
                att = lax.dot_general(qd, _bf(k * jnp.exp(-b)), NT_DIMS, preferred_element_type=F32)
                att = jnp.where(keep[d], att, 0.0)
                od = (jnp.dot(_bf(att), v, preferred_element_type=F32)
                      + lax.dot_general(qd, s_scr[d, c], NT_DIMS, preferred_element_type=F32))
                o = od if o is None else o + od
            o = o * lax.rsqrt(jnp.mean(o * o, axis=-1, keepdims=True) + EPS) * ng_ref[...]
            gate = gate_ref[0, rows, :]
            out_ref[0, rows, :] = (o * (gate * jax.nn.sigmoid(gate))).astype(out_ref.dtype)
            return carry

        lax.fori_loop(0, n, emit, 0, unroll=4)
        return st_f, st_b

    zero = jnp.zeros((GLA_DV, GLA_DK), F32)
    st_f, st_b = run((qc_ref, kc_ref, vc_ref), gc_ref, dc_ref, yc_ref, qc_ref.shape[1], zero, zero)
    run((q_ref, k_ref, v_ref), g_ref, d_ref, y_ref, q_ref.shape[1], st_f, st_b)


def _gla(z, zc, wa2, ba, norm_g):
    B, T, _ = z.shape
    Tc = zc.shape[1]
    wa_p = jnp.zeros((2, LANE, GLA_HEADS * GLA_DK), F32)
    wa_p = wa_p.at[0, :GLA_RANK].set(wa2[0]).at[1, GLA_RANK:2 * GLA_RANK].set(wa2[1])

    def seq(n, col, w):
        return pl.BlockSpec((1, n, w), lambda b, h: (b, 0, col // w + h), pipeline_mode=pl.Buffered(1))

    def dec(n):
        return pl.BlockSpec((1, n, LANE), lambda b, h: (b, 0, COL_DEC // LANE), pipeline_mode=pl.Buffered(1))

    def ins(n):
        return [seq(n, COL_GQ, GLA_DK), seq(n, COL_GK, GLA_DK), seq(n, COL_GV, GLA_DV), seq(n, COL_GG, GLA_DV), dec(n)]

    out = lambda n: pl.BlockSpec((1, n, GLA_DV), lambda b, h: (b, 0, h))
    return pl.pallas_call(
        _gla_kernel,
        grid=(B, GLA_HEADS),
        in_specs=ins(T) + ins(Tc) + [
            pl.BlockSpec((2, LANE, GLA_DK), lambda b, h: (0, 0, h)),
            pl.BlockSpec((2, GLA_DK), lambda b, h: (0, h)),
            pl.BlockSpec((1, GLA_DV), lambda b, h: (0, 0)),
        ],
        out_specs=[out(T), out(Tc)],
        out_shape=[jax.ShapeDtypeStruct((B, T, BRANCH_W), BF16), jax.ShapeDtypeStruct((B, Tc, BRANCH_W), BF16)],
        scratch_shapes=[pltpu.VMEM((2, T, GLA_DK), F32), pltpu.VMEM((2, T // GLA_CHUNK, GLA_DV, GLA_DK), BF16)],
        compiler_params=_params("arbitrary", "arbitrary"),
        name="gla",
    )(z, z, z, z, z, zc, zc, zc, zc, zc, wa_p, ba, norm_g.reshape(1, -1))


def _permute_mix_weight(w_rows):
    o = np.cumsum((0,) + MIX_SIZES)
    gq, gk, gv, gg, af, ab, cq, ckv, kr, lg, lx, nq, nk, nv = [w_rows[o[i]:o[i + 1]] for i in range(len(MIX_SIZES))]
    half = MLA_ROPE // 2
    kr_sw = jnp.concatenate([kr[half:], kr[:half]], axis=0)
    pad = jnp.zeros((MIXP - (COL_DEC + 2 * GLA_RANK), w_rows.shape[1]), w_rows.dtype)
    return _bf(jnp.concatenate([gv, gg, lg, lx, nq, nk, nv, gq, gk, cq, ckv, kr, kr_sw, af, ab, pad], axis=0))


def kernel(x, c, ctx, c_ctx, w_ada, b_ada, norm1_g, norm2_g, w_in, gla_wa2, gla_ba, gla_norm_g, mla_q_norm_g, mla_w_uq, mla_kv_norm_g, mla_w_ukv, lru_conv_w, lru_conv_b, lru_wa, lru_ba, lru_wx, lru_bx, lru_lambda, na_rpb, w_branch, w_out, peer_wq, peer_keys, peer_u, peer_v, final_norm_g):
    B, T, D = x.shape
    Tc = ctx.shape[1]
    L = w_ada.shape[0]
    rope_l = _rope_tables(T, True)
    rope_c = _rope_tables(Tc, False)

    n_rows = -(-(B + 1) // SUBLANE) * SUBLANE
    cc = jnp.zeros((n_rows, D), F32).at[:B].set(c).at[B].set(c_ctx)
    mods = _ada(cc, w_ada, b_ada)

    xc = ctx
    pe_l = pe_c = gate_l = gate_c = None
    pu = _bf(peer_u)
    pv = _bf(peer_v)
    w_in_rows = _bf(jnp.swapaxes(w_in, 1, 2))
    wb = _bf(w_branch)
    wo = _bf(w_out)
    wq = jnp.swapaxes(_bf(peer_wq).reshape(L, D, PEER_HEADS, PEER_DQ), 1, 2)
    for l in range(L):
        update_ctx = l < L - 1
        ml = jnp.split(mods[l, :B], 6, axis=-1)
        mc = jnp.split(jnp.broadcast_to(mods[l, B], (B, 6 * D)), 6, axis=-1)
        w_mix = _permute_mix_weight(w_in_rows[l, :MIX_COLS])
        mla_wq, mla_wkv = _mla_weights(mla_w_uq[l], mla_w_ukv[l])

        xn, h = _norm(x, norm1_g[l], delta=pe_l, gate=gate_l, shift=ml[0], scale=ml[1])
        x = x if xn is None else xn
        xcn, hc = _norm(xc, norm1_g[l], delta=pe_c, gate=gate_c, shift=mc[0], scale=mc[1])
        xc = xc if xcn is None else xcn

        h2d = h.reshape(B * T, D)
        hc2d = hc.reshape(B * Tc, D)
        z = _matmul(h2d, w_mix, out_dtype=F32, b_rows=True).reshape(B, T, MIXP)
        zc = _matmul(hc2d, w_mix, out_dtype=F32, b_rows=True).reshape(B, Tc, MIXP)

        y_gla, yc_gla = _gla(z, zc, gla_wa2[l], gla_ba[l], gla_norm_g[l])
        y_lru, yc_lru = _lru(z, zc, lru_conv_w[l], lru_conv_b[l], lru_wa[l], lru_ba[l], lru_wx[l], lru_bx[l],
                             lru_lambda[l])
        ql, kl, vl = _mla_prep(z, *rope_l, mla_q_norm_g[l], mla_kv_norm_g[l], mla_wq, mla_wkv)
        qc, kc, vc = _mla_prep(zc, *rope_c, mla_q_norm_g[l], mla_kv_norm_g[l], mla_wq, mla_wkv)
        mla_args = dict(heads=MLA_HEADS, dq=MLA_DQP, dv=MLA_DV, scale=(MLA_NOPE + MLA_ROPE) ** -0.5)
        y_mla = _attention(ql, 0, kc, 0, vc, 0, kl, 0, vl, 0, **mla_args)
        y_na = _na(z, zc, na_rpb[l])
        ys_l = [y_gla, y_mla, y_lru, y_na]
        if update_ctx:
            yc_mla = _attention(qc, 0, kc, 0, vc, 0, **mla_args)
            yc_na = _attention(zc, COL_NQ, zc, COL_NK, zc, COL_NV, heads=NA_HEADS, dq=NA_DH, dv=NA_DH,
                               scale=NA_DH ** -0.5)
            ys_c = [yc_gla, yc_mla, yc_lru, yc_na]

        def channel_mix(xs, hs, ys_s, m, n_tok):
            M = B * n_tok
            gates = _matmul(hs, w_in_rows, out_dtype=BF16, act="sigmoid", layer=l, b_rows=True,
                            row0=MIX_COLS, n_out=N_BRANCH * D)
            mrg = _merge([y.reshape(M, BRANCH_W) for y in ys_s], gates, wb, l)
            xs = _matmul(mrg, wo, out_dtype=F32, res=xs.reshape(M, D), mod=m[2],
                         rows_per_batch=n_tok, layer=l).reshape(B, n_tok, D)
            _, h2 = _norm(xs, norm2_g[l], shift=m[3], scale=m[4])
            pe = _peer(h2.reshape(M, D), wq, peer_keys[l], pu, pv, l).reshape(B, n_tok, D)
            return xs, pe

        x, pe_l = channel_mix(x, h2d, ys_l, ml, T)
        gate_l = ml[5]
        if update_ctx:
            xc, pe_c = channel_mix(xc, hc2d, ys_c, mc, Tc)
            gate_c = mc[5]
        else:
            pe_c = gate_c = None
    _, out = _norm(x, final_norm_g, delta=pe_l, gate=gate_l, out_dtype=F32, emit_x=False)
    return out
```
